```python
import math
import jax
import jax.numpy as jnp
from jax import lax
import numpy as np

D_MODEL = 1024
BATCH = 16
SEQ = 4096
DEPTH = 4

GRID_W = 64
CTX_LEN = 256
EPS = 1e-6
ROPE_THETA = 10000.0
Q_BLOCK = 128

BRANCH_WIDTH = D_MODEL // 2
N_BRANCH = 3
DA_HEADS = 4
DA_V_DIM = BRANCH_WIDTH // DA_HEADS
DA_QK_DIM = DA_V_DIM // 2
POOL_WINDOWS = (2, 4, 8, 16)
POOL_GROUPS = 4
POOL_GROUP = BRANCH_WIDTH // POOL_GROUPS
WA_HEAD_DIM = 64
WA_HEADS = BRANCH_WIDTH // WA_HEAD_DIM
WA_KV_HEADS = 2
WA_GROUP = WA_HEADS // WA_KV_HEADS
WA_WINDOW = 128
WA_BLOCK = 128
D_FF = ((8 * D_MODEL + 3 * 256 - 1) // (3 * 256)) * 256
IN_SPLITS = (
    DA_HEADS * 2 * DA_QK_DIM,
    DA_HEADS * 2 * DA_QK_DIM,
    DA_HEADS * DA_V_DIM,
    BRANCH_WIDTH,
    WA_HEADS * WA_HEAD_DIM,
    WA_KV_HEADS * WA_HEAD_DIM,
    WA_KV_HEADS * WA_HEAD_DIM,
    N_BRANCH * D_MODEL,
)
IN_WIDTH = (3 * DA_HEADS * 2 * DA_QK_DIM - DA_HEADS * 2 * DA_QK_DIM + DA_HEADS * DA_V_DIM
            + BRANCH_WIDTH + WA_HEADS * WA_HEAD_DIM + 2 * WA_KV_HEADS * WA_HEAD_DIM
            + N_BRANCH * D_MODEL)

kernel_name = 'hybrid_diffattn_pool_swa_adaln_block'


def rms_norm(x, g):
    xf = x.astype(jnp.float32)
    y = xf * lax.rsqrt(jnp.mean(xf * xf, axis=-1, keepdims=True) + EPS)
    return (y * g.astype(jnp.float32)).astype(x.dtype)


def modulate(x, g, shift, scale):
    return rms_norm(x, g) * (1 + scale) + shift


def axial_rope_tables(n, dim):
    rows = n // GRID_W
    row = jnp.repeat(jnp.arange(rows, dtype=jnp.float32), GRID_W)
    col = jnp.tile(jnp.arange(GRID_W, dtype=jnp.float32), rows)
    n_freq = dim // 4
    inv = ROPE_THETA ** (-jnp.arange(n_freq, dtype=jnp.float32) / n_freq)
    ang = jnp.concatenate([row[:, None] * inv, col[:, None] * inv], axis=-1)
    return jnp.cos(ang), jnp.sin(ang)


def apply_rope(x, cos, sin):
    shp = (cos.shape[0],) + (1,) * (x.ndim - 3) + (cos.shape[1],)
    cs = cos.reshape(shp).astype(x.dtype)
    sn = sin.reshape(shp).astype(x.dtype)
    x1 = x[..., 0::2]
    x2 = x[..., 1::2]
    return jnp.stack([x1 * cs - x2 * sn, x1 * sn + x2 * cs], axis=-1).reshape(x.shape)


def split_mixer_inputs(z):
    b, n = z.shape[:2]
    parts = []
    start = 0
    for w in IN_SPLITS:
        parts.append(z[..., start:start + w])
        start += w
    qa, ka, va, u, qw, kw, vw, gates = parts
    return (qa.reshape(b, n, DA_HEADS, 2, DA_QK_DIM),
            ka.reshape(b, n, DA_HEADS, 2, DA_QK_DIM),
            va.reshape(b, n, DA_HEADS, DA_V_DIM),
            u,
            qw.reshape(b, n, WA_HEADS, WA_HEAD_DIM),
            kw.reshape(b, n, WA_KV_HEADS, WA_HEAD_DIM),
            vw.reshape(b, n, WA_KV_HEADS, WA_HEAD_DIM),
            gates)


def diff_attention(q, k, v, lam, g_sub, lam_init):
    b, nq = q.shape[:2]
    nb = nq // Q_BLOCK
    qb = q.reshape((b, nb, Q_BLOCK) + q.shape[2:]).swapaxes(0, 1)
    scale = DA_QK_DIM ** -0.5

    def one(qblk):
        s = jnp.einsum('bqhsd,bkhsd->bhsqk', qblk, k).astype(jnp.float32) * scale
        p = jax.nn.softmax(s, axis=-1)
        a = p[:, :, 0] - lam * p[:, :, 1]
        return jnp.einsum('bhqk,bkhe->bqhe', a.astype(v.dtype), v)

    o = lax.map(one, qb).swapaxes(0, 1).reshape(b, nq, DA_HEADS, DA_V_DIM)
    o = rms_norm(o, g_sub) * (1 - lam_init)
    return o.reshape(b, nq, DA_HEADS * DA_V_DIM)


def multiscale_pool(u, w_pool, pool_scale):
    b, n, _ = u.shape
    uf = u.reshape(b, n, POOL_GROUPS, POOL_GROUP).astype(jnp.float32)
    csum = jnp.concatenate([jnp.zeros((b, 1, POOL_GROUPS, POOL_GROUP), jnp.float32),
                            lax.cumsum(uf, axis=1)], axis=1)
    w = jnp.array(POOL_WINDOWS, dtype=jnp.int32)
    t = jnp.arange(n, dtype=jnp.int32)[:, None]
    lo = jnp.clip(t - w // 2, 0, n)
    hi = jnp.clip(t + w - w // 2, 0, n)
    grp = jnp.arange(POOL_GROUPS)[None, :]
    wsum = csum[:, hi, grp] - csum[:, lo, grp]
    count = (hi - lo).astype(jnp.float32)[..., None]
    pooled = (wsum / count - uf).astype(u.dtype)
    y = jnp.einsum('bngc,gcd->bngd', pooled, w_pool).reshape(b, n, BRANCH_WIDTH)
    return y * pool_scale


def sink_softmax(s, sink):
    sk = sink.astype(jnp.float32)[:, :, None, None]
    m = jnp.maximum(jnp.max(s, axis=-1, keepdims=True), sk)
    e = jnp.exp(s - m)
    return e / (jnp.sum(e, axis=-1, keepdims=True) + jnp.exp(sk - m))


def window_sink_attention(q, k, v, kc, vc, sink):
    b, n = q.shape[:2]
    nb = n // WA_BLOCK
    pad = ((0, 0), (WA_BLOCK, WA_BLOCK), (0, 0), (0, 0))
    kp = jnp.pad(k, pad)
    vp = jnp.pad(v, pad)
    qb = q.reshape(b, nb, WA_BLOCK, WA_KV_HEADS, WA_GROUP, WA_HEAD_DIM).swapaxes(0, 1)
    sink_g = sink.reshape(WA_KV_HEADS, WA_GROUP)
    scale = WA_HEAD_DIM ** -0.5
    band = 3 * WA_BLOCK
    rel = jnp.arange(band)[None, :] - WA_BLOCK - jnp.arange(WA_BLOCK)[:, None]
    band_ok = jnp.abs(rel) <= WA_WINDOW

    def one(args):
        i, qblk = args
        kb = lax.dynamic_slice_in_dim(kp, i * WA_BLOCK, band, axis=1)
        vb = lax.dynamic_slice_in_dim(vp, i * WA_BLOCK, band, axis=1)
        kpos = (i - 1) * WA_BLOCK + jnp.arange(band)
        ok = band_ok & ((kpos >= 0) & (kpos < n))[None, :]
        s_band = jnp.einsum('bqkgd,bjkd->bkgqj', qblk, kb).astype(jnp.float32) * scale
        s_band = jnp.where(ok, s_band, -jnp.inf)
        s_ctx = jnp.einsum('bqkgd,bjkd->bkgqj', qblk, kc).astype(jnp.float32) * scale
        p = sink_softmax(jnp.concatenate([s_band, s_ctx], axis=-1), sink_g)
        vv = jnp.concatenate([vb, vc], axis=1)
        return jnp.einsum('bkgqj,bjkd->bqkgd', p.astype(v.dtype), vv)

    o = lax.map(one, (jnp.arange(nb, dtype=jnp.int32), qb))
    return o.swapaxes(0, 1).reshape(b, n, WA_HEADS * WA_HEAD_DIM)


def context_sink_attention(q, k, v, sink):
    b, n = q.shape[:2]
    qg = q.reshape(b, n, WA_KV_HEADS, WA_GROUP, WA_HEAD_DIM)
    s = jnp.einsum('bqkgd,bjkd->bkgqj', qg, k).astype(jnp.float32) * WA_HEAD_DIM ** -0.5
    p = sink_softmax(s, sink.reshape(WA_KV_HEADS, WA_GROUP))
    o = jnp.einsum('bkgqj,bjkd->bqkgd', p.astype(v.dtype), v)
    return o.reshape(b, n, WA_HEADS * WA_HEAD_DIM)


def merge_branches(ya, yb, yc, gates, w_branch, w_out):
    b, n = gates.shape[:2]
    g = jax.nn.sigmoid(gates.reshape(b, n, N_BRANCH, D_MODEL))
    m = (g[:, :, 0] * (ya @ w_branch[0]) + g[:, :, 1] * (yb @ w_branch[1])
         + g[:, :, 2] * (yc @ w_branch[2]))
    return m @ w_out


def swiglu(h, w_in, w_out):
    gate, up = jnp.split(h @ w_in, 2, axis=-1)
    return (jax.nn.silu(gate) * up) @ w_out


def setup_inputs(seed: int = 0) -> dict:
    key = jax.random.key(seed)
    ks = jax.random.split(key, 17)

    def nrm(k, shape, s):
        return jax.random.normal(k, shape, jnp.float32) * s

    return {
        'x': nrm(ks[0], (BATCH, SEQ, D_MODEL), 1.0),
        'c': nrm(ks[1], (BATCH, D_MODEL), 1.0),
        'ctx': nrm(ks[2], (BATCH, CTX_LEN, D_MODEL), 1.0),
        'c_ctx': nrm(ks[3], (D_MODEL,), 1.0),
        'w_ada': nrm(ks[4], (DEPTH, D_MODEL, 6 * D_MODEL), 0.5 * D_MODEL ** -0.5),
        'b_ada': nrm(ks[5], (DEPTH, 6 * D_MODEL), 0.02),
        'norm_g': 1.0 + nrm(ks[6], (DEPTH, 4, D_MODEL), 0.02),
        'w_in': nrm(ks[7], (DEPTH, D_MODEL, IN_WIDTH), D_MODEL ** -0.5),
        'da_lambda': nrm(ks[8], (DEPTH, 4, DA_QK_DIM), 0.1),
        'da_subln_g': 1.0 + nrm(ks[9], (DEPTH, DA_V_DIM), 0.02),
        'pool_w': nrm(ks[10], (DEPTH, POOL_GROUPS, POOL_GROUP, POOL_GROUP), POOL_GROUP ** -0.5),
        'pool_scale': 1.0 + nrm(ks[11], (DEPTH, BRANCH_WIDTH), 0.02),
        'wa_sink': nrm(ks[12], (DEPTH, WA_HEADS), 0.5),
        'w_branch': nrm(ks[13], (DEPTH, N_BRANCH, BRANCH_WIDTH, D_MODEL), BRANCH_WIDTH ** -0.5),
        'w_out': nrm(ks[14], (DEPTH, D_MODEL, D_MODEL), D_MODEL ** -0.5),
        'w_ffn_in': nrm(ks[15], (DEPTH, D_MODEL, 2 * D_FF), D_MODEL ** -0.5),
        'w_ffn_out': nrm(ks[16], (DEPTH, D_FF, D_MODEL), D_FF ** -0.5),
    }


def reference(x, c, ctx, c_ctx, w_ada, b_ada, norm_g, w_in, da_lambda, da_subln_g,
              pool_w, pool_scale, wa_sink, w_branch, w_out, w_ffn_in, w_ffn_out):
    n = x.shape[1]
    cos_a, sin_a = axial_rope_tables(n, DA_QK_DIM)
    cos_w, sin_w = axial_rope_tables(n, WA_HEAD_DIM)
    xc = ctx
    for l in range(DEPTH):
        mod_x = jnp.split((jax.nn.silu(c) @ w_ada[l] + b_ada[l])[:, None, :], 6, axis=-1)
        mod_c = jnp.split((jax.nn.silu(c_ctx) @ w_ada[l] + b_ada[l])[None, None, :], 6, axis=-1)
        lam_init = 0.8 - 0.6 * math.exp(-0.3 * l)
        lp = da_lambda[l].astype(jnp.float32)
        lam = jnp.exp(jnp.sum(lp[0] * lp[1])) - jnp.exp(jnp.sum(lp[2] * lp[3])) + lam_init

        hc = modulate(xc, norm_g[l, 0], mod_c[0], mod_c[1])
        qac, kac, vac, uc, qwc, kwc, vwc, gatec = split_mixer_inputs(hc @ w_in[l])

        h = modulate(x, norm_g[l, 0], mod_x[0], mod_x[1])
        qa, ka, va, u, qw, kw, vw, gate = split_mixer_inputs(h @ w_in[l])
        qa = apply_rope(qa, cos_a, sin_a)
        ka = apply_rope(ka, cos_a, sin_a)
        qw = apply_rope(qw, cos_w, sin_w)
        kw = apply_rope(kw, cos_w, sin_w)
        ya = diff_attention(qa, jnp.concatenate([ka, kac], axis=1),
                            jnp.concatenate([va, vac], axis=1), lam, da_subln_g[l], lam_init)
        yb = multiscale_pool(u, pool_w[l], pool_scale[l])
        yc = window_sink_attention(qw, kw, vw, kwc, vwc, wa_sink[l])
        mix = merge_branches(ya, yb, yc, gate, w_branch[l], w_out[l])
        x = x + mod_x[2] * rms_norm(mix, norm_g[l, 1])
        h2 = modulate(x, norm_g[l, 2], mod_x[3], mod_x[4])
        x = x + mod_x[5] * rms_norm(swiglu(h2, w_ffn_in[l], w_ffn_out[l]), norm_g[l, 3])

        if l < DEPTH - 1:
            yac = diff_attention(qac, kac, vac, lam, da_subln_g[l], lam_init)
            ybc = multiscale_pool(uc, pool_w[l], pool_scale[l])
            ycc = context_sink_attention(qwc, kwc, vwc, wa_sink[l])
            mixc = merge_branches(yac, ybc, ycc, gatec, w_branch[l], w_out[l])
            xc = xc + mod_c[2] * rms_norm(mixc, norm_g[l, 1])
            h2c = modulate(xc, norm_g[l, 2], mod_c[3], mod_c[4])
            xc = xc + mod_c[5] * rms_norm(swiglu(h2c, w_ffn_in[l], w_ffn_out[l]), norm_g[l, 3])
    return x
```

```python
import functools
import math

import numpy as np
import jax
import jax.numpy as jnp
from jax import lax
from jax.experimental import pallas as pl
from jax.experimental.pallas import tpu as pltpu

F32 = jnp.float32
BF16 = jnp.bfloat16

D_MODEL = 1024
CTX_LEN = 256
GRID_W = 64
EPS = 1e-6
ROPE_THETA = 10000.0
BRANCH_WIDTH = D_MODEL // 2
N_BRANCH = 3
DA_HEADS = 4
DA_V_DIM = BRANCH_WIDTH // DA_HEADS
DA_QK_DIM = DA_V_DIM // 2
POOL_WINDOWS = (2, 4, 8, 16)
POOL_GROUPS = 4
POOL_GROUP = BRANCH_WIDTH // POOL_GROUPS
WA_HEAD_DIM = 64
WA_HEADS = BRANCH_WIDTH // WA_HEAD_DIM
WA_KV_HEADS = 2
WA_GROUP = WA_HEADS // WA_KV_HEADS
WA_WINDOW = 128
WA_BLOCK = 128
D_FF = ((8 * D_MODEL + 3 * 256 - 1) // (3 * 256)) * 256
PROJ_WIDTH = 5 * BRANCH_WIDTH + 2 * WA_KV_HEADS * WA_HEAD_DIM
GATE_WIDTH = N_BRANCH * D_MODEL

LANES = 128
BF16_SUBLANES = 16
VMEM_LIMIT = 56 * 1024 * 1024

TM = 512
TQ = 256
CK = 512
POOL_SUB = 256
POOL_HALO = BF16_SUBLANES


def _params(n_axes):
    return pltpu.CompilerParams(dimension_semantics=("arbitrary",) * n_axes,
                                vmem_limit_bytes=VMEM_LIMIT)


def _const_spec(shape):
    nd = len(shape)
    return pl.BlockSpec(shape, lambda *_: (0,) * nd, pipeline_mode=pl.Buffered(1))


def _rms(x, g):
    return x * lax.rsqrt(jnp.mean(x * x, axis=-1, keepdims=True) + EPS) * g


def _modulate(x, g, shift, scale):
    return _rms(x, g) * (1.0 + scale) + shift


def _mod_kernel(c_ref, w_ref, b_ref, o_ref):
    c = c_ref[...]
    a = c * (1.0 / (1.0 + jnp.exp(-c)))
    o_ref[0] = jnp.dot(a, w_ref[0], preferred_element_type=F32,
                       precision=lax.Precision.HIGHEST) + b_ref[0]


def _mod_call(cvec, w_ada, b_ada):
    depth, d, width = w_ada.shape
    rows = cvec.shape[0]
    tn = 1536
    return pl.pallas_call(
        _mod_kernel,
        grid=(depth, width // tn),
        in_specs=[pl.BlockSpec((rows, d), lambda l, j: (0, 0)),
                  pl.BlockSpec((1, d, tn), lambda l, j: (l, 0, j)),
                  pl.BlockSpec((1, 1, tn), lambda l, j: (l, 0, j))],
        out_specs=pl.BlockSpec((1, rows, tn), lambda l, j: (l, 0, j)),
        out_shape=jax.ShapeDtypeStruct((depth, rows, width), F32),
        compiler_params=_params(2),
        name="adaln_mod",
    )(cvec, w_ada, b_ada.reshape(depth, 1, width))


def _proj_kernel(x_ref, mod_ref, g_ref, w_ref, c_ref, s_ref,
                 qa_ref, ka_ref, va_ref, u_ref, qw_ref, kw_ref, vw_ref):
    h = _modulate(x_ref[...], g_ref[...], mod_ref[0, 0:1, :], mod_ref[0, 1:2, :]).astype(BF16)
    cos = c_ref[...]
    sin = s_ref[...]

    def proj(lo, width):
        return jnp.dot(h, w_ref[:, lo:lo + width], preferred_element_type=F32)

    def rope(yj):
        return yj * cos + pltpu.roll(yj, LANES // 2, 1) * sin

    def key_store(ref, y):
        for j in range(y.shape[1] // LANES):
            ref[:, j * LANES:(j + 1) * LANES] = rope(y[:, j * LANES:(j + 1) * LANES]).astype(BF16)

    lane = lax.broadcasted_iota(jnp.int32, (TM, LANES), 1)
    first = (lane & (DA_QK_DIM // 2)) == 0

    def query_store(ref, y, scale):
        for j in range(y.shape[1] // LANES):
            r = rope(y[:, j * LANES:(j + 1) * LANES]) * scale
            ref[:, (2 * j) * LANES:(2 * j + 1) * LANES] = jnp.where(first, r, 0.0).astype(BF16)
            ref[:, (2 * j + 1) * LANES:(2 * j + 2) * LANES] = jnp.where(first, 0.0, r).astype(BF16)

    bw = BRANCH_WIDTH
    query_store(qa_ref, proj(0, bw), DA_QK_DIM ** -0.5)
    key_store(ka_ref, proj(bw, bw))
    va_ref[...] = proj(2 * bw, bw).astype(BF16)
    u_ref[...] = proj(3 * bw, bw).astype(BF16)
    query_store(qw_ref, proj(4 * bw, bw), WA_HEAD_DIM ** -0.5)
    kvw = proj(5 * bw, 2 * LANES)
    key_store(kw_ref, kvw[:, :LANES])
    vw_ref[...] = kvw[:, LANES:].astype(BF16)


def _tile_maps(nb, n, ctx):
    n_lat = nb * n // TM
    per_seq = n // TM

    def mod_map(t):
        return (jnp.where(t < n_lat, t // per_seq, nb), 0, 0)

    def rope_map(t):
        return (jnp.where(t < n_lat, t % per_seq, per_seq), 0)

    return n_lat, mod_map, rope_map


def _proj_call(tok, mod, g, w1, cos, sin, nb, n):
    nt, d = tok.shape
    _, mod_map, rope_map = _tile_maps(nb, n, CTX_LEN)
    row = lambda t: (t, 0)
    spec = lambda w: pl.BlockSpec((TM, w), row)
    sd = lambda w: jax.ShapeDtypeStruct((nt, w), BF16)
    widths = (2 * BRANCH_WIDTH, BRANCH_WIDTH, BRANCH_WIDTH, BRANCH_WIDTH, 2 * BRANCH_WIDTH, LANES, LANES)
    return pl.pallas_call(
        _proj_kernel,
        grid=(nt // TM,),
        in_specs=[pl.BlockSpec((TM, d), row),
                  pl.BlockSpec((1, 6, d), mod_map),
                  _const_spec((1, d)),
                  _const_spec(w1.shape),
                  pl.BlockSpec((TM, LANES), rope_map),
                  pl.BlockSpec((TM, LANES), rope_map)],
        out_specs=[spec(w) for w in widths],
        out_shape=[sd(w) for w in widths],
        compiler_params=_params(1),
        name="in_proj",
    )(tok, mod, g, w1, cos, sin)


def _da_kernel(dl_ref, g_ref, q_ref, kl_ref, kc_ref, vl_ref, vc_ref, o_ref, *,
               lam_init, n_q, n_chunks):
    i = pl.program_id(2)
    lp = dl_ref[...]
    lam = (jnp.exp(jnp.sum(lp[0:1] * lp[1:2], axis=-1, keepdims=True))
           - jnp.exp(jnp.sum(lp[2:3] * lp[3:4], axis=-1, keepdims=True)) + lam_init)

    tq = q_ref.shape[0]
    qs = jnp.concatenate([q_ref[:, :LANES], q_ref[:, LANES:]], axis=0)

    def chunk(carry, k, v):
        m, l, acc = carry
        s = lax.dot_general(qs, k, (((1,), (1,)), ((), ())), preferred_element_type=F32)
        m_new = jnp.maximum(m, jnp.max(s, axis=-1, keepdims=True))
        alpha = jnp.exp(m - m_new)
        e = jnp.exp(s - m_new)
        l = alpha * l + jnp.sum(e, axis=-1, keepdims=True)
        acc = alpha * acc + jnp.dot(e.astype(BF16), v, preferred_element_type=F32)
        return m_new, l, acc

    def finish(carry):
        _, l, acc = carry
        o = acc[:tq] / l[:tq] - lam * (acc[tq:] / l[tq:])
        o_ref[...] = (_rms(o, g_ref[...]) * (1.0 - lam_init)).astype(BF16)

    init = (jnp.full((2 * tq, 1), -jnp.inf, F32), jnp.zeros((2 * tq, 1), F32),
            jnp.zeros((2 * tq, DA_V_DIM), F32))

    @pl.when(i < n_q)
    def _():
        def body(c, carry):
            off = pl.multiple_of(c * CK, CK)
            return chunk(carry, kl_ref[pl.ds(off, CK), :], vl_ref[pl.ds(off, CK), :])
        carry = lax.fori_loop(0, n_chunks, body, init)
        finish(chunk(carry, kc_ref[...], vc_ref[...]))

    @pl.when(i >= n_q)
    def _():
        finish(chunk(init, kc_ref[...], vc_ref[...]))


def _da_call(qa, ka, va, da_lambda, g_sub, lam_init, nb, n, with_ctx):
    nt = qa.shape[0]
    n_q = n // TQ
    ctx0 = nb * n // CTX_LEN
    steps = n_q + (1 if with_ctx else 0)

    def q_map(b, h, i):
        return (jnp.where(i < n_q, b * n_q + i, ctx0 + b), h)

    lat = pl.BlockSpec((n, DA_V_DIM), lambda b, h, i: (b, h))
    ctx = pl.BlockSpec((CTX_LEN, DA_V_DIM), lambda b, h, i: (ctx0 + b, h))
    kern = functools.partial(_da_kernel, lam_init=lam_init, n_q=n_q, n_chunks=n // CK)
    return pl.pallas_call(
        kern,
        grid=(nb, DA_HEADS, steps),
        in_specs=[pl.BlockSpec(da_lambda.shape, lambda b, h, i: (0, 0)),
                  pl.BlockSpec((1, DA_V_DIM), lambda b, h, i: (0, 0)),
                  pl.BlockSpec((TQ, 2 * DA_V_DIM), q_map),
                  lat, ctx, lat, ctx],
        out_specs=pl.BlockSpec((TQ, DA_V_DIM), q_map),
        out_shape=jax.ShapeDtypeStruct((nt, BRANCH_WIDTH), BF16),
        compiler_params=_params(3),
        name="diff_attn",
    )(da_lambda, g_sub, qa, ka, ka, va, va)


def _wa_kernel(sink_ref, q_ref, kp_ref, kc_ref, kn_ref, kx_ref, vp_ref, vc_ref, vn_ref, vx_ref,
               o_ref, *, n_blk, n):
    i = pl.program_id(1)
    rows = q_ref.shape[0]
    lane = lax.broadcasted_iota(jnp.int32, (rows, LANES), 1)
    qs = jnp.concatenate([q_ref[:, k * LANES:(k + 1) * LANES] for k in range(WA_HEADS)], axis=0)

    def attend(k, v, mask):
        s = lax.dot_general(qs, k, (((1,), (1,)), ((), ())), preferred_element_type=F32)
        outs = []
        for blk in range(WA_HEADS):
            head = (blk // 2) + (WA_HEADS // 2) * (blk % 2)
            sk = sink_ref[head]
            sb = s[blk * rows:(blk + 1) * rows]
            if mask is not None:
                sb = jnp.where(mask, sb, -jnp.inf)
            m = jnp.maximum(jnp.max(sb, axis=-1, keepdims=True), sk)
            e = jnp.exp(sb - m)
            den = jnp.sum(e, axis=-1, keepdims=True) + jnp.exp(sk - m)
            outs.append(jnp.dot(e.astype(BF16), v, preferred_element_type=F32) / den)
        half = lane < WA_HEAD_DIM
        for j in range(WA_HEADS // 2):
            o_ref[:, j * LANES:(j + 1) * LANES] = jnp.where(half, outs[2 * j], outs[2 * j + 1]).astype(BF16)

    @pl.when(i < n_blk)
    def _():
        k = jnp.concatenate([kp_ref[...], kc_ref[...], kn_ref[...], kx_ref[...]], axis=0)
        v = jnp.concatenate([vp_ref[...], vc_ref[...], vn_ref[...], vx_ref[...]], axis=0)
        band = 3 * WA_BLOCK
        col = lax.broadcasted_iota(jnp.int32, (rows, band + CTX_LEN), 1)
        qpos = lax.broadcasted_iota(jnp.int32, (rows, band + CTX_LEN), 0)
        rel = col - WA_BLOCK - qpos
        kpos = (i - 1) * WA_BLOCK + col
        ok = (col >= band) | ((jnp.abs(rel) <= WA_WINDOW) & (kpos >= 0) & (kpos < n))
        attend(k, v, ok)

    @pl.when(i >= n_blk)
    def _():
        attend(kx_ref[...], vx_ref[...], None)


def _wa_call(qw, kw, vw, sink, nb, n, with_ctx):
    nt = qw.shape[0]
    n_blk = n // WA_BLOCK
    ctx_blk = CTX_LEN // WA_BLOCK
    ctx0_q = nb * n // WA_BLOCK
    ctx0_k = nb * n // CTX_LEN
    steps = n_blk + (ctx_blk if with_ctx else 0)

    def q_map(b, i):
        return (jnp.where(i < n_blk, b * n_blk + i, ctx0_q + b * ctx_blk + (i - n_blk)), 0)

    def near(delta):
        def m(b, i):
            j = jnp.clip(i + delta, 0, n_blk - 1)
            return (b * n_blk + j, 0)
        return pl.BlockSpec((WA_BLOCK, LANES), m)

    ctx = pl.BlockSpec((CTX_LEN, LANES), lambda b, i: (ctx0_k + b, 0))
    kern = functools.partial(_wa_kernel, n_blk=n_blk, n=n)
    return pl.pallas_call(
        kern,
        grid=(nb, steps),
        in_specs=[pl.BlockSpec(memory_space=pltpu.SMEM),
                  pl.BlockSpec((WA_BLOCK, 2 * BRANCH_WIDTH), q_map),
                  near(-1), near(0), near(1), ctx,
                  near(-1), near(0), near(1), ctx],
        out_specs=pl.BlockSpec((WA_BLOCK, BRANCH_WIDTH), q_map),
        out_shape=jax.ShapeDtypeStruct((nt, BRANCH_WIDTH), BF16),
        compiler_params=_params(2),
        name="window_attn",
    )(sink, qw, kw, kw, kw, kw, vw, vw, vw, vw)


def _pool_band():
    ext = POOL_SUB + 2 * POOL_HALO
    t = np.arange(POOL_SUB)[:, None]
    d = np.arange(ext)[None, :] - POOL_HALO - t
    return np.stack([((d >= -(w // 2)) & (d < w - w // 2)) for w in POOL_WINDOWS]).astype(np.float32)


def _merge_kernel(x_ref, mod_ref, g0_ref, g1_ref, wg_ref, ya_ref, yc_ref, u_ref, up_ref, un_ref,
                  band_ref, pw_ref, ps_ref, wb_ref, wo_ref, o_ref, *, n_lat, n):
    t = pl.program_id(0)
    x = x_ref[...]
    h = _modulate(x, g0_ref[...], mod_ref[0, 0:1, :], mod_ref[0, 1:2, :]).astype(BF16)

    u = u_ref[...]
    is_lat = t < n_lat
    halo_one = jnp.ones((POOL_HALO, POOL_GROUP), BF16)
    halo_zero = jnp.zeros((POOL_HALO, POOL_GROUP), BF16)
    yb_parts = []
    for sb in range(TM // POOL_SUB):
        r0 = sb * POOL_SUB
        prev = up_ref[...] if sb == 0 else u[r0 - POOL_HALO:r0]
        nxt = un_ref[...] if r0 + POOL_SUB == TM else u[r0 + POOL_SUB:r0 + POOL_SUB + POOL_HALO]
        cur = u[r0:r0 + POOL_SUB]
        g_row = t * TM + r0
        prev_ok = is_lat & ((g_row % n) != 0)
        next_ok = is_lat & (((g_row + POOL_SUB) % n) != 0)
        ext = jnp.concatenate([jnp.where(prev_ok, prev, jnp.zeros_like(prev)), cur,
                               jnp.where(next_ok, nxt, jnp.zeros_like(nxt))], axis=0)
        ones = jnp.concatenate([jnp.where(prev_ok, halo_one, halo_zero),
                                jnp.ones((POOL_SUB, POOL_GROUP), BF16),
                                jnp.where(next_ok, halo_one, halo_zero)], axis=0)
        cols = []
        for g in range(POOL_GROUPS):
            sl = slice(g * POOL_GROUP, (g + 1) * POOL_GROUP)
            band = band_ref[g]
            wsum = jnp.dot(band, ext[:, sl], preferred_element_type=F32)
            count = jnp.dot(band, ones, preferred_element_type=F32)
            pooled = (wsum / count - cur[:, sl].astype(F32)).astype(BF16)
            cols.append(jnp.dot(pooled, pw_ref[g], preferred_element_type=F32) * ps_ref[:, sl])
        yb_parts.append(jnp.concatenate(cols, axis=1))
    yb = jnp.concatenate(yb_parts, axis=0).astype(BF16)

    branches = (ya_ref[...], yb, yc_ref[...])
    m = None
    for k in range(N_BRANCH):
        gate = jnp.dot(h, wg_ref[:, k * D_MODEL:(k + 1) * D_MODEL], preferred_element_type=F32)
        gate = 1.0 / (1.0 + jnp.exp(-gate))
        term = gate * jnp.dot(branches[k], wb_ref[k], preferred_element_type=F32)
        m = term if m is None else m + term
    mix = jnp.dot(m.astype(BF16), wo_ref[...], preferred_element_type=F32)
    o_ref[...] = x + mod_ref[0, 2:3, :] * _rms(mix, g1_ref[...])


def _merge_call(tok, mod, g0, g1, wg, ya, yc, u, band, pool_w, pool_scale, wb, wo, nb, n, n_tiles):
    nt, d = tok.shape
    n_lat, mod_map, _ = _tile_maps(nb, n, CTX_LEN)
    row = lambda t: (t, 0)
    per_tile = TM // POOL_HALO
    last = nt // POOL_HALO - 1
    wide = pl.BlockSpec((TM, BRANCH_WIDTH), row)
    kern = functools.partial(_merge_kernel, n_lat=n_lat, n=n)
    return pl.pallas_call(
        kern,
        grid=(n_tiles,),
        in_specs=[pl.BlockSpec((TM, d), row),
                  pl.BlockSpec((1, 6, d), mod_map),
                  _const_spec((1, d)), _const_spec((1, d)),
                  _const_spec(wg.shape),
                  wide, wide, wide,
                  pl.BlockSpec((POOL_HALO, BRANCH_WIDTH), lambda t: (jnp.maximum(t * per_tile - 1, 0), 0)),
                  pl.BlockSpec((POOL_HALO, BRANCH_WIDTH), lambda t: (jnp.minimum((t + 1) * per_tile, last), 0)),
                  _const_spec(band.shape), _const_spec(pool_w.shape), _const_spec((1, BRANCH_WIDTH)),
                  _const_spec(wb.shape), _const_spec(wo.shape)],
        out_specs=pl.BlockSpec((TM, d), row),
        out_shape=jax.ShapeDtypeStruct((n_tiles * TM, d), F32),
        compiler_params=_params(1),
        name="pool_merge",
    )(tok, mod, g0, g1, wg, ya, yc, u, u, u, band, pool_w, pool_scale, wb, wo)


def _ffn_kernel(x_ref, mod_ref, g2_ref, g3_ref, wi_ref, wo_ref, o_ref):
    x = x_ref[...]
    h = _modulate(x, g2_ref[...], mod_ref[0, 3:4, :], mod_ref[0, 4:5, :]).astype(BF16)
    step = 256
    acc = None
    for c in range(D_FF // step):
        gate = jnp.dot(h, wi_ref[:, c * step:(c + 1) * step], preferred_element_type=F32)
        up = jnp.dot(h, wi_ref[:, D_FF + c * step:D_FF + (c + 1) * step], preferred_element_type=F32)
        act = (gate * (1.0 / (1.0 + jnp.exp(-gate))) * up).astype(BF16)
        part = jnp.dot(act, wo_ref[c * step:(c + 1) * step, :], preferred_element_type=F32)
        acc = part if acc is None else acc + part
    o_ref[...] = x + mod_ref[0, 5:6, :] * _rms(acc, g3_ref[...])


def _ffn_call(tok, mod, g2, g3, wi, wo, nb, n):
    nt, d = tok.shape
    _, mod_map, _ = _tile_maps(nb, n, CTX_LEN)
    row = lambda t: (t, 0)
    return pl.pallas_call(
        _ffn_kernel,
        grid=(nt // TM,),
        in_specs=[pl.BlockSpec((TM, d), row),
                  pl.BlockSpec((1, 6, d), mod_map),
                  _const_spec((1, d)), _const_spec((1, d)),
                  _const_spec(wi.shape), _const_spec(wo.shape)],
        out_specs=pl.BlockSpec((TM, d), row),
        out_shape=jax.ShapeDtypeStruct((nt, d), F32),
        compiler_params=_params(1),
        name="swiglu_ffn",
    )(tok, mod, g2, g3, wi, wo)


def _column_order():
    bw = BRANCH_WIDTH
    order = np.arange(PROJ_WIDTH)
    half = DA_QK_DIM // 2
    for base in (0, bw):
        for h in range(DA_HEADS):
            for p in range(2):
                for s in range(2):
                    for i in range(half):
                        order[base + h * DA_V_DIM + p * 64 + s * half + i] = (
                            base + h * DA_V_DIM + s * DA_QK_DIM + 2 * i + p)
    half = WA_HEAD_DIM // 2
    base = 4 * bw
    for j in range(WA_HEADS // 2):
        for p in range(2):
            for g in range(2):
                for i in range(half):
                    order[base + j * LANES + p * 64 + g * half + i] = (
                        base + (j + (WA_HEADS // 2) * g) * WA_HEAD_DIM + 2 * i + p)
    base = 5 * bw
    for p in range(2):
        for kv in range(WA_KV_HEADS):
            for i in range(half):
                order[base + p * 64 + kv * half + i] = base + kv * WA_HEAD_DIM + 2 * i + p
    return order


def _wa_row_order():
    order = np.arange(BRANCH_WIDTH)
    for j in range(WA_HEADS // 2):
        for g in range(2):
            for dd in range(WA_HEAD_DIM):
                order[j * LANES + g * WA_HEAD_DIM + dd] = (j + (WA_HEADS // 2) * g) * WA_HEAD_DIM + dd
    return order


def _rope_tables(n):
    rows = n // GRID_W
    row = jnp.repeat(jnp.arange(rows, dtype=F32), GRID_W)
    col = jnp.tile(jnp.arange(GRID_W, dtype=F32), rows)
    n_freq = DA_QK_DIM // 4
    inv = ROPE_THETA ** (-jnp.arange(n_freq, dtype=F32) / n_freq)
    ang = jnp.concatenate([row[:, None] * inv, col[:, None] * inv], axis=-1)
    cos = jnp.tile(jnp.cos(ang), (1, 4))
    sin = jnp.tile(jnp.sin(ang), (1, 4)) * jnp.where(jnp.arange(LANES) < 64, -1.0, 1.0)
    cos = jnp.concatenate([cos, jnp.ones((TM, LANES), F32)], axis=0)
    sin = jnp.concatenate([sin, jnp.zeros((TM, LANES), F32)], axis=0)
    return cos, sin


def kernel(x, c, ctx, c_ctx, w_ada, b_ada, norm_g, w_in, da_lambda, da_subln_g,
           pool_w, pool_scale, wa_sink, w_branch, w_out, w_ffn_in, w_ffn_out):
    nb, n, d = x.shape
    depth = w_ada.shape[0]
    assert d == D_MODEL and ctx.shape[1] == CTX_LEN
    assert n % TM == 0 and (nb * CTX_LEN) % TM == 0 and n % CK == 0 and n % GRID_W == 0
    assert WA_HEAD_DIM == DA_QK_DIM

    tok = jnp.concatenate([x.reshape(nb * n, d), ctx.reshape(nb * CTX_LEN, d)], axis=0)
    cvec = jnp.concatenate([c, c_ctx[None, :]], axis=0)
    mod = _mod_call(cvec, w_ada, b_ada).reshape(depth, nb + 1, 6, d)
    cos, sin = _rope_tables(n)
    band = jnp.asarray(_pool_band(), BF16)

    w1 = w_in[:, :, _column_order()].astype(BF16)
    wg = w_in[:, :, PROJ_WIDTH:].astype(BF16)
    wb = jnp.concatenate([w_branch[:, :2], w_branch[:, 2:, _wa_row_order()]], axis=1).astype(BF16)
    wo = w_out.astype(BF16)
    wfi = w_ffn_in.astype(BF16)
    wfo = w_ffn_out.astype(BF16)
    pw = pool_w.astype(BF16)

    n_tiles_all = tok.shape[0] // TM
    n_tiles_lat = nb * n // TM
    for l in range(depth):
        last = l == depth - 1
        lam_init = 0.8 - 0.6 * math.exp(-0.3 * l)
        g = norm_g[l].reshape(4, 1, d)
        qa, ka, va, u, qw, kw, vw = _proj_call(tok, mod[l], g[0], w1[l], cos, sin, nb, n)
        ya = _da_call(qa, ka, va, da_lambda[l], da_subln_g[l].reshape(1, DA_V_DIM), lam_init, nb, n, not last)
        yc = _wa_call(qw, kw, vw, wa_sink[l], nb, n, not last)
        n_tiles = n_tiles_lat if last else n_tiles_all
        tok = _merge_call(tok, mod[l], g[0], g[1], wg[l], ya, yc, u, band, pw[l],
                          pool_scale[l].reshape(1, BRANCH_WIDTH), wb[l], wo[l], nb, n, n_tiles)
        tok = _ffn_call(tok, mod[l], g[2], g[3], wfi[l], wfo[l], nb, n)
    return tok[:nb * n].reshape(nb, n, d)
```

```python
import functools
import math

import numpy as np
import jax
import jax.numpy as jnp
from jax import lax
from jax.experimental import pallas as pl
from jax.experimental.pallas import tpu as pltpu

F32 = jnp.float32
BF16 = jnp.bfloat16

D_MODEL = 1024
CTX_LEN = 256
GRID_W = 64
EPS = 1e-6
ROPE_THETA = 10000.0
BRANCH_WIDTH = D_MODEL // 2
N_BRANCH = 3
DA_HEADS = 4
DA_V_DIM = BRANCH_WIDTH // DA_HEADS
DA_QK_DIM = DA_V_DIM // 2
POOL_WINDOWS = (2, 4, 8, 16)
POOL_GROUPS = 4
POOL_GROUP = BRANCH_WIDTH // POOL_GROUPS
WA_HEAD_DIM = 64
WA_HEADS = BRANCH_WIDTH // WA_HEAD_DIM
WA_KV_HEADS = 2
WA_GROUP = WA_HEADS // WA_KV_HEADS
WA_WINDOW = 128
WA_BLOCK = 128
D_FF = ((8 * D_MODEL + 3 * 256 - 1) // (3 * 256)) * 256
PROJ_WIDTH = 5 * BRANCH_WIDTH + 2 * WA_KV_HEADS * WA_HEAD_DIM
GATE_WIDTH = N_BRANCH * D_MODEL

LANES = 128
BF16_SUBLANES = 16
VMEM_LIMIT = 56 * 1024 * 1024

TM = 512
TQ = 256
CK = 1024
POOL_SUB = 256
POOL_HALO = BF16_SUBLANES


def _params(n_axes):
    return pltpu.CompilerParams(dimension_semantics=("arbitrary",) * n_axes,
                                vmem_limit_bytes=VMEM_LIMIT)


def _const_spec(shape):
    nd = len(shape)
    return pl.BlockSpec(shape, lambda *_: (0,) * nd, pipeline_mode=pl.Buffered(1))


def _rms(x, g):
    return x * lax.rsqrt(jnp.mean(x * x, axis=-1, keepdims=True) + EPS) * g


def _modulate(x, g, shift, scale):
    return _rms(x, g) * (1.0 + scale) + shift


def _mod_kernel(c_ref, w_ref, b_ref, o_ref):
    c = c_ref[...]
    a = c * (1.0 / (1.0 + jnp.exp(-c)))
    o_ref[0] = jnp.dot(a, w_ref[0], preferred_element_type=F32,
                       precision=lax.Precision.HIGHEST) + b_ref[0]


def _mod_call(cvec, w_ada, b_ada):
    depth, d, width = w_ada.shape
    rows = cvec.shape[0]
    tn = 1536
    return pl.pallas_call(
        _mod_kernel,
        grid=(depth, width // tn),
        in_specs=[pl.BlockSpec((rows, d), lambda l, j: (0, 0)),
                  pl.BlockSpec((1, d, tn), lambda l, j: (l, 0, j)),
                  pl.BlockSpec((1, 1, tn), lambda l, j: (l, 0, j))],
        out_specs=pl.BlockSpec((1, rows, tn), lambda l, j: (l, 0, j)),
        out_shape=jax.ShapeDtypeStruct((depth, rows, width), F32),
        compiler_params=_params(2),
        name="adaln_mod",
    )(cvec, w_ada, b_ada.reshape(depth, 1, width))


def _proj_kernel(x_ref, mod_ref, g_ref, w_ref, c_ref, s_ref,
                 qa_ref, ka_ref, va_ref, u_ref, qw_ref, kw_ref, vw_ref):
    h = _modulate(x_ref[...], g_ref[...], mod_ref[0, 0:1, :], mod_ref[0, 1:2, :]).astype(BF16)
    cos = c_ref[...]
    sin = s_ref[...]

    def proj(lo, width):
        return jnp.dot(h, w_ref[:, lo:lo + width], preferred_element_type=F32)

    def rope(yj):
        return yj * cos + pltpu.roll(yj, LANES // 2, 1) * sin

    def key_store(ref, y):
        for j in range(y.shape[1] // LANES):
            ref[:, j * LANES:(j + 1) * LANES] = rope(y[:, j * LANES:(j + 1) * LANES]).astype(BF16)

    lane = lax.broadcasted_iota(jnp.int32, (TM, LANES), 1)
    first = (lane & (DA_QK_DIM // 2)) == 0

    def query_store(ref, y, scale):
        for j in range(y.shape[1] // LANES):
            r = rope(y[:, j * LANES:(j + 1) * LANES]) * scale
            ref[:, (2 * j) * LANES:(2 * j + 1) * LANES] = jnp.where(first, r, 0.0).astype(BF16)
            ref[:, (2 * j + 1) * LANES:(2 * j + 2) * LANES] = jnp.where(first, 0.0, r).astype(BF16)

    bw = BRANCH_WIDTH
    query_store(qa_ref, proj(0, bw), DA_QK_DIM ** -0.5 * math.log2(math.e))
    key_store(ka_ref, proj(bw, bw))
    va_ref[...] = proj(2 * bw, bw).astype(BF16)
    u_ref[...] = proj(3 * bw, bw).astype(BF16)
    query_store(qw_ref, proj(4 * bw, bw), WA_HEAD_DIM ** -0.5)
    kvw = proj(5 * bw, 2 * LANES)
    key_store(kw_ref, kvw[:, :LANES])
    vw_ref[...] = kvw[:, LANES:].astype(BF16)


def _tile_maps(nb, n, ctx):
    n_lat = nb * n // TM
    per_seq = n // TM

    def mod_map(t):
        return (jnp.where(t < n_lat, t // per_seq, nb), 0, 0)

    def rope_map(t):
        return (jnp.where(t < n_lat, t % per_seq, per_seq), 0)

    return n_lat, mod_map, rope_map


def _proj_call(tok, mod, g, w1, cos, sin, nb, n):
    nt, d = tok.shape
    _, mod_map, rope_map = _tile_maps(nb, n, CTX_LEN)
    row = lambda t: (t, 0)
    spec = lambda w: pl.BlockSpec((TM, w), row)
    sd = lambda w: jax.ShapeDtypeStruct((nt, w), BF16)
    widths = (2 * BRANCH_WIDTH, BRANCH_WIDTH, BRANCH_WIDTH, BRANCH_WIDTH, 2 * BRANCH_WIDTH, LANES, LANES)
    return pl.pallas_call(
        _proj_kernel,
        grid=(nt // TM,),
        in_specs=[pl.BlockSpec((TM, d), row),
                  pl.BlockSpec((1, 6, d), mod_map),
                  _const_spec((1, d)),
                  _const_spec(w1.shape),
                  pl.BlockSpec((TM, LANES), rope_map),
                  pl.BlockSpec((TM, LANES), rope_map)],
        out_specs=[spec(w) for w in widths],
        out_shape=[sd(w) for w in widths],
        compiler_params=_params(1),
        name="in_proj",
    )(tok, mod, g, w1, cos, sin)


def _da_kernel(dl_ref, g_ref, q_ref, kl_ref, kc_ref, vl_ref, vc_ref, o_ref, s_scr, *,
               lam_init, n_q, n_chunks):
    i = pl.program_id(2)
    lp = dl_ref[...]
    lam = (jnp.exp(jnp.sum(lp[0:1] * lp[1:2], axis=-1, keepdims=True))
           - jnp.exp(jnp.sum(lp[2:3] * lp[3:4], axis=-1, keepdims=True)) + lam_init)

    tq = q_ref.shape[0]
    width = 2 * tq
    n_lat = kl_ref.shape[0]
    qs = jnp.concatenate([q_ref[:, :LANES], q_ref[:, LANES:]], axis=0)

    def scores(k):
        return lax.dot_general(k, qs, (((1,), (1,)), ((), ())), preferred_element_type=F32)

    def fold(x, op):
        return op(x.reshape(x.shape[0] // 8, 8, width), axis=0)

    def pv(v, e):
        return lax.dot_general(v, e.astype(BF16), (((0,), (0,)), ((), ())), preferred_element_type=F32)

    def finish(l8, acc):
        l = jnp.sum(l8, axis=0, keepdims=True)
        ot = acc[:, :tq] / l[:, :tq] - lam * (acc[:, tq:] / l[:, tq:])
        o_ref[...] = (_rms(ot.T, g_ref[...]) * (1.0 - lam_init)).astype(BF16)

    @pl.when(i < n_q)
    def _():
        m8 = None
        for c in range(n_chunks):
            s = scores(kl_ref[c * CK:(c + 1) * CK, :])
            s_scr[c * CK:(c + 1) * CK, :] = s
            f = fold(s, jnp.max)
            m8 = f if m8 is None else jnp.maximum(m8, f)
        sc = scores(kc_ref[...])
        s_scr[n_lat:n_lat + CTX_LEN, :] = sc
        m = jnp.max(jnp.maximum(m8, fold(sc, jnp.max)), axis=0, keepdims=True)
        l8 = None
        acc = None
        for c in range(n_chunks):
            e = jnp.exp2(s_scr[c * CK:(c + 1) * CK, :] - m)
            f = fold(e, jnp.sum)
            p = pv(vl_ref[c * CK:(c + 1) * CK, :], e)
            l8 = f if l8 is None else l8 + f
            acc = p if acc is None else acc + p
        e = jnp.exp2(s_scr[n_lat:n_lat + CTX_LEN, :] - m)
        finish(l8 + fold(e, jnp.sum), acc + pv(vc_ref[...], e))

    @pl.when(i >= n_q)
    def _():
        sc = scores(kc_ref[...])
        e = jnp.exp2(sc - jnp.max(sc, axis=0, keepdims=True))
        finish(fold(e, jnp.sum), pv(vc_ref[...], e))


def _da_call(qa, ka, va, da_lambda, g_sub, lam_init, nb, n, with_ctx):
    nt = qa.shape[0]
    n_q = n // TQ
    ctx0 = nb * n // CTX_LEN
    steps = n_q + (1 if with_ctx else 0)

    def q_map(b, h, i):
        return (jnp.where(i < n_q, b * n_q + i, ctx0 + b), h)

    lat = pl.BlockSpec((n, DA_V_DIM), lambda b, h, i: (b, h))
    ctx = pl.BlockSpec((CTX_LEN, DA_V_DIM), lambda b, h, i: (ctx0 + b, h))
    kern = functools.partial(_da_kernel, lam_init=lam_init, n_q=n_q, n_chunks=n // CK)
    return pl.pallas_call(
        kern,
        grid=(nb, DA_HEADS, steps),
        in_specs=[pl.BlockSpec(da_lambda.shape, lambda b, h, i: (0, 0)),
                  pl.BlockSpec((1, DA_V_DIM), lambda b, h, i: (0, 0)),
                  pl.BlockSpec((TQ, 2 * DA_V_DIM), q_map),
                  lat, ctx, lat, ctx],
        out_specs=pl.BlockSpec((TQ, DA_V_DIM), q_map),
        out_shape=jax.ShapeDtypeStruct((nt, BRANCH_WIDTH), BF16),
        scratch_shapes=[pltpu.VMEM((n + CTX_LEN, 2 * TQ), F32)],
        compiler_params=_params(3),
        name="diff_attn",
    )(da_lambda, g_sub, qa, ka, ka, va, va)


def _wa_kernel(sink_ref, q_ref, kp_ref, kc_ref, kn_ref, kx_ref, vp_ref, vc_ref, vn_ref, vx_ref,
               o_ref, *, n_blk, n):
    i = pl.program_id(1)
    rows = q_ref.shape[0]
    lane = lax.broadcasted_iota(jnp.int32, (rows, LANES), 1)
    qs = jnp.concatenate([q_ref[:, k * LANES:(k + 1) * LANES] for k in range(WA_HEADS)], axis=0)

    def attend(k, v, mask):
        s = lax.dot_general(qs, k, (((1,), (1,)), ((), ())), preferred_element_type=F32)
        outs = []
        for blk in range(WA_HEADS):
            head = (blk // 2) + (WA_HEADS // 2) * (blk % 2)
            sk = sink_ref[head]
            sb = s[blk * rows:(blk + 1) * rows]
            if mask is not None:
                sb = jnp.where(mask, sb, -jnp.inf)
            m = jnp.maximum(jnp.max(sb, axis=-1, keepdims=True), sk)
            e = jnp.exp(sb - m)
            den = jnp.sum(e, axis=-1, keepdims=True) + jnp.exp(sk - m)
            outs.append(jnp.dot(e.astype(BF16), v, preferred_element_type=F32) / den)
        half = lane < WA_HEAD_DIM
        for j in range(WA_HEADS // 2):
            o_ref[:, j * LANES:(j + 1) * LANES] = jnp.where(half, outs[2 * j], outs[2 * j + 1]).astype(BF16)

    @pl.when(i < n_blk)
    def _():
        k = jnp.concatenate([kp_ref[...], kc_ref[...], kn_ref[...], kx_ref[...]], axis=0)
        v = jnp.concatenate([vp_ref[...], vc_ref[...], vn_ref[...], vx_ref[...]], axis=0)
        band = 3 * WA_BLOCK
        col = lax.broadcasted_iota(jnp.int32, (rows, band + CTX_LEN), 1)
        qpos = lax.broadcasted_iota(jnp.int32, (rows, band + CTX_LEN), 0)
        rel = col - WA_BLOCK - qpos
        kpos = (i - 1) * WA_BLOCK + col
        ok = (col >= band) | ((jnp.abs(rel) <= WA_WINDOW) & (kpos >= 0) & (kpos < n))
        attend(k, v, ok)

    @pl.when(i >= n_blk)
    def _():
        attend(kx_ref[...], vx_ref[...], None)


def _wa_call(qw, kw, vw, sink, nb, n, with_ctx):
    nt = qw.shape[0]
    n_blk = n // WA_BLOCK
    ctx_blk = CTX_LEN // WA_BLOCK
    ctx0_q = nb * n // WA_BLOCK
    ctx0_k = nb * n // CTX_LEN
    steps = n_blk + (ctx_blk if with_ctx else 0)

    def q_map(b, i):
        return (jnp.where(i < n_blk, b * n_blk + i, ctx0_q + b * ctx_blk + (i - n_blk)), 0)

    def near(delta):
        def m(b, i):
            j = jnp.clip(i + delta, 0, n_blk - 1)
            return (b * n_blk + j, 0)
        return pl.BlockSpec((WA_BLOCK, LANES), m)

    ctx = pl.BlockSpec((CTX_LEN, LANES), lambda b, i: (ctx0_k + b, 0))
    kern = functools.partial(_wa_kernel, n_blk=n_blk, n=n)
    return pl.pallas_call(
        kern,
        grid=(nb, steps),
        in_specs=[pl.BlockSpec(memory_space=pltpu.SMEM),
                  pl.BlockSpec((WA_BLOCK, 2 * BRANCH_WIDTH), q_map),
                  near(-1), near(0), near(1), ctx,
                  near(-1), near(0), near(1), ctx],
        out_specs=pl.BlockSpec((WA_BLOCK, BRANCH_WIDTH), q_map),
        out_shape=jax.ShapeDtypeStruct((nt, BRANCH_WIDTH), BF16),
        compiler_params=_params(2),
        name="window_attn",
    )(sink, qw, kw, kw, kw, kw, vw, vw, vw, vw)


def _pool_band():
    ext = POOL_SUB + 2 * POOL_HALO
    t = np.arange(POOL_SUB)[:, None]
    d = np.arange(ext)[None, :] - POOL_HALO - t
    return np.stack([((d >= -(w // 2)) & (d < w - w // 2)) for w in POOL_WINDOWS]).astype(np.float32)


def _merge_kernel(x_ref, mod_ref, g0_ref, g1_ref, wg_ref, ya_ref, yc_ref, u_ref, up_ref, un_ref,
                  band_ref, pw_ref, ps_ref, wb_ref, wo_ref, o_ref, *, n_lat, n):
    t = pl.program_id(0)
    x = x_ref[...]
    h = _modulate(x, g0_ref[...], mod_ref[0, 0:1, :], mod_ref[0, 1:2, :]).astype(BF16)

    u = u_ref[...]
    is_lat = t < n_lat
    halo_one = jnp.ones((POOL_HALO, POOL_GROUP), BF16)
    halo_zero = jnp.zeros((POOL_HALO, POOL_GROUP), BF16)
    yb_parts = []
    for sb in range(TM // POOL_SUB):
        r0 = sb * POOL_SUB
        prev = up_ref[...] if sb == 0 else u[r0 - POOL_HALO:r0]
        nxt = un_ref[...] if r0 + POOL_SUB == TM else u[r0 + POOL_SUB:r0 + POOL_SUB + POOL_HALO]
        cur = u[r0:r0 + POOL_SUB]
        g_row = t * TM + r0
        prev_ok = is_lat & ((g_row % n) != 0)
        next_ok = is_lat & (((g_row + POOL_SUB) % n) != 0)
        ext = jnp.concatenate([jnp.where(prev_ok, prev, jnp.zeros_like(prev)), cur,
                               jnp.where(next_ok, nxt, jnp.zeros_like(nxt))], axis=0)
        ones = jnp.concatenate([jnp.where(prev_ok, halo_one, halo_zero),
                                jnp.ones((POOL_SUB, POOL_GROUP), BF16),
                                jnp.where(next_ok, halo_one, halo_zero)], axis=0)
        cols = []
        for g in range(POOL_GROUPS):
            sl = slice(g * POOL_GROUP, (g + 1) * POOL_GROUP)
            band = band_ref[g]
            wsum = jnp.dot(band, ext[:, sl], preferred_element_type=F32)
            count = jnp.dot(band, ones, preferred_element_type=F32)
            pooled = (wsum / count - cur[:, sl].astype(F32)).astype(BF16)
            cols.append(jnp.dot(pooled, pw_ref[g], preferred_element_type=F32) * ps_ref[:, sl])
        yb_parts.append(jnp.concatenate(cols, axis=1))
    yb = jnp.concatenate(yb_parts, axis=0).astype(BF16)

    branches = (ya_ref[...], yb, yc_ref[...])
    m = None
    for k in range(N_BRANCH):
        gate = jnp.dot(h, wg_ref[:, k * D_MODEL:(k + 1) * D_MODEL], preferred_element_type=F32)
        gate = 1.0 / (1.0 + jnp.exp(-gate))
        term = gate * jnp.dot(branches[k], wb_ref[k], preferred_element_type=F32)
        m = term if m is None else m + term
    mix = jnp.dot(m.astype(BF16), wo_ref[...], preferred_element_type=F32)
    o_ref[...] = x + mod_ref[0, 2:3, :] * _rms(mix, g1_ref[...])


def _merge_call(tok, mod, g0, g1, wg, ya, yc, u, band, pool_w, pool_scale, wb, wo, nb, n, n_tiles):
    nt, d = tok.shape
    n_lat, mod_map, _ = _tile_maps(nb, n, CTX_LEN)
    row = lambda t: (t, 0)
    per_tile = TM // POOL_HALO
    last = nt // POOL_HALO - 1
    wide = pl.BlockSpec((TM, BRANCH_WIDTH), row)
    kern = functools.partial(_merge_kernel, n_lat=n_lat, n=n)
    return pl.pallas_call(
        kern,
        grid=(n_tiles,),
        in_specs=[pl.BlockSpec((TM, d), row),
                  pl.BlockSpec((1, 6, d), mod_map),
                  _const_spec((1, d)), _const_spec((1, d)),
                  _const_spec(wg.shape),
                  wide, wide, wide,
                  pl.BlockSpec((POOL_HALO, BRANCH_WIDTH), lambda t: (jnp.maximum(t * per_tile - 1, 0), 0)),
                  pl.BlockSpec((POOL_HALO, BRANCH_WIDTH), lambda t: (jnp.minimum((t + 1) * per_tile, last), 0)),
                  _const_spec(band.shape), _const_spec(pool_w.shape), _const_spec((1, BRANCH_WIDTH)),
                  _const_spec(wb.shape), _const_spec(wo.shape)],
        out_specs=pl.BlockSpec((TM, d), row),
        out_shape=jax.ShapeDtypeStruct((n_tiles * TM, d), F32),
        compiler_params=_params(1),
        name="pool_merge",
    )(tok, mod, g0, g1, wg, ya, yc, u, u, u, band, pool_w, pool_scale, wb, wo)


def _ffn_kernel(x_ref, mod_ref, g2_ref, g3_ref, wi_ref, wo_ref, o_ref):
    x = x_ref[...]
    h = _modulate(x, g2_ref[...], mod_ref[0, 3:4, :], mod_ref[0, 4:5, :]).astype(BF16)
    step = 256
    acc = None
    for c in range(D_FF // step):
        gate = jnp.dot(h, wi_ref[:, c * step:(c + 1) * step], preferred_element_type=F32)
        up = jnp.dot(h, wi_ref[:, D_FF + c * step:D_FF + (c + 1) * step], preferred_element_type=F32)
        act = (gate * (1.0 / (1.0 + jnp.exp(-gate))) * up).astype(BF16)
        part = jnp.dot(act, wo_ref[c * step:(c + 1) * step, :], preferred_element_type=F32)
        acc = part if acc is None else acc + part
    o_ref[...] = x + mod_ref[0, 5:6, :] * _rms(acc, g3_ref[...])


def _ffn_call(tok, mod, g2, g3, wi, wo, nb, n):
    nt, d = tok.shape
    _, mod_map, _ = _tile_maps(nb, n, CTX_LEN)
    row = lambda t: (t, 0)
    return pl.pallas_call(
        _ffn_kernel,
        grid=(nt // TM,),
        in_specs=[pl.BlockSpec((TM, d), row),
                  pl.BlockSpec((1, 6, d), mod_map),
                  _const_spec((1, d)), _const_spec((1, d)),
                  _const_spec(wi.shape), _const_spec(wo.shape)],
        out_specs=pl.BlockSpec((TM, d), row),
        out_shape=jax.ShapeDtypeStruct((nt, d), F32),
        compiler_params=_params(1),
        name="swiglu_ffn",
    )(tok, mod, g2, g3, wi, wo)


def _column_order():
    bw = BRANCH_WIDTH
    order = np.arange(PROJ_WIDTH)
    half = DA_QK_DIM // 2
    for base in (0, bw):
        for h in range(DA_HEADS):
            for p in range(2):
                for s in range(2):
                    for i in range(half):
                        order[base + h * DA_V_DIM + p * 64 + s * half + i] = (
                            base + h * DA_V_DIM + s * DA_QK_DIM + 2 * i + p)
    half = WA_HEAD_DIM // 2
    base = 4 * bw
    for j in range(WA_HEADS // 2):
        for p in range(2):
            for g in range(2):
                for i in range(half):
                    order[base + j * LANES + p * 64 + g * half + i] = (
                        base + (j + (WA_HEADS // 2) * g) * WA_HEAD_DIM + 2 * i + p)
    base = 5 * bw
    for p in range(2):
        for kv in range(WA_KV_HEADS):
            for i in range(half):
                order[base + p * 64 + kv * half + i] = base + kv * WA_HEAD_DIM + 2 * i + p
    return order


def _wa_row_order():
    order = np.arange(BRANCH_WIDTH)
    for j in range(WA_HEADS // 2):
        for g in range(2):
            for dd in range(WA_HEAD_DIM):
                order[j * LANES + g * WA_HEAD_DIM + dd] = (j + (WA_HEADS // 2) * g) * WA_HEAD_DIM + dd
    return order


def _rope_tables(n):
    rows = n // GRID_W
    row = jnp.repeat(jnp.arange(rows, dtype=F32), GRID_W)
    col = jnp.tile(jnp.arange(GRID_W, dtype=F32), rows)
    n_freq = DA_QK_DIM // 4
    inv = ROPE_THETA ** (-jnp.arange(n_freq, dtype=F32) / n_freq)
    ang = jnp.concatenate([row[:, None] * inv, col[:, None] * inv], axis=-1)
    cos = jnp.tile(jnp.cos(ang), (1, 4))
    sin = jnp.tile(jnp.sin(ang), (1, 4)) * jnp.where(jnp.arange(LANES) < 64, -1.0, 1.0)
    cos = jnp.concatenate([cos, jnp.ones((TM, LANES), F32)], axis=0)
    sin = jnp.concatenate([sin, jnp.zeros((TM, LANES), F32)], axis=0)
    return cos, sin


def kernel(x, c, ctx, c_ctx, w_ada, b_ada, norm_g, w_in, da_lambda, da_subln_g,
           pool_w, pool_scale, wa_sink, w_branch, w_out, w_ffn_in, w_ffn_out):
    nb, n, d = x.shape
    depth = w_ada.shape[0]
    assert d == D_MODEL and ctx.shape[1] == CTX_LEN
    assert n % TM == 0 and (nb * CTX_LEN) % TM == 0 and n % CK == 0 and n % GRID_W == 0
    assert WA_HEAD_DIM == DA_QK_DIM

    tok = jnp.concatenate([x.reshape(nb * n, d), ctx.reshape(nb * CTX_LEN, d)], axis=0)
    cvec = jnp.concatenate([c, c_ctx[None, :]], axis=0)
    mod = _mod_call(cvec, w_ada, b_ada).reshape(depth, nb + 1, 6, d)
    cos, sin = _rope_tables(n)
    band = jnp.asarray(_pool_band(), BF16)

    w1 = w_in[:, :, _column_order()].astype(BF16)
    wg = w_in[:, :, PROJ_WIDTH:].astype(BF16)
    wb = jnp.concatenate([w_branch[:, :2], w_branch[:, 2:, _wa_row_order()]], axis=1).astype(BF16)
    wo = w_out.astype(BF16)
    wfi = w_ffn_in.astype(BF16)
    wfo = w_ffn_out.astype(BF16)
    pw = pool_w.astype(BF16)

    n_tiles_all = tok.shape[0] // TM
    n_tiles_lat = nb * n // TM
    for l in range(depth):
        last = l == depth - 1
        lam_init = 0.8 - 0.6 * math.exp(-0.3 * l)
        g = norm_g[l].reshape(4, 1, d)
        qa, ka, va, u, qw, kw, vw = _proj_call(tok, mod[l], g[0], w1[l], cos, sin, nb, n)
        ya = _da_call(qa, ka, va, da_lambda[l], da_subln_g[l].reshape(1, DA_V_DIM), lam_init, nb, n, not last)
        yc = _wa_call(qw, kw, vw, wa_sink[l], nb, n, not last)
        n_tiles = n_tiles_lat if last else n_tiles_all
        tok = _merge_call(tok, mod[l], g[0], g[1], wg[l], ya, yc, u, band, pw[l],
                          pool_scale[l].reshape(1, BRANCH_WIDTH), wb[l], wo[l], nb, n, n_tiles)
        tok = _ffn_call(tok, mod[l], g[2], g[3], wfi[l], wfo[l], nb, n)
    return tok[:nb * n].reshape(nb, n, d)
```

```python
import functools
import math

import numpy as np
import jax
import jax.numpy as jnp
from jax import lax
from jax.experimental import pallas as pl
from jax.experimental.pallas import tpu as pltpu

F32 = jnp.float32
BF16 = jnp.bfloat16

D_MODEL = 1024
CTX_LEN = 256
GRID_W = 64
EPS = 1e-6
ROPE_THETA = 10000.0
BRANCH_WIDTH = D_MODEL // 2
N_BRANCH = 3
DA_HEADS = 4
DA_V_DIM = BRANCH_WIDTH // DA_HEADS
DA_QK_DIM = DA_V_DIM // 2
POOL_WINDOWS = (2, 4, 8, 16)
POOL_GROUPS = 4
POOL_GROUP = BRANCH_WIDTH // POOL_GROUPS
WA_HEAD_DIM = 64
WA_HEADS = BRANCH_WIDTH // WA_HEAD_DIM
WA_KV_HEADS = 2
WA_GROUP = WA_HEADS // WA_KV_HEADS
WA_WINDOW = 128
WA_BLOCK = 128
D_FF = ((8 * D_MODEL + 3 * 256 - 1) // (3 * 256)) * 256
PROJ_WIDTH = 5 * BRANCH_WIDTH + 2 * WA_KV_HEADS * WA_HEAD_DIM
GATE_WIDTH = N_BRANCH * D_MODEL

LANES = 128
BF16_SUBLANES = 16
VMEM_LIMIT = 56 * 1024 * 1024

TM = 512
TQ = 256
CK = 256
POOL_SUB = 256
POOL_HALO = BF16_SUBLANES


def _params(n_axes):
    return pltpu.CompilerParams(dimension_semantics=("arbitrary",) * n_axes,
                                vmem_limit_bytes=VMEM_LIMIT)


def _const_spec(shape):
    nd = len(shape)
    return pl.BlockSpec(shape, lambda *_: (0,) * nd, pipeline_mode=pl.Buffered(1))


def _rms(x, g):
    return x * lax.rsqrt(jnp.mean(x * x, axis=-1, keepdims=True) + EPS) * g


def _modulate(x, g, shift, scale):
    return _rms(x, g) * (1.0 + scale) + shift


def _mod_kernel(c_ref, w_ref, b_ref, o_ref):
    c = c_ref[...]
    a = c * (1.0 / (1.0 + jnp.exp(-c)))
    o_ref[0] = jnp.dot(a, w_ref[0], preferred_element_type=F32,
                       precision=lax.Precision.HIGHEST) + b_ref[0]


def _mod_call(cvec, w_ada, b_ada):
    depth, d, width = w_ada.shape
    rows = cvec.shape[0]
    tn = 1536
    return pl.pallas_call(
        _mod_kernel,
        grid=(depth, width // tn),
        in_specs=[pl.BlockSpec((rows, d), lambda l, j: (0, 0)),
                  pl.BlockSpec((1, d, tn), lambda l, j: (l, 0, j)),
                  pl.BlockSpec((1, 1, tn), lambda l, j: (l, 0, j))],
        out_specs=pl.BlockSpec((1, rows, tn), lambda l, j: (l, 0, j)),
        out_shape=jax.ShapeDtypeStruct((depth, rows, width), F32),
        compiler_params=_params(2),
        name="adaln_mod",
    )(cvec, w_ada, b_ada.reshape(depth, 1, width))


def _proj_kernel(x_ref, mod_ref, g_ref, w_ref, c_ref, s_ref,
                 qa_ref, ka_ref, va_ref, u_ref, qw_ref, kw_ref, vw_ref):
    h = _modulate(x_ref[...], g_ref[...], mod_ref[0, 0:1, :], mod_ref[0, 1:2, :]).astype(BF16)
    cos = c_ref[...]
    sin = s_ref[...]

    def proj(lo, width):
        return jnp.dot(h, w_ref[:, lo:lo + width], preferred_element_type=F32)

    def rope(yj):
        return yj * cos + pltpu.roll(yj, LANES // 2, 1) * sin

    def key_store(ref, y):
        for j in range(y.shape[1] // LANES):
            ref[:, j * LANES:(j + 1) * LANES] = rope(y[:, j * LANES:(j + 1) * LANES]).astype(BF16)

    lane = lax.broadcasted_iota(jnp.int32, (TM, LANES), 1)
    first = (lane & (DA_QK_DIM // 2)) == 0

    def query_store(ref, y, scale):
        for j in range(y.shape[1] // LANES):
            r = rope(y[:, j * LANES:(j + 1) * LANES]) * scale
            ref[:, (2 * j) * LANES:(2 * j + 1) * LANES] = jnp.where(first, r, 0.0).astype(BF16)
            ref[:, (2 * j + 1) * LANES:(2 * j + 2) * LANES] = jnp.where(first, 0.0, r).astype(BF16)

    bw = BRANCH_WIDTH
    query_store(qa_ref, proj(0, bw), DA_QK_DIM ** -0.5 * math.log2(math.e))
    key_store(ka_ref, proj(bw, bw))
    va_ref[...] = proj(2 * bw, bw).astype(BF16)
    u_ref[...] = proj(3 * bw, bw).astype(BF16)
    query_store(qw_ref, proj(4 * bw, bw), WA_HEAD_DIM ** -0.5)
    kvw = proj(5 * bw, 2 * LANES)
    key_store(kw_ref, kvw[:, :LANES])
    vw_ref[...] = kvw[:, LANES:].astype(BF16)


def _tile_maps(nb, n, ctx):
    n_lat = nb * n // TM
    per_seq = n // TM

    def mod_map(t):
        return (jnp.where(t < n_lat, t // per_seq, nb), 0, 0)

    def rope_map(t):
        return (jnp.where(t < n_lat, t % per_seq, per_seq), 0)

    return n_lat, mod_map, rope_map


def _proj_call(tok, mod, g, w1, cos, sin, nb, n):
    nt, d = tok.shape
    _, mod_map, rope_map = _tile_maps(nb, n, CTX_LEN)
    row = lambda t: (t, 0)
    spec = lambda w: pl.BlockSpec((TM, w), row)
    sd = lambda w: jax.ShapeDtypeStruct((nt, w), BF16)
    widths = (2 * BRANCH_WIDTH, BRANCH_WIDTH, BRANCH_WIDTH, BRANCH_WIDTH, 2 * BRANCH_WIDTH, LANES, LANES)
    return pl.pallas_call(
        _proj_kernel,
        grid=(nt // TM,),
        in_specs=[pl.BlockSpec((TM, d), row),
                  pl.BlockSpec((1, 6, d), mod_map),
                  _const_spec((1, d)),
                  _const_spec(w1.shape),
                  pl.BlockSpec((TM, LANES), rope_map),
                  pl.BlockSpec((TM, LANES), rope_map)],
        out_specs=[spec(w) for w in widths],
        out_shape=[sd(w) for w in widths],
        compiler_params=_params(1),
        name="in_proj",
    )(tok, mod, g, w1, cos, sin)


def _da_lambda(dl_ref, lam_init):
    lp = dl_ref[...]
    return (jnp.exp(jnp.sum(lp[0:1] * lp[1:2], axis=-1, keepdims=True))
            - jnp.exp(jnp.sum(lp[2:3] * lp[3:4], axis=-1, keepdims=True)) + lam_init)


def _da_scores(k, qs):
    return lax.dot_general(k, qs, (((1,), (1,)), ((), ())), preferred_element_type=F32)


def _da_fold(x, op):
    return op(x.reshape(x.shape[0] // 8, 8, x.shape[1]), axis=0)


def _da_pv(v, e):
    return lax.dot_general(v, e.astype(BF16), (((0,), (0,)), ((), ())), preferred_element_type=F32)


def _da_finish(o_ref, g_ref, lam, lam_init, l8, acc):
    tq = o_ref.shape[0]
    l = jnp.sum(l8, axis=0, keepdims=True)
    ot = acc[:, :tq] / l[:, :tq] - lam * (acc[:, tq:] / l[:, tq:])
    o_ref[...] = (_rms(ot.T, g_ref[...]) * (1.0 - lam_init)).astype(BF16)


def _da_kernel(dl_ref, g_ref, q_ref, kl_ref, kc_ref, vl_ref, vc_ref, o_ref, s_scr, m_scr, *, lam_init):
    t = pl.program_id(0)
    lam = _da_lambda(dl_ref, lam_init)
    n_lat = kl_ref.shape[0]
    qs = jnp.concatenate([q_ref[:, :LANES], q_ref[:, LANES:]], axis=0)

    @pl.when(t == 0)
    def _():
        s_scr[1] = jnp.zeros(s_scr.shape[1:], F32)
        m_scr[1] = jnp.zeros(m_scr.shape[1:], F32)

    def step(wr, rd):
        m_prev = m_scr[rd]
        m8 = l8 = acc = None
        for c in range(n_lat // CK):
            rows = slice(c * CK, (c + 1) * CK)
            s = _da_scores(kl_ref[rows, :], qs)
            s_scr[wr, rows, :] = s
            f = _da_fold(s, jnp.max)
            m8 = f if m8 is None else jnp.maximum(m8, f)
            e = jnp.exp2(s_scr[rd, rows, :] - m_prev)
            f = _da_fold(e, jnp.sum)
            p = _da_pv(vl_ref[rows, :], e)
            l8 = f if l8 is None else l8 + f
            acc = p if acc is None else acc + p
        rows = slice(n_lat, n_lat + CTX_LEN)
        sc = _da_scores(kc_ref[...], qs)
        s_scr[wr, rows, :] = sc
        m_scr[wr] = jnp.max(jnp.maximum(m8, _da_fold(sc, jnp.max)), axis=0, keepdims=True)
        e = jnp.exp2(s_scr[rd, rows, :] - m_prev)
        _da_finish(o_ref, g_ref, lam, lam_init, l8 + _da_fold(e, jnp.sum), acc + _da_pv(vc_ref[...], e))

    @pl.when(t % 2 == 0)
    def _():
        step(0, 1)

    @pl.when(t % 2 == 1)
    def _():
        step(1, 0)


def _da_call(qa, ka, va, da_lambda, g_sub, lam_init, nb, n):
    nt = qa.shape[0]
    n_q = n // TQ
    units = nb * DA_HEADS * n_q
    ctx0 = nb * n // CTX_LEN

    def unit(u):
        return u // (n_q * DA_HEADS), (u // n_q) % DA_HEADS, u % n_q

    def scored(t):
        return unit(jnp.minimum(t, units - 1))

    def finished(t):
        return unit(jnp.maximum(t - 1, 0))

    def tile_map(which):
        def m(t):
            b, h, i = which(t)
            return (b * n_q + i, h)
        return m

    def lat(which):
        return pl.BlockSpec((n, DA_V_DIM), lambda t: which(t)[:2])

    def ctx(which):
        def m(t):
            b, h, _ = which(t)
            return (ctx0 + b, h)
        return pl.BlockSpec((CTX_LEN, DA_V_DIM), m)

    kern = functools.partial(_da_kernel, lam_init=lam_init)
    return pl.pallas_call(
        kern,
        grid=(units + 1,),
        in_specs=[pl.BlockSpec(da_lambda.shape, lambda t: (0, 0)),
                  pl.BlockSpec((1, DA_V_DIM), lambda t: (0, 0)),
                  pl.BlockSpec((TQ, 2 * DA_V_DIM), tile_map(scored)),
                  lat(scored), ctx(scored), lat(finished), ctx(finished)],
        out_specs=pl.BlockSpec((TQ, DA_V_DIM), tile_map(finished)),
        out_shape=jax.ShapeDtypeStruct((nt, BRANCH_WIDTH), BF16),
        scratch_shapes=[pltpu.VMEM((2, n + CTX_LEN, 2 * TQ), F32), pltpu.VMEM((2, 1, 2 * TQ), F32)],
        compiler_params=_params(1),
        name="diff_attn",
    )(da_lambda, g_sub, qa, ka, ka, va, va)


def _da_ctx_kernel(dl_ref, g_ref, q_ref, k_ref, v_ref, ya_ref, o_ref, *, lam_init):
    del ya_ref
    qs = jnp.concatenate([q_ref[:, :LANES], q_ref[:, LANES:]], axis=0)
    s = _da_scores(k_ref[...], qs)
    e = jnp.exp2(s - jnp.max(s, axis=0, keepdims=True))
    _da_finish(o_ref, g_ref, _da_lambda(dl_ref, lam_init), lam_init, _da_fold(e, jnp.sum), _da_pv(v_ref[...], e))


def _da_ctx_call(qa, ka, va, ya, da_lambda, g_sub, lam_init, nb, n):
    ctx0 = nb * n // CTX_LEN
    blk = lambda w: pl.BlockSpec((CTX_LEN, w), lambda b, h: (ctx0 + b, h))
    kern = functools.partial(_da_ctx_kernel, lam_init=lam_init)
    return pl.pallas_call(
        kern,
        grid=(nb, DA_HEADS),
        in_specs=[pl.BlockSpec(da_lambda.shape, lambda b, h: (0, 0)),
                  pl.BlockSpec((1, DA_V_DIM), lambda b, h: (0, 0)),
                  blk(2 * DA_V_DIM), blk(DA_V_DIM), blk(DA_V_DIM),
                  pl.BlockSpec(memory_space=pl.ANY)],
        out_specs=blk(DA_V_DIM),
        out_shape=jax.ShapeDtypeStruct(ya.shape, ya.dtype),
        input_output_aliases={5: 0},
        compiler_params=_params(2),
        name="diff_attn_ctx",
    )(da_lambda, g_sub, qa, ka, va, ya)


def _wa_kernel(sink_ref, q_ref, kp_ref, kc_ref, kn_ref, kx_ref, vp_ref, vc_ref, vn_ref, vx_ref,
               o_ref, *, n_blk, n):
    i = pl.program_id(1)
    rows = q_ref.shape[0]
    lane = lax.broadcasted_iota(jnp.int32, (rows, LANES), 1)
    qs = jnp.concatenate([q_ref[:, k * LANES:(k + 1) * LANES] for k in range(WA_HEADS)], axis=0)

    def attend(k, v, mask):
        s = lax.dot_general(qs, k, (((1,), (1,)), ((), ())), preferred_element_type=F32)
        outs = []
        for blk in range(WA_HEADS):
            head = (blk // 2) + (WA_HEADS // 2) * (blk % 2)
            sk = sink_ref[head]
            sb = s[blk * rows:(blk + 1) * rows]
            if mask is not None:
                sb = jnp.where(mask, sb, -jnp.inf)
            m = jnp.maximum(jnp.max(sb, axis=-1, keepdims=True), sk)
            e = jnp.exp(sb - m)
            den = jnp.sum(e, axis=-1, keepdims=True) + jnp.exp(sk - m)
            outs.append(jnp.dot(e.astype(BF16), v, preferred_element_type=F32) / den)
        half = lane < WA_HEAD_DIM
        for j in range(WA_HEADS // 2):
            o_ref[:, j * LANES:(j + 1) * LANES] = jnp.where(half, outs[2 * j], outs[2 * j + 1]).astype(BF16)

    @pl.when(i < n_blk)
    def _():
        k = jnp.concatenate([kp_ref[...], kc_ref[...], kn_ref[...], kx_ref[...]], axis=0)
        v = jnp.concatenate([vp_ref[...], vc_ref[...], vn_ref[...], vx_ref[...]], axis=0)
        band = 3 * WA_BLOCK
        col = lax.broadcasted_iota(jnp.int32, (rows, band + CTX_LEN), 1)
        qpos = lax.broadcasted_iota(jnp.int32, (rows, band + CTX_LEN), 0)
        rel = col - WA_BLOCK - qpos
        kpos = (i - 1) * WA_BLOCK + col
        ok = (col >= band) | ((jnp.abs(rel) <= WA_WINDOW) & (kpos >= 0) & (kpos < n))
        attend(k, v, ok)

    @pl.when(i >= n_blk)
    def _():
        attend(kx_ref[...], vx_ref[...], None)


def _wa_call(qw, kw, vw, sink, nb, n, with_ctx):
    nt = qw.shape[0]
    n_blk = n // WA_BLOCK
    ctx_blk = CTX_LEN // WA_BLOCK
    ctx0_q = nb * n // WA_BLOCK
    ctx0_k = nb * n // CTX_LEN
    steps = n_blk + (ctx_blk if with_ctx else 0)

    def q_map(b, i):
        return (jnp.where(i < n_blk, b * n_blk + i, ctx0_q + b * ctx_blk + (i - n_blk)), 0)

    def near(delta):
        def m(b, i):
            j = jnp.clip(i + delta, 0, n_blk - 1)
            return (b * n_blk + j, 0)
        return pl.BlockSpec((WA_BLOCK, LANES), m)

    ctx = pl.BlockSpec((CTX_LEN, LANES), lambda b, i: (ctx0_k + b, 0))
    kern = functools.partial(_wa_kernel, n_blk=n_blk, n=n)
    return pl.pallas_call(
        kern,
        grid=(nb, steps),
        in_specs=[pl.BlockSpec(memory_space=pltpu.SMEM),
                  pl.BlockSpec((WA_BLOCK, 2 * BRANCH_WIDTH), q_map),
                  near(-1), near(0), near(1), ctx,
                  near(-1), near(0), near(1), ctx],
        out_specs=pl.BlockSpec((WA_BLOCK, BRANCH_WIDTH), q_map),
        out_shape=jax.ShapeDtypeStruct((nt, BRANCH_WIDTH), BF16),
        compiler_params=_params(2),
        name="window_attn",
    )(sink, qw, kw, kw, kw, kw, vw, vw, vw, vw)


def _pool_band():
    ext = POOL_SUB + 2 * POOL_HALO
    t = np.arange(POOL_SUB)[:, None]
    d = np.arange(ext)[None, :] - POOL_HALO - t
    return np.stack([((d >= -(w // 2)) & (d < w - w // 2)) for w in POOL_WINDOWS]).astype(np.float32)


def _merge_kernel(x_ref, mod_ref, g0_ref, g1_ref, wg_ref, ya_ref, yc_ref, u_ref, up_ref, un_ref,
                  band_ref, pw_ref, ps_ref, wb_ref, wo_ref, o_ref, *, n_lat, n):
    t = pl.program_id(0)
    x = x_ref[...]
    h = _modulate(x, g0_ref[...], mod_ref[0, 0:1, :], mod_ref[0, 1:2, :]).astype(BF16)

    u = u_ref[...]
    is_lat = t < n_lat
    halo_one = jnp.ones((POOL_HALO, POOL_GROUP), BF16)
    halo_zero = jnp.zeros((POOL_HALO, POOL_GROUP), BF16)
    yb_parts = []
    for sb in range(TM // POOL_SUB):
        r0 = sb * POOL_SUB
        prev = up_ref[...] if sb == 0 else u[r0 - POOL_HALO:r0]
        nxt = un_ref[...] if r0 + POOL_SUB == TM else u[r0 + POOL_SUB:r0 + POOL_SUB + POOL_HALO]
        cur = u[r0:r0 + POOL_SUB]
        g_row = t * TM + r0
        prev_ok = is_lat & ((g_row % n) != 0)
        next_ok = is_lat & (((g_row + POOL_SUB) % n) != 0)
        ext = jnp.concatenate([jnp.where(prev_ok, prev, jnp.zeros_like(prev)), cur,
                               jnp.where(next_ok, nxt, jnp.zeros_like(nxt))], axis=0)
        ones = jnp.concatenate([jnp.where(prev_ok, halo_one, halo_zero),
                                jnp.ones((POOL_SUB, POOL_GROUP), BF16),
                                jnp.where(next_ok, halo_one, halo_zero)], axis=0)
        cols = []
        for g in range(POOL_GROUPS):
            sl = slice(g * POOL_GROUP, (g + 1) * POOL_GROUP)
            band = band_ref[g]
            wsum = jnp.dot(band, ext[:, sl], preferred_element_type=F32)
            count = jnp.dot(band, ones, preferred_element_type=F32)
            pooled = (wsum / count - cur[:, sl].astype(F32)).astype(BF16)
            cols.append(jnp.dot(pooled, pw_ref[g], preferred_element_type=F32) * ps_ref[:, sl])
        yb_parts.append(jnp.concatenate(cols, axis=1))
    yb = jnp.concatenate(yb_parts, axis=0).astype(BF16)

    branches = (ya_ref[...], yb, yc_ref[...])
    m = None
    for k in range(N_BRANCH):
        gate = jnp.dot(h, wg_ref[:, k * D_MODEL:(k + 1) * D_MODEL], preferred_element_type=F32)
        gate = 1.0 / (1.0 + jnp.exp(-gate))
        term = gate * jnp.dot(branches[k], wb_ref[k], preferred_element_type=F32)
        m = term if m is None else m + term
    mix = jnp.dot(m.astype(BF16), wo_ref[...], preferred_element_type=F32)
    o_ref[...] = x + mod_ref[0, 2:3, :] * _rms(mix, g1_ref[...])


def _merge_call(tok, mod, g0, g1, wg, ya, yc, u, band, pool_w, pool_scale, wb, wo, nb, n, n_tiles):
    nt, d = tok.shape
    n_lat, mod_map, _ = _tile_maps(nb, n, CTX_LEN)
    row = lambda t: (t, 0)
    per_tile = TM // POOL_HALO
    last = nt // POOL_HALO - 1
    wide = pl.BlockSpec((TM, BRANCH_WIDTH), row)
    kern = functools.partial(_merge_kernel, n_lat=n_lat, n=n)
    return pl.pallas_call(
        kern,
        grid=(n_tiles,),
        in_specs=[pl.BlockSpec((TM, d), row),
                  pl.BlockSpec((1, 6, d), mod_map),
                  _const_spec((1, d)), _const_spec((1, d)),
                  _const_spec(wg.shape),
                  wide, wide, wide,
                  pl.BlockSpec((POOL_HALO, BRANCH_WIDTH), lambda t: (jnp.maximum(t * per_tile - 1, 0), 0)),
                  pl.BlockSpec((POOL_HALO, BRANCH_WIDTH), lambda t: (jnp.minimum((t + 1) * per_tile, last), 0)),
                  _const_spec(band.shape), _const_spec(pool_w.shape), _const_spec((1, BRANCH_WIDTH)),
                  _const_spec(wb.shape), _const_spec(wo.shape)],
        out_specs=pl.BlockSpec((TM, d), row),
        out_shape=jax.ShapeDtypeStruct((n_tiles * TM, d), F32),
        compiler_params=_params(1),
        name="pool_merge",
    )(tok, mod, g0, g1, wg, ya, yc, u, u, u, band, pool_w, pool_scale, wb, wo)


def _ffn_kernel(x_ref, mod_ref, g2_ref, g3_ref, wi_ref, wo_ref, o_ref):
    x = x_ref[...]
    h = _modulate(x, g2_ref[...], mod_ref[0, 3:4, :], mod_ref[0, 4:5, :]).astype(BF16)
    step = 256
    acc = None
    for c in range(D_FF // step):
        gate = jnp.dot(h, wi_ref[:, c * step:(c + 1) * step], preferred_element_type=F32)
        up = jnp.dot(h, wi_ref[:, D_FF + c * step:D_FF + (c + 1) * step], preferred_element_type=F32)
        act = (gate * (1.0 / (1.0 + jnp.exp(-gate))) * up).astype(BF16)
        part = jnp.dot(act, wo_ref[c * step:(c + 1) * step, :], preferred_element_type=F32)
        acc = part if acc is None else acc + part
    o_ref[...] = x + mod_ref[0, 5:6, :] * _rms(acc, g3_ref[...])


def _ffn_call(tok, mod, g2, g3, wi, wo, nb, n):
    nt, d = tok.shape
    _, mod_map, _ = _tile_maps(nb, n, CTX_LEN)
    row = lambda t: (t, 0)
    return pl.pallas_call(
        _ffn_kernel,
        grid=(nt // TM,),
        in_specs=[pl.BlockSpec((TM, d), row),
                  pl.BlockSpec((1, 6, d), mod_map),
                  _const_spec((1, d)), _const_spec((1, d)),
                  _const_spec(wi.shape), _const_spec(wo.shape)],
        out_specs=pl.BlockSpec((TM, d), row),
        out_shape=jax.ShapeDtypeStruct((nt, d), F32),
        compiler_params=_params(1),
        name="swiglu_ffn",
    )(tok, mod, g2, g3, wi, wo)


def _column_order():
    bw = BRANCH_WIDTH
    order = np.arange(PROJ_WIDTH)
    half = DA_QK_DIM // 2
    for base in (0, bw):
        for h in range(DA_HEADS):
            for p in range(2):
                for s in range(2):
                    for i in range(half):
                        order[base + h * DA_V_DIM + p * 64 + s * half + i] = (
                            base + h * DA_V_DIM + s * DA_QK_DIM + 2 * i + p)
    half = WA_HEAD_DIM // 2
    base = 4 * bw
    for j in range(WA_HEADS // 2):
        for p in range(2):
            for g in range(2):
                for i in range(half):
                    order[base + j * LANES + p * 64 + g * half + i] = (
                        base + (j + (WA_HEADS // 2) * g) * WA_HEAD_DIM + 2 * i + p)
    base = 5 * bw
    for p in range(2):
        for kv in range(WA_KV_HEADS):
            for i in range(half):
                order[base + p * 64 + kv * half + i] = base + kv * WA_HEAD_DIM + 2 * i + p
    return order


def _wa_row_order():
    order = np.arange(BRANCH_WIDTH)
    for j in range(WA_HEADS // 2):
        for g in range(2):
            for dd in range(WA_HEAD_DIM):
                order[j * LANES + g * WA_HEAD_DIM + dd] = (j + (WA_HEADS // 2) * g) * WA_HEAD_DIM + dd
    return order


def _rope_tables(n):
    rows = n // GRID_W
    row = jnp.repeat(jnp.arange(rows, dtype=F32), GRID_W)
    col = jnp.tile(jnp.arange(GRID_W, dtype=F32), rows)
    n_freq = DA_QK_DIM // 4
    inv = ROPE_THETA ** (-jnp.arange(n_freq, dtype=F32) / n_freq)
    ang = jnp.concatenate([row[:, None] * inv, col[:, None] * inv], axis=-1)
    cos = jnp.tile(jnp.cos(ang), (1, 4))
    sin = jnp.tile(jnp.sin(ang), (1, 4)) * jnp.where(jnp.arange(LANES) < 64, -1.0, 1.0)
    cos = jnp.concatenate([cos, jnp.ones((TM, LANES), F32)], axis=0)
    sin = jnp.concatenate([sin, jnp.zeros((TM, LANES), F32)], axis=0)
    return cos, sin


def kernel(x, c, ctx, c_ctx, w_ada, b_ada, norm_g, w_in, da_lambda, da_subln_g,
           pool_w, pool_scale, wa_sink, w_branch, w_out, w_ffn_in, w_ffn_out):
    nb, n, d = x.shape
    depth = w_ada.shape[0]
    assert d == D_MODEL and ctx.shape[1] == CTX_LEN
    assert n % TM == 0 and (nb * CTX_LEN) % TM == 0 and n % CK == 0 and n % GRID_W == 0
    assert TQ == CTX_LEN and n % TQ == 0
    assert WA_HEAD_DIM == DA_QK_DIM

    tok = jnp.concatenate([x.reshape(nb * n, d), ctx.reshape(nb * CTX_LEN, d)], axis=0)
    cvec = jnp.concatenate([c, c_ctx[None, :]], axis=0)
    mod = _mod_call(cvec, w_ada, b_ada).reshape(depth, nb + 1, 6, d)
    cos, sin = _rope_tables(n)
    band = jnp.asarray(_pool_band(), BF16)

    w1 = w_in[:, :, _column_order()].astype(BF16)
    wg = w_in[:, :, PROJ_WIDTH:].astype(BF16)
    wb = jnp.concatenate([w_branch[:, :2], w_branch[:, 2:, _wa_row_order()]], axis=1).astype(BF16)
    wo = w_out.astype(BF16)
    wfi = w_ffn_in.astype(BF16)
    wfo = w_ffn_out.astype(BF16)
    pw = pool_w.astype(BF16)

    n_tiles_all = tok.shape[0] // TM
    n_tiles_lat = nb * n // TM
    for l in range(depth):
        last = l == depth - 1
        lam_init = 0.8 - 0.6 * math.exp(-0.3 * l)
        g = norm_g[l].reshape(4, 1, d)
        qa, ka, va, u, qw, kw, vw = _proj_call(tok, mod[l], g[0], w1[l], cos, sin, nb, n)
        g_sub = da_subln_g[l].reshape(1, DA_V_DIM)
        ya = _da_call(qa, ka, va, da_lambda[l], g_sub, lam_init, nb, n)
        if not last:
            ya = _da_ctx_call(qa, ka, va, ya, da_lambda[l], g_sub, lam_init, nb, n)
        yc = _wa_call(qw, kw, vw, wa_sink[l], nb, n, not last)
        n_tiles = n_tiles_lat if last else n_tiles_all
        tok = _merge_call(tok, mod[l], g[0], g[1], wg[l], ya, yc, u, band, pw[l],
                          pool_scale[l].reshape(1, BRANCH_WIDTH), wb[l], wo[l], nb, n, n_tiles)
        tok = _ffn_call(tok, mod[l], g[2], g[3], wfi[l], wfo[l], nb, n)
    return tok[:nb * n].reshape(nb, n, d)
```

```python
import functools
import math

import numpy as np
import jax
import jax.numpy as jnp
from jax import lax
from jax.experimental import pallas as pl
from jax.experimental.pallas import tpu as pltpu

F32 = jnp.float32
BF16 = jnp.bfloat16

D_MODEL = 1024
CTX_LEN = 256
GRID_W = 64
EPS = 1e-6
ROPE_THETA = 10000.0
BRANCH_WIDTH = D_MODEL // 2
N_BRANCH = 3
DA_HEADS = 4
DA_V_DIM = BRANCH_WIDTH // DA_HEADS
DA_QK_DIM = DA_V_DIM // 2
POOL_WINDOWS = (2, 4, 8, 16)
POOL_GROUPS = 4
POOL_GROUP = BRANCH_WIDTH // POOL_GROUPS
WA_HEAD_DIM = 64
WA_HEADS = BRANCH_WIDTH // WA_HEAD_DIM
WA_KV_HEADS = 2
WA_GROUP = WA_HEADS // WA_KV_HEADS
WA_WINDOW = 128
WA_BLOCK = 128
D_FF = ((8 * D_MODEL + 3 * 256 - 1) // (3 * 256)) * 256
PROJ_WIDTH = 5 * BRANCH_WIDTH + 2 * WA_KV_HEADS * WA_HEAD_DIM
GATE_WIDTH = N_BRANCH * D_MODEL

LANES = 128
BF16_SUBLANES = 16
VMEM_LIMIT = 56 * 1024 * 1024

TM = 512
TQ = 256
CK = 256
POOL_SUB = 128
POOL_HALO = BF16_SUBLANES


def _params(n_axes):
    return pltpu.CompilerParams(dimension_semantics=("arbitrary",) * n_axes,
                                vmem_limit_bytes=VMEM_LIMIT)


def _const_spec(shape):
    nd = len(shape)
    return pl.BlockSpec(shape, lambda *_: (0,) * nd, pipeline_mode=pl.Buffered(1))


def _rms(x, g):
    return x * lax.rsqrt(jnp.mean(x * x, axis=-1, keepdims=True) + EPS) * g


def _modulate(x, g, shift, scale):
    return _rms(x, g) * (1.0 + scale) + shift


def _mod_kernel(c_ref, w_ref, b_ref, o_ref):
    c = c_ref[...]
    a = c * (1.0 / (1.0 + jnp.exp(-c)))
    o_ref[0] = jnp.dot(a, w_ref[0], preferred_element_type=F32,
                       precision=lax.Precision.HIGHEST) + b_ref[0]


def _mod_call(cvec, w_ada, b_ada):
    depth, d, width = w_ada.shape
    rows = cvec.shape[0]
    tn = 1536
    return pl.pallas_call(
        _mod_kernel,
        grid=(depth, width // tn),
        in_specs=[pl.BlockSpec((rows, d), lambda l, j: (0, 0)),
                  pl.BlockSpec((1, d, tn), lambda l, j: (l, 0, j)),
                  pl.BlockSpec((1, 1, tn), lambda l, j: (l, 0, j))],
        out_specs=pl.BlockSpec((1, rows, tn), lambda l, j: (l, 0, j)),
        out_shape=jax.ShapeDtypeStruct((depth, rows, width), F32),
        compiler_params=_params(2),
        name="adaln_mod",
    )(cvec, w_ada, b_ada.reshape(depth, 1, width))


def _proj_kernel(x_ref, mod_ref, g_ref, w_ref, c_ref, s_ref,
                 qa_ref, ka_ref, va_ref, u_ref, qw_ref, kw_ref, vw_ref):
    h = _modulate(x_ref[...], g_ref[...], mod_ref[0, 0:1, :], mod_ref[0, 1:2, :]).astype(BF16)
    cos = c_ref[...]
    sin = s_ref[...]

    def proj(lo, width):
        return jnp.dot(h, w_ref[:, lo:lo + width], preferred_element_type=F32)

    def rope(yj):
        return yj * cos + pltpu.roll(yj, LANES // 2, 1) * sin

    def key_store(ref, y):
        for j in range(y.shape[1] // LANES):
            ref[:, j * LANES:(j + 1) * LANES] = rope(y[:, j * LANES:(j + 1) * LANES]).astype(BF16)

    lane = lax.broadcasted_iota(jnp.int32, (TM, LANES), 1)
    first = (lane & (DA_QK_DIM // 2)) == 0

    def query_store(ref, y, scale):
        for j in range(y.shape[1] // LANES):
            r = rope(y[:, j * LANES:(j + 1) * LANES]) * scale
            ref[:, (2 * j) * LANES:(2 * j + 1) * LANES] = jnp.where(first, r, 0.0).astype(BF16)
            ref[:, (2 * j + 1) * LANES:(2 * j + 2) * LANES] = jnp.where(first, 0.0, r).astype(BF16)

    bw = BRANCH_WIDTH
    query_store(qa_ref, proj(0, bw), DA_QK_DIM ** -0.5 * math.log2(math.e))
    key_store(ka_ref, proj(bw, bw))
    va_ref[...] = proj(2 * bw, bw).astype(BF16)
    u_ref[...] = proj(3 * bw, bw).astype(BF16)
    query_store(qw_ref, proj(4 * bw, bw), WA_HEAD_DIM ** -0.5 * math.log2(math.e))
    kvw = proj(5 * bw, 2 * LANES)
    key_store(kw_ref, kvw[:, :LANES])
    vw_ref[...] = kvw[:, LANES:].astype(BF16)


def _tile_maps(nb, n, ctx):
    n_lat = nb * n // TM
    per_seq = n // TM

    def mod_map(t):
        return (jnp.where(t < n_lat, t // per_seq, nb), 0, 0)

    def rope_map(t):
        return (jnp.where(t < n_lat, t % per_seq, per_seq), 0)

    return n_lat, mod_map, rope_map


def _proj_call(tok, mod, g, w1, cos, sin, nb, n):
    nt, d = tok.shape
    _, mod_map, rope_map = _tile_maps(nb, n, CTX_LEN)
    row = lambda t: (t, 0)
    spec = lambda w: pl.BlockSpec((TM, w), row)
    sd = lambda w: jax.ShapeDtypeStruct((nt, w), BF16)
    widths = (2 * BRANCH_WIDTH, BRANCH_WIDTH, BRANCH_WIDTH, BRANCH_WIDTH, 2 * BRANCH_WIDTH, LANES, LANES)
    return pl.pallas_call(
        _proj_kernel,
        grid=(nt // TM,),
        in_specs=[pl.BlockSpec((TM, d), row),
                  pl.BlockSpec((1, 6, d), mod_map),
                  _const_spec((1, d)),
                  _const_spec(w1.shape),
                  pl.BlockSpec((TM, LANES), rope_map),
                  pl.BlockSpec((TM, LANES), rope_map)],
        out_specs=[spec(w) for w in widths],
        out_shape=[sd(w) for w in widths],
        compiler_params=_params(1),
        name="in_proj",
    )(tok, mod, g, w1, cos, sin)


def _da_lambda(dl_ref, lam_init):
    lp = dl_ref[...]
    return (jnp.exp(jnp.sum(lp[0:1] * lp[1:2], axis=-1, keepdims=True))
            - jnp.exp(jnp.sum(lp[2:3] * lp[3:4], axis=-1, keepdims=True)) + lam_init)


def _da_scores(k, qs):
    return lax.dot_general(k, qs, (((1,), (1,)), ((), ())), preferred_element_type=F32)


def _da_fold(x, op):
    return op(x.reshape(x.shape[0] // 8, 8, x.shape[1]), axis=0)


def _da_pv(v, e):
    return lax.dot_general(v, e.astype(BF16), (((0,), (0,)), ((), ())), preferred_element_type=F32)


def _da_finish(o_ref, g_ref, lam, lam_init, l8, acc):
    tq = o_ref.shape[0]
    l = jnp.sum(l8, axis=0, keepdims=True)
    ot = acc[:, :tq] / l[:, :tq] - lam * (acc[:, tq:] / l[:, tq:])
    o_ref[...] = (_rms(ot.T, g_ref[...]) * (1.0 - lam_init)).astype(BF16)


def _da_kernel(dl_ref, g_ref, q_ref, kl_ref, kc_ref, vl_ref, vc_ref, o_ref, s_scr, m_scr, *, lam_init):
    t = pl.program_id(0)
    lam = _da_lambda(dl_ref, lam_init)
    n_lat = kl_ref.shape[0]
    qs = jnp.concatenate([q_ref[:, :LANES], q_ref[:, LANES:]], axis=0)

    @pl.when(t == 0)
    def _():
        s_scr[1] = jnp.zeros(s_scr.shape[1:], F32)
        m_scr[1] = jnp.zeros(m_scr.shape[1:], F32)

    def step(wr, rd):
        m_prev = m_scr[rd]
        m8 = l8 = acc = None
        for c in range(n_lat // CK):
            rows = slice(c * CK, (c + 1) * CK)
            s = _da_scores(kl_ref[rows, :], qs)
            s_scr[wr, rows, :] = s
            f = _da_fold(s, jnp.max)
            m8 = f if m8 is None else jnp.maximum(m8, f)
            e = jnp.exp2(s_scr[rd, rows, :] - m_prev)
            f = _da_fold(e, jnp.sum)
            p = _da_pv(vl_ref[rows, :], e)
            l8 = f if l8 is None else l8 + f
            acc = p if acc is None else acc + p
        rows = slice(n_lat, n_lat + CTX_LEN)
        sc = _da_scores(kc_ref[...], qs)
        s_scr[wr, rows, :] = sc
        m_scr[wr] = jnp.max(jnp.maximum(m8, _da_fold(sc, jnp.max)), axis=0, keepdims=True)
        e = jnp.exp2(s_scr[rd, rows, :] - m_prev)
        _da_finish(o_ref, g_ref, lam, lam_init, l8 + _da_fold(e, jnp.sum), acc + _da_pv(vc_ref[...], e))

    @pl.when(t % 2 == 0)
    def _():
        step(0, 1)

    @pl.when(t % 2 == 1)
    def _():
        step(1, 0)


def _da_call(qa, ka, va, da_lambda, g_sub, lam_init, nb, n):
    nt = qa.shape[0]
    n_q = n // TQ
    units = nb * DA_HEADS * n_q
    ctx0 = nb * n // CTX_LEN

    def unit(u):
        return u // (n_q * DA_HEADS), (u // n_q) % DA_HEADS, u % n_q

    def scored(t):
        return unit(jnp.minimum(t, units - 1))

    def finished(t):
        return unit(jnp.maximum(t - 1, 0))

    def tile_map(which):
        def m(t):
            b, h, i = which(t)
            return (b * n_q + i, h)
        return m

    def lat(which):
        return pl.BlockSpec((n, DA_V_DIM), lambda t: which(t)[:2])

    def ctx(which):
        def m(t):
            b, h, _ = which(t)
            return (ctx0 + b, h)
        return pl.BlockSpec((CTX_LEN, DA_V_DIM), m)

    kern = functools.partial(_da_kernel, lam_init=lam_init)
    return pl.pallas_call(
        kern,
        grid=(units + 1,),
        in_specs=[pl.BlockSpec(da_lambda.shape, lambda t: (0, 0)),
                  pl.BlockSpec((1, DA_V_DIM), lambda t: (0, 0)),
                  pl.BlockSpec((TQ, 2 * DA_V_DIM), tile_map(scored)),
                  lat(scored), ctx(scored), lat(finished), ctx(finished)],
        out_specs=pl.BlockSpec((TQ, DA_V_DIM), tile_map(finished)),
        out_shape=jax.ShapeDtypeStruct((nb * n, BRANCH_WIDTH), BF16),
        scratch_shapes=[pltpu.VMEM((2, n + CTX_LEN, 2 * TQ), F32), pltpu.VMEM((2, 1, 2 * TQ), F32)],
        compiler_params=_params(1),
        name="diff_attn",
    )(da_lambda, g_sub, qa, ka, ka, va, va)


def _da_ctx_kernel(dl_ref, g_ref, q_ref, k_ref, v_ref, o_ref, *, lam_init):
    qs = jnp.concatenate([q_ref[:, :LANES], q_ref[:, LANES:]], axis=0)
    s = _da_scores(k_ref[...], qs)
    e = jnp.exp2(s - jnp.max(s, axis=0, keepdims=True))
    _da_finish(o_ref, g_ref, _da_lambda(dl_ref, lam_init), lam_init, _da_fold(e, jnp.sum), _da_pv(v_ref[...], e))


def _da_ctx_call(qa, ka, va, da_lambda, g_sub, lam_init, nb, n):
    ctx0 = nb * n // CTX_LEN
    blk = lambda w: pl.BlockSpec((CTX_LEN, w), lambda b, h: (ctx0 + b, h))
    kern = functools.partial(_da_ctx_kernel, lam_init=lam_init)
    return pl.pallas_call(
        kern,
        grid=(nb, DA_HEADS),
        in_specs=[pl.BlockSpec(da_lambda.shape, lambda b, h: (0, 0)),
                  pl.BlockSpec((1, DA_V_DIM), lambda b, h: (0, 0)),
                  blk(2 * DA_V_DIM), blk(DA_V_DIM), blk(DA_V_DIM)],
        out_specs=pl.BlockSpec((CTX_LEN, DA_V_DIM), lambda b, h: (b, h)),
        out_shape=jax.ShapeDtypeStruct((nb * CTX_LEN, BRANCH_WIDTH), BF16),
        compiler_params=_params(2),
        name="diff_attn_ctx",
    )(da_lambda, g_sub, qa, ka, va)


def _wa_pair_queries(q_ref, j):
    return jnp.concatenate([q_ref[:, (2 * j) * LANES:(2 * j + 1) * LANES],
                            q_ref[:, (2 * j + 1) * LANES:(2 * j + 2) * LANES]], axis=0)


def _wa_pair_sinks(sink_ref, j):
    lane = lax.broadcasted_iota(jnp.int32, (1, 2 * WA_BLOCK), 1)
    return jnp.where(lane < WA_BLOCK, sink_ref[j], sink_ref[j + WA_HEADS // 2]) * math.log2(math.e)


def _wa_pair_max(s, sk):
    return jnp.maximum(jnp.max(_da_fold(s, jnp.max), axis=0, keepdims=True), sk)


def _wa_pair_finish(o_ref, j, s, m, sk, v):
    e = jnp.exp2(s - m)
    den = jnp.sum(_da_fold(e, jnp.sum), axis=0, keepdims=True) + jnp.exp2(sk - m)
    ot = _da_pv(v, e) / den
    first = lax.broadcasted_iota(jnp.int32, (LANES, WA_BLOCK), 0) < WA_HEAD_DIM
    o_ref[:, j * LANES:(j + 1) * LANES] = jnp.where(first, ot[:, :WA_BLOCK], ot[:, WA_BLOCK:]).T.astype(BF16)


def _wa_kernel(sink_ref, q_ref, kp_ref, kc_ref, kn_ref, kx_ref, vp_ref, vc_ref, vn_ref, vx_ref,
               o_ref, s_scr, m_scr, *, n_blk, n, units):
    t = pl.program_id(0)
    i = lax.rem(jnp.minimum(t, units - 1), n_blk)

    @pl.when(t == 0)
    def _():
        s_scr[1] = jnp.zeros(s_scr.shape[1:], F32)
        m_scr[1] = jnp.zeros(m_scr.shape[1:], F32)

    def step(wr, rd):
        k = jnp.concatenate([kp_ref[...], kc_ref[...], kn_ref[...], kx_ref[...]], axis=0)
        v = jnp.concatenate([vp_ref[...], vc_ref[...], vn_ref[...], vx_ref[...]], axis=0)
        band = 3 * WA_BLOCK
        shape = (band + CTX_LEN, 2 * WA_BLOCK)
        key = lax.broadcasted_iota(jnp.int32, shape, 0)
        qpos = lax.broadcasted_iota(jnp.int32, shape, 1) & (WA_BLOCK - 1)
        kpos = (i - 1) * WA_BLOCK + key
        ok = (key >= band) | ((jnp.abs(key - WA_BLOCK - qpos) <= WA_WINDOW) & (kpos >= 0) & (kpos < n))
        for j in range(WA_HEADS // 2):
            cols = slice(j * 2 * WA_BLOCK, (j + 1) * 2 * WA_BLOCK)
            sk = _wa_pair_sinks(sink_ref, j)
            s = jnp.where(ok, _da_scores(k, _wa_pair_queries(q_ref, j)), -jnp.inf)
            s_scr[wr, :, cols] = s
            m_scr[wr, :, cols] = _wa_pair_max(s, sk)
            _wa_pair_finish(o_ref, j, s_scr[rd, :, cols], m_scr[rd, :, cols], sk, v)

    @pl.when(t % 2 == 0)
    def _():
        step(0, 1)

    @pl.when(t % 2 == 1)
    def _():
        step(1, 0)


def _wa_call(qw, kw, vw, sink, nb, n):
    n_blk = n // WA_BLOCK
    units = nb * n_blk
    ctx0 = nb * n // CTX_LEN

    def scored(t):
        u = jnp.minimum(t, units - 1)
        return u // n_blk, u % n_blk

    def finished(t):
        u = jnp.maximum(t - 1, 0)
        return u // n_blk, u % n_blk

    def q_map(which):
        def m(t):
            b, i = which(t)
            return (b * n_blk + i, 0)
        return m

    def near(which, delta):
        def m(t):
            b, i = which(t)
            return (b * n_blk + jnp.clip(i + delta, 0, n_blk - 1), 0)
        return pl.BlockSpec((WA_BLOCK, LANES), m)

    def ctx(which):
        return pl.BlockSpec((CTX_LEN, LANES), lambda t: (ctx0 + which(t)[0], 0))

    keys = WA_WINDOW + WA_BLOCK + WA_WINDOW + CTX_LEN
    kern = functools.partial(_wa_kernel, n_blk=n_blk, n=n, units=units)
    return pl.pallas_call(
        kern,
        grid=(units + 1,),
        in_specs=[pl.BlockSpec(memory_space=pltpu.SMEM),
                  pl.BlockSpec((WA_BLOCK, 2 * BRANCH_WIDTH), q_map(scored)),
                  near(scored, -1), near(scored, 0), near(scored, 1), ctx(scored),
                  near(finished, -1), near(finished, 0), near(finished, 1), ctx(finished)],
        out_specs=pl.BlockSpec((WA_BLOCK, BRANCH_WIDTH), q_map(finished)),
        out_shape=jax.ShapeDtypeStruct((nb * n, BRANCH_WIDTH), BF16),
        scratch_shapes=[pltpu.VMEM((2, keys, WA_HEADS * WA_BLOCK), F32),
                        pltpu.VMEM((2, 1, WA_HEADS * WA_BLOCK), F32)],
        compiler_params=_params(1),
        name="window_attn",
    )(sink, qw, kw, kw, kw, kw, vw, vw, vw, vw)


def _wa_ctx_kernel(sink_ref, q_ref, k_ref, v_ref, o_ref):
    k = k_ref[...]
    s_next = _da_scores(k, _wa_pair_queries(q_ref, 0))
    for j in range(WA_HEADS // 2):
        s = s_next
        if j + 1 < WA_HEADS // 2:
            s_next = _da_scores(k, _wa_pair_queries(q_ref, j + 1))
        sk = _wa_pair_sinks(sink_ref, j)
        _wa_pair_finish(o_ref, j, s, _wa_pair_max(s, sk), sk, v_ref[...])


def _wa_ctx_call(qw, kw, vw, sink, nb, n):
    ctx_blk = CTX_LEN // WA_BLOCK
    ctx0_q = nb * n // WA_BLOCK
    ctx0_k = nb * n // CTX_LEN
    keys = pl.BlockSpec((CTX_LEN, LANES), lambda b, i: (ctx0_k + b, 0))
    return pl.pallas_call(
        _wa_ctx_kernel,
        grid=(nb, ctx_blk),
        in_specs=[pl.BlockSpec(memory_space=pltpu.SMEM),
                  pl.BlockSpec((WA_BLOCK, 2 * BRANCH_WIDTH), lambda b, i: (ctx0_q + b * ctx_blk + i, 0)),
                  keys, keys],
        out_specs=pl.BlockSpec((WA_BLOCK, BRANCH_WIDTH), lambda b, i: (b * ctx_blk + i, 0)),
        out_shape=jax.ShapeDtypeStruct((nb * CTX_LEN, BRANCH_WIDTH), BF16),
        compiler_params=_params(2),
        name="window_attn_ctx",
    )(sink, qw, kw, vw)


def _pool_band():
    ext = POOL_SUB + 2 * POOL_HALO
    t = np.arange(POOL_SUB)[:, None]
    d = np.arange(ext)[None, :] - POOL_HALO - t
    return np.stack([((d >= -(w // 2)) & (d < w - w // 2)) for w in POOL_WINDOWS]).astype(np.float32)


def _merge_kernel(x_ref, mod_ref, g0_ref, g1_ref, wg_ref, yal_ref, yax_ref, ycl_ref, ycx_ref,
                  u_ref, up_ref, un_ref, band_ref, pw_ref, ps_ref, wb_ref, wo_ref, o_ref, *, n_lat, n):
    t = pl.program_id(0)
    x = x_ref[...]
    h = _modulate(x, g0_ref[...], mod_ref[0, 0:1, :], mod_ref[0, 1:2, :]).astype(BF16)
    is_lat = t < n_lat
    ya = jnp.where(is_lat, yal_ref[...], yax_ref[...])
    yc = jnp.where(is_lat, ycl_ref[...], ycx_ref[...])

    u = u_ref[...]
    halo_one = jnp.ones((POOL_HALO, POOL_GROUP), BF16)
    halo_zero = jnp.zeros((POOL_HALO, POOL_GROUP), BF16)
    yb_parts = []
    for sb in range(TM // POOL_SUB):
        r0 = sb * POOL_SUB
        prev = up_ref[...] if sb == 0 else u[r0 - POOL_HALO:r0]
        nxt = un_ref[...] if r0 + POOL_SUB == TM else u[r0 + POOL_SUB:r0 + POOL_SUB + POOL_HALO]
        cur = u[r0:r0 + POOL_SUB]
        g_row = t * TM + r0
        seq_row = jnp.where(is_lat, g_row % n, g_row % CTX_LEN)
        seq_len = jnp.where(is_lat, n, CTX_LEN)
        prev_ok = seq_row != 0
        next_ok = seq_row + POOL_SUB != seq_len
        ext = jnp.concatenate([jnp.where(prev_ok, prev, jnp.zeros_like(prev)), cur,
                               jnp.where(next_ok, nxt, jnp.zeros_like(nxt))], axis=0)
        ones = jnp.concatenate([jnp.where(prev_ok, halo_one, halo_zero),
                                jnp.ones((POOL_SUB, POOL_GROUP), BF16),
                                jnp.where(next_ok, halo_one, halo_zero)], axis=0)
        cols = []
        for g in range(POOL_GROUPS):
            sl = slice(g * POOL_GROUP, (g + 1) * POOL_GROUP)
            both = jnp.dot(band_ref[g], jnp.concatenate([ext[:, sl], ones], axis=1), preferred_element_type=F32)
            pooled = (both[:, :POOL_GROUP] / both[:, POOL_GROUP:] - cur[:, sl].astype(F32)).astype(BF16)
            cols.append(jnp.dot(pooled, pw_ref[g], preferred_element_type=F32) * ps_ref[:, sl])
        yb_parts.append(jnp.concatenate(cols, axis=1))
    yb = jnp.concatenate(yb_parts, axis=0).astype(BF16)

    branches = (ya, yb, yc)
    m = None
    for k in range(N_BRANCH):
        gate = jnp.dot(h, wg_ref[:, k * D_MODEL:(k + 1) * D_MODEL], preferred_element_type=F32)
        gate = 1.0 / (1.0 + jnp.exp(-gate))
        term = gate * jnp.dot(branches[k], wb_ref[k], preferred_element_type=F32)
        m = term if m is None else m + term
    mix = jnp.dot(m.astype(BF16), wo_ref[...], preferred_element_type=F32)
    o_ref[...] = x + mod_ref[0, 2:3, :] * _rms(mix, g1_ref[...])


def _merge_call(tok, mod, g0, g1, wg, ya_lat, ya_ctx, yc_lat, yc_ctx, u, band, pool_w, pool_scale, wb, wo,
                nb, n, n_tiles):
    nt, d = tok.shape
    n_lat, mod_map, _ = _tile_maps(nb, n, CTX_LEN)
    row = lambda t: (t, 0)
    per_tile = TM // POOL_HALO
    last = nt // POOL_HALO - 1
    wide = pl.BlockSpec((TM, BRANCH_WIDTH), row)
    n_ctx = ya_ctx.shape[0] // TM
    lat_rows = pl.BlockSpec((TM, BRANCH_WIDTH), lambda t: (jnp.minimum(t, n_lat - 1), 0))
    ctx_rows = pl.BlockSpec((TM, BRANCH_WIDTH), lambda t: (jnp.clip(t - n_lat, 0, n_ctx - 1), 0))
    kern = functools.partial(_merge_kernel, n_lat=n_lat, n=n)
    return pl.pallas_call(
        kern,
        grid=(n_tiles,),
        in_specs=[pl.BlockSpec((TM, d), row),
                  pl.BlockSpec((1, 6, d), mod_map),
                  _const_spec((1, d)), _const_spec((1, d)),
                  _const_spec(wg.shape),
                  lat_rows, ctx_rows, lat_rows, ctx_rows, wide,
                  pl.BlockSpec((POOL_HALO, BRANCH_WIDTH), lambda t: (jnp.maximum(t * per_tile - 1, 0), 0)),
                  pl.BlockSpec((POOL_HALO, BRANCH_WIDTH), lambda t: (jnp.minimum((t + 1) * per_tile, last), 0)),
                  _const_spec(band.shape), _const_spec(pool_w.shape), _const_spec((1, BRANCH_WIDTH)),
                  _const_spec(wb.shape), _const_spec(wo.shape)],
        out_specs=pl.BlockSpec((TM, d), row),
        out_shape=jax.ShapeDtypeStruct((n_tiles * TM, d), F32),
        compiler_params=_params(1),
        name="pool_merge",
    )(tok, mod, g0, g1, wg, ya_lat, ya_ctx, yc_lat, yc_ctx, u, u, u, band, pool_w, pool_scale, wb, wo)


def _ffn_kernel(x_ref, mod_ref, g2_ref, g3_ref, wi_ref, wo_ref, o_ref):
    x = x_ref[...]
    h = _modulate(x, g2_ref[...], mod_ref[0, 3:4, :], mod_ref[0, 4:5, :]).astype(BF16)
    step = 256
    acc = None
    for c in range(D_FF // step):
        gate = jnp.dot(h, wi_ref[:, c * step:(c + 1) * step], preferred_element_type=F32)
        up = jnp.dot(h, wi_ref[:, D_FF + c * step:D_FF + (c + 1) * step], preferred_element_type=F32)
        act = (gate * (1.0 / (1.0 + jnp.exp(-gate))) * up).astype(BF16)
        part = jnp.dot(act, wo_ref[c * step:(c + 1) * step, :], preferred_element_type=F32)
        acc = part if acc is None else acc + part
    o_ref[...] = x + mod_ref[0, 5:6, :] * _rms(acc, g3_ref[...])


def _ffn_call(tok, mod, g2, g3, wi, wo, nb, n):
    nt, d = tok.shape
    _, mod_map, _ = _tile_maps(nb, n, CTX_LEN)
    row = lambda t: (t, 0)
    return pl.pallas_call(
        _ffn_kernel,
        grid=(nt // TM,),
        in_specs=[pl.BlockSpec((TM, d), row),
                  pl.BlockSpec((1, 6, d), mod_map),
                  _const_spec((1, d)), _const_spec((1, d)),
                  _const_spec(wi.shape), _const_spec(wo.shape)],
        out_specs=pl.BlockSpec((TM, d), row),
        out_shape=jax.ShapeDtypeStruct((nt, d), F32),
        compiler_params=_params(1),
        name="swiglu_ffn",
    )(tok, mod, g2, g3, wi, wo)


def _column_order():
    bw = BRANCH_WIDTH
    order = np.arange(PROJ_WIDTH)
    half = DA_QK_DIM // 2
    for base in (0, bw):
        for h in range(DA_HEADS):
            for p in range(2):
                for s in range(2):
                    for i in range(half):
                        order[base + h * DA_V_DIM + p * 64 + s * half + i] = (
                            base + h * DA_V_DIM + s * DA_QK_DIM + 2 * i + p)
    half = WA_HEAD_DIM // 2
    base = 4 * bw
    for j in range(WA_HEADS // 2):
        for p in range(2):
            for g in range(2):
                for i in range(half):
                    order[base + j * LANES + p * 64 + g * half + i] = (
                        base + (j + (WA_HEADS // 2) * g) * WA_HEAD_DIM + 2 * i + p)
    base = 5 * bw
    for p in range(2):
        for kv in range(WA_KV_HEADS):
            for i in range(half):
                order[base + p * 64 + kv * half + i] = base + kv * WA_HEAD_DIM + 2 * i + p
    return order


def _wa_row_order():
    order = np.arange(BRANCH_WIDTH)
    for j in range(WA_HEADS // 2):
        for g in range(2):
            for dd in range(WA_HEAD_DIM):
                order[j * LANES + g * WA_HEAD_DIM + dd] = (j + (WA_HEADS // 2) * g) * WA_HEAD_DIM + dd
    return order


def _rope_tables(n):
    rows = n // GRID_W
    row = jnp.repeat(jnp.arange(rows, dtype=F32), GRID_W)
    col = jnp.tile(jnp.arange(GRID_W, dtype=F32), rows)
    n_freq = DA_QK_DIM // 4
    inv = ROPE_THETA ** (-jnp.arange(n_freq, dtype=F32) / n_freq)
    ang = jnp.concatenate([row[:, None] * inv, col[:, None] * inv], axis=-1)
    cos = jnp.tile(jnp.cos(ang), (1, 4))
    sin = jnp.tile(jnp.sin(ang), (1, 4)) * jnp.where(jnp.arange(LANES) < 64, -1.0, 1.0)
    cos = jnp.concatenate([cos, jnp.ones((TM, LANES), F32)], axis=0)
    sin = jnp.concatenate([sin, jnp.zeros((TM, LANES), F32)], axis=0)
    return cos, sin


def kernel(x, c, ctx, c_ctx, w_ada, b_ada, norm_g, w_in, da_lambda, da_subln_g,
           pool_w, pool_scale, wa_sink, w_branch, w_out, w_ffn_in, w_ffn_out):
    nb, n, d = x.shape
    depth = w_ada.shape[0]
    assert d == D_MODEL and ctx.shape[1] == CTX_LEN
    assert n % TM == 0 and (nb * CTX_LEN) % TM == 0 and n % CK == 0 and n % GRID_W == 0
    assert TQ == CTX_LEN and n % TQ == 0
    assert TM % POOL_SUB == 0 and CTX_LEN % POOL_SUB == 0 and (nb * n) % CTX_LEN == 0
    assert WA_HEAD_DIM == DA_QK_DIM

    tok = jnp.concatenate([x.reshape(nb * n, d), ctx.reshape(nb * CTX_LEN, d)], axis=0)
    cvec = jnp.concatenate([c, c_ctx[None, :]], axis=0)
    mod = _mod_call(cvec, w_ada, b_ada).reshape(depth, nb + 1, 6, d)
    cos, sin = _rope_tables(n)
    band = jnp.asarray(_pool_band(), BF16)

    w1 = w_in[:, :, _column_order()].astype(BF16)
    wg = w_in[:, :, PROJ_WIDTH:].astype(BF16)
    wb = jnp.concatenate([w_branch[:, :2], w_branch[:, 2:, _wa_row_order()]], axis=1).astype(BF16)
    wo = w_out.astype(BF16)
    wfi = w_ffn_in.astype(BF16)
    wfo = w_ffn_out.astype(BF16)
    pw = pool_w.astype(BF16)

    n_tiles_all = tok.shape[0] // TM
    n_tiles_lat = nb * n // TM
    for l in range(depth):
        last = l == depth - 1
        lam_init = 0.8 - 0.6 * math.exp(-0.3 * l)
        g = norm_g[l].reshape(4, 1, d)
        qa, ka, va, u, qw, kw, vw = _proj_call(tok, mod[l], g[0], w1[l], cos, sin, nb, n)
        g_sub = da_subln_g[l].reshape(1, DA_V_DIM)
        ya = _da_call(qa, ka, va, da_lambda[l], g_sub, lam_init, nb, n)
        ya_ctx = ya if last else _da_ctx_call(qa, ka, va, da_lambda[l], g_sub, lam_init, nb, n)
        yc = _wa_call(qw, kw, vw, wa_sink[l], nb, n)
        yc_ctx = yc if last else _wa_ctx_call(qw, kw, vw, wa_sink[l], nb, n)
        n_tiles = n_tiles_lat if last else n_tiles_all
        tok = _merge_call(tok, mod[l], g[0], g[1], wg[l], ya, ya_ctx, yc, yc_ctx, u, band, pw[l],
                          pool_scale[l].reshape(1, BRANCH_WIDTH), wb[l], wo[l], nb, n, n_tiles)
        tok = _ffn_call(tok, mod[l], g[2], g[3], wfi[l], wfo[l], nb, n)
    return tok[:nb * n].reshape(nb, n, d)
```

```python
import functools
import math

import numpy as np
import jax
import jax.numpy as jnp
from jax import lax
from jax.experimental import pallas as pl
from jax.experimental.pallas import tpu as pltpu

F32 = jnp.float32
BF16 = jnp.bfloat16

D_MODEL = 1024
CTX_LEN = 256
GRID_W = 64
EPS = 1e-6
ROPE_THETA = 10000.0
BRANCH_WIDTH = D_MODEL // 2
N_BRANCH = 3
DA_HEADS = 4
DA_V_DIM = BRANCH_WIDTH // DA_HEADS
DA_QK_DIM = DA_V_DIM // 2
POOL_WINDOWS = (2, 4, 8, 16)
POOL_GROUPS = 4
POOL_GROUP = BRANCH_WIDTH // POOL_GROUPS
WA_HEAD_DIM = 64
WA_HEADS = BRANCH_WIDTH // WA_HEAD_DIM
WA_KV_HEADS = 2
WA_GROUP = WA_HEADS // WA_KV_HEADS
WA_WINDOW = 128
WA_BLOCK = 128
D_FF = ((8 * D_MODEL + 3 * 256 - 1) // (3 * 256)) * 256
PROJ_WIDTH = 5 * BRANCH_WIDTH + 2 * WA_KV_HEADS * WA_HEAD_DIM
GATE_WIDTH = N_BRANCH * D_MODEL

LANES = 128
BF16_SUBLANES = 16
VMEM_LIMIT = 56 * 1024 * 1024

TM = 512
TQ = 512
CK = 256
POOL_SUB = 128
POOL_HALO = BF16_SUBLANES


def _params(n_axes):
    return pltpu.CompilerParams(dimension_semantics=("arbitrary",) * n_axes,
                                vmem_limit_bytes=VMEM_LIMIT)


def _const_spec(shape):
    nd = len(shape)
    return pl.BlockSpec(shape, lambda *_: (0,) * nd, pipeline_mode=pl.Buffered(1))


def _rms(x, g):
    return x * lax.rsqrt(jnp.mean(x * x, axis=-1, keepdims=True) + EPS) * g


def _modulate(x, g, shift, scale):
    return _rms(x, g) * (1.0 + scale) + shift


def _mod_kernel(c_ref, w_ref, b_ref, o_ref):
    c = c_ref[...]
    a = c * (1.0 / (1.0 + jnp.exp(-c)))
    o_ref[0] = jnp.dot(a, w_ref[0], preferred_element_type=F32,
                       precision=lax.Precision.HIGHEST) + b_ref[0]


def _mod_call(cvec, w_ada, b_ada):
    depth, d, width = w_ada.shape
    rows = cvec.shape[0]
    tn = 1536
    return pl.pallas_call(
        _mod_kernel,
        grid=(depth, width // tn),
        in_specs=[pl.BlockSpec((rows, d), lambda l, j: (0, 0)),
                  pl.BlockSpec((1, d, tn), lambda l, j: (l, 0, j)),
                  pl.BlockSpec((1, 1, tn), lambda l, j: (l, 0, j))],
        out_specs=pl.BlockSpec((1, rows, tn), lambda l, j: (l, 0, j)),
        out_shape=jax.ShapeDtypeStruct((depth, rows, width), F32),
        compiler_params=_params(2),
        name="adaln_mod",
    )(cvec, w_ada, b_ada.reshape(depth, 1, width))


def _proj_kernel(x_ref, mod_ref, g_ref, w_ref, c_ref, s_ref,
                 qa_ref, ka_ref, va_ref, u_ref, qw_ref, kw_ref, vw_ref):
    h = _modulate(x_ref[...], g_ref[...], mod_ref[0, 0:1, :], mod_ref[0, 1:2, :]).astype(BF16)
    cos = c_ref[...]
    sin = s_ref[...]

    def proj(lo, width):
        return jnp.dot(h, w_ref[:, lo:lo + width], preferred_element_type=F32)

    def rope(yj):
        return yj * cos + pltpu.roll(yj, LANES // 2, 1) * sin

    def key_store(ref, y):
        for j in range(y.shape[1] // LANES):
            ref[:, j * LANES:(j + 1) * LANES] = rope(y[:, j * LANES:(j + 1) * LANES]).astype(BF16)

    lane = lax.broadcasted_iota(jnp.int32, (TM, LANES), 1)
    first = (lane & (DA_QK_DIM // 2)) == 0

    def query_store(ref, y, scale):
        for j in range(y.shape[1] // LANES):
            r = rope(y[:, j * LANES:(j + 1) * LANES]) * scale
            ref[:, (2 * j) * LANES:(2 * j + 1) * LANES] = jnp.where(first, r, 0.0).astype(BF16)
            ref[:, (2 * j + 1) * LANES:(2 * j + 2) * LANES] = jnp.where(first, 0.0, r).astype(BF16)

    bw = BRANCH_WIDTH
    query_store(qa_ref, proj(0, bw), DA_QK_DIM ** -0.5 * math.log2(math.e))
    key_store(ka_ref, proj(bw, bw))
    va_ref[...] = proj(2 * bw, bw).astype(BF16)
    u_ref[...] = proj(3 * bw, bw).astype(BF16)
    query_store(qw_ref, proj(4 * bw, bw), WA_HEAD_DIM ** -0.5 * math.log2(math.e))
    kvw = proj(5 * bw, 2 * LANES)
    key_store(kw_ref, kvw[:, :LANES])
    vw_ref[...] = kvw[:, LANES:].astype(BF16)


def _tile_maps(nb, n, ctx):
    n_lat = nb * n // TM
    per_seq = n // TM

    def mod_map(t):
        return (jnp.where(t < n_lat, t // per_seq, nb), 0, 0)

    def rope_map(t):
        return (jnp.where(t < n_lat, t % per_seq, per_seq), 0)

    return n_lat, mod_map, rope_map


def _proj_call(tok, mod, g, w1, cos, sin, nb, n):
    nt, d = tok.shape
    _, mod_map, rope_map = _tile_maps(nb, n, CTX_LEN)
    row = lambda t: (t, 0)
    spec = lambda w: pl.BlockSpec((TM, w), row)
    sd = lambda w: jax.ShapeDtypeStruct((nt, w), BF16)
    widths = (2 * BRANCH_WIDTH, BRANCH_WIDTH, BRANCH_WIDTH, BRANCH_WIDTH, 2 * BRANCH_WIDTH, LANES, LANES)
    return pl.pallas_call(
        _proj_kernel,
        grid=(nt // TM,),
        in_specs=[pl.BlockSpec((TM, d), row),
                  pl.BlockSpec((1, 6, d), mod_map),
                  _const_spec((1, d)),
                  _const_spec(w1.shape),
                  pl.BlockSpec((TM, LANES), rope_map),
                  pl.BlockSpec((TM, LANES), rope_map)],
        out_specs=[spec(w) for w in widths],
        out_shape=[sd(w) for w in widths],
        compiler_params=_params(1),
        name="in_proj",
    )(tok, mod, g, w1, cos, sin)


def _da_lambda(dl_ref, lam_init):
    lp = dl_ref[...]
    return (jnp.exp(jnp.sum(lp[0:1] * lp[1:2], axis=-1, keepdims=True))
            - jnp.exp(jnp.sum(lp[2:3] * lp[3:4], axis=-1, keepdims=True)) + lam_init)


def _da_scores(k, qs):
    return lax.dot_general(k, qs, (((1,), (1,)), ((), ())), preferred_element_type=F32)


def _da_fold(x, op):
    return op(x.reshape(x.shape[0] // 8, 8, x.shape[1]), axis=0)


def _da_pv(v, e):
    return lax.dot_general(v, e.astype(BF16), (((0,), (0,)), ((), ())), preferred_element_type=F32)


def _da_finish(o_ref, g_ref, lam, lam_init, l8, acc):
    tq = o_ref.shape[0]
    l = jnp.sum(l8, axis=0, keepdims=True)
    ot = acc[:, :tq] / l[:, :tq] - lam * (acc[:, tq:] / l[:, tq:])
    o_ref[...] = (_rms(ot.T, g_ref[...]) * (1.0 - lam_init)).astype(BF16)


def _da_kernel(dl_ref, g_ref, q_ref, kl_ref, kc_ref, vl_ref, vc_ref, o_ref, s_scr, m_scr, *, lam_init):
    t = pl.program_id(0)
    lam = _da_lambda(dl_ref, lam_init)
    n_lat = kl_ref.shape[0]
    qs = jnp.concatenate([q_ref[:, :LANES], q_ref[:, LANES:]], axis=0)

    @pl.when(t == 0)
    def _():
        s_scr[1] = jnp.zeros(s_scr.shape[1:], F32)
        m_scr[1] = jnp.zeros(m_scr.shape[1:], F32)

    def step(wr, rd):
        m_prev = m_scr[rd]
        m8 = l8 = acc = None
        for c in range(n_lat // CK):
            rows = slice(c * CK, (c + 1) * CK)
            s = _da_scores(kl_ref[rows, :], qs)
            s_scr[wr, rows, :] = s
            f = _da_fold(s, jnp.max)
            m8 = f if m8 is None else jnp.maximum(m8, f)
            e = jnp.exp2(s_scr[rd, rows, :] - m_prev)
            f = _da_fold(e, jnp.sum)
            p = _da_pv(vl_ref[rows, :], e)
            l8 = f if l8 is None else l8 + f
            acc = p if acc is None else acc + p
        rows = slice(n_lat, n_lat + CTX_LEN)
        sc = _da_scores(kc_ref[...], qs)
        s_scr[wr, rows, :] = sc
        m_scr[wr] = jnp.max(jnp.maximum(m8, _da_fold(sc, jnp.max)), axis=0, keepdims=True)
        e = jnp.exp2(s_scr[rd, rows, :] - m_prev)
        _da_finish(o_ref, g_ref, lam, lam_init, l8 + _da_fold(e, jnp.sum), acc + _da_pv(vc_ref[...], e))

    @pl.when(t % 2 == 0)
    def _():
        step(0, 1)

    @pl.when(t % 2 == 1)
    def _():
        step(1, 0)


def _da_call(qa, ka, va, da_lambda, g_sub, lam_init, nb, n):
    nt = qa.shape[0]
    n_q = n // TQ
    units = nb * DA_HEADS * n_q
    ctx0 = nb * n // CTX_LEN

    def unit(u):
        return u // (n_q * DA_HEADS), (u // n_q) % DA_HEADS, u % n_q

    def scored(t):
        return unit(jnp.minimum(t, units - 1))

    def finished(t):
        return unit(jnp.maximum(t - 1, 0))

    def tile_map(which):
        def m(t):
            b, h, i = which(t)
            return (b * n_q + i, h)
        return m

    def lat(which):
        return pl.BlockSpec((n, DA_V_DIM), lambda t: which(t)[:2])

    def ctx(which):
        def m(t):
            b, h, _ = which(t)
            return (ctx0 + b, h)
        return pl.BlockSpec((CTX_LEN, DA_V_DIM), m)

    kern = functools.partial(_da_kernel, lam_init=lam_init)
    return pl.pallas_call(
        kern,
        grid=(units + 1,),
        in_specs=[pl.BlockSpec(da_lambda.shape, lambda t: (0, 0)),
                  pl.BlockSpec((1, DA_V_DIM), lambda t: (0, 0)),
                  pl.BlockSpec((TQ, 2 * DA_V_DIM), tile_map(scored)),
                  lat(scored), ctx(scored), lat(finished), ctx(finished)],
        out_specs=pl.BlockSpec((TQ, DA_V_DIM), tile_map(finished)),
        out_shape=jax.ShapeDtypeStruct((nb * n, BRANCH_WIDTH), BF16),
        scratch_shapes=[pltpu.VMEM((2, n + CTX_LEN, 2 * TQ), F32), pltpu.VMEM((2, 1, 2 * TQ), F32)],
        compiler_params=_params(1),
        name="diff_attn",
    )(da_lambda, g_sub, qa, ka, ka, va, va)


def _da_ctx_kernel(dl_ref, g_ref, q_ref, k_ref, v_ref, o_ref, *, lam_init):
    qs = jnp.concatenate([q_ref[:, :LANES], q_ref[:, LANES:]], axis=0)
    s = _da_scores(k_ref[...], qs)
    e = jnp.exp2(s - jnp.max(s, axis=0, keepdims=True))
    _da_finish(o_ref, g_ref, _da_lambda(dl_ref, lam_init), lam_init, _da_fold(e, jnp.sum), _da_pv(v_ref[...], e))


def _da_ctx_call(qa, ka, va, da_lambda, g_sub, lam_init, nb, n):
    ctx0 = nb * n // CTX_LEN
    blk = lambda w: pl.BlockSpec((CTX_LEN, w), lambda b, h: (ctx0 + b, h))
    kern = functools.partial(_da_ctx_kernel, lam_init=lam_init)
    return pl.pallas_call(
        kern,
        grid=(nb, DA_HEADS),
        in_specs=[pl.BlockSpec(da_lambda.shape, lambda b, h: (0, 0)),
                  pl.BlockSpec((1, DA_V_DIM), lambda b, h: (0, 0)),
                  blk(2 * DA_V_DIM), blk(DA_V_DIM), blk(DA_V_DIM)],
        out_specs=pl.BlockSpec((CTX_LEN, DA_V_DIM), lambda b, h: (b, h)),
        out_shape=jax.ShapeDtypeStruct((nb * CTX_LEN, BRANCH_WIDTH), BF16),
        compiler_params=_params(2),
        name="diff_attn_ctx",
    )(da_lambda, g_sub, qa, ka, va)


def _wa_pair_queries(q_ref, j):
    return jnp.concatenate([q_ref[:, (2 * j) * LANES:(2 * j + 1) * LANES],
                            q_ref[:, (2 * j + 1) * LANES:(2 * j + 2) * LANES]], axis=0)


def _wa_pair_sinks(sink_ref, j):
    lane = lax.broadcasted_iota(jnp.int32, (1, 2 * WA_BLOCK), 1)
    return jnp.where(lane < WA_BLOCK, sink_ref[j], sink_ref[j + WA_HEADS // 2]) * math.log2(math.e)


def _wa_pair_max(s, sk):
    return jnp.maximum(jnp.max(_da_fold(s, jnp.max), axis=0, keepdims=True), sk)


def _wa_pair_finish(o_ref, j, s, m, sk, v):
    e = jnp.exp2(s - m)
    den = jnp.sum(_da_fold(e, jnp.sum), axis=0, keepdims=True) + jnp.exp2(sk - m)
    ot = _da_pv(v, e) / den
    first = lax.broadcasted_iota(jnp.int32, (LANES, WA_BLOCK), 0) < WA_HEAD_DIM
    o_ref[:, j * LANES:(j + 1) * LANES] = jnp.where(first, ot[:, :WA_BLOCK], ot[:, WA_BLOCK:]).T.astype(BF16)


def _wa_band_bias():
    band = 3 * WA_BLOCK
    key = np.arange(band + CTX_LEN)[:, None]
    qpos = np.arange(2 * WA_BLOCK)[None, :] % WA_BLOCK
    ok = (key >= band) | (np.abs(key - WA_BLOCK - qpos) <= WA_WINDOW)
    return np.where(ok, 0.0, -np.inf).astype(np.float32)


def _wa_kernel(sink_ref, bias_ref, q_ref, kp_ref, kc_ref, kn_ref, kx_ref, vp_ref, vc_ref, vn_ref, vx_ref,
               o_ref, s_scr, m_scr, *, n_blk, units):
    t = pl.program_id(0)
    i = lax.rem(jnp.minimum(t, units - 1), n_blk)

    @pl.when(t == 0)
    def _():
        s_scr[1] = jnp.zeros(s_scr.shape[1:], F32)
        m_scr[1] = jnp.zeros(m_scr.shape[1:], F32)

    def step(wr, rd):
        k = jnp.concatenate([kp_ref[...], kc_ref[...], kn_ref[...], kx_ref[...]], axis=0)
        v = jnp.concatenate([vp_ref[...], vc_ref[...], vn_ref[...], vx_ref[...]], axis=0)
        out = jnp.full((WA_BLOCK, 2 * WA_BLOCK), -jnp.inf, F32)
        bias = jnp.concatenate([jnp.where(i > 0, bias_ref[:WA_BLOCK, :], out),
                                bias_ref[WA_BLOCK:2 * WA_BLOCK, :],
                                jnp.where(i < n_blk - 1, bias_ref[2 * WA_BLOCK:3 * WA_BLOCK, :], out),
                                bias_ref[3 * WA_BLOCK:, :]], axis=0)
        for j in range(WA_HEADS // 2):
            cols = slice(j * 2 * WA_BLOCK, (j + 1) * 2 * WA_BLOCK)
            sk = _wa_pair_sinks(sink_ref, j)
            s = _da_scores(k, _wa_pair_queries(q_ref, j)) + bias
            s_scr[wr, :, cols] = s
            m_scr[wr, :, cols] = _wa_pair_max(s, sk)
            _wa_pair_finish(o_ref, j, s_scr[rd, :, cols], m_scr[rd, :, cols], sk, v)

    @pl.when(t % 2 == 0)
    def _():
        step(0, 1)

    @pl.when(t % 2 == 1)
    def _():
        step(1, 0)


def _wa_call(qw, kw, vw, sink, nb, n):
    n_blk = n // WA_BLOCK
    units = nb * n_blk
    ctx0 = nb * n // CTX_LEN

    def scored(t):
        u = jnp.minimum(t, units - 1)
        return u // n_blk, u % n_blk

    def finished(t):
        u = jnp.maximum(t - 1, 0)
        return u // n_blk, u % n_blk

    def q_map(which):
        def m(t):
            b, i = which(t)
            return (b * n_blk + i, 0)
        return m

    def near(which, delta):
        def m(t):
            b, i = which(t)
            return (b * n_blk + jnp.clip(i + delta, 0, n_blk - 1), 0)
        return pl.BlockSpec((WA_BLOCK, LANES), m)

    def ctx(which):
        return pl.BlockSpec((CTX_LEN, LANES), lambda t: (ctx0 + which(t)[0], 0))

    bias = jnp.asarray(_wa_band_bias())
    keys = bias.shape[0]
    kern = functools.partial(_wa_kernel, n_blk=n_blk, units=units)
    return pl.pallas_call(
        kern,
        grid=(units + 1,),
        in_specs=[pl.BlockSpec(memory_space=pltpu.SMEM),
                  _const_spec(bias.shape),
                  pl.BlockSpec((WA_BLOCK, 2 * BRANCH_WIDTH), q_map(scored)),
                  near(scored, -1), near(scored, 0), near(scored, 1), ctx(scored),
                  near(finished, -1), near(finished, 0), near(finished, 1), ctx(finished)],
        out_specs=pl.BlockSpec((WA_BLOCK, BRANCH_WIDTH), q_map(finished)),
        out_shape=jax.ShapeDtypeStruct((nb * n, BRANCH_WIDTH), BF16),
        scratch_shapes=[pltpu.VMEM((2, keys, WA_HEADS * WA_BLOCK), F32),
                        pltpu.VMEM((2, 1, WA_HEADS * WA_BLOCK), F32)],
        compiler_params=_params(1),
        name="window_attn",
    )(sink, bias, qw, kw, kw, kw, kw, vw, vw, vw, vw)


def _wa_ctx_kernel(sink_ref, q_ref, k_ref, v_ref, o_ref):
    k = k_ref[...]
    s_next = _da_scores(k, _wa_pair_queries(q_ref, 0))
    for j in range(WA_HEADS // 2):
        s = s_next
        if j + 1 < WA_HEADS // 2:
            s_next = _da_scores(k, _wa_pair_queries(q_ref, j + 1))
        sk = _wa_pair_sinks(sink_ref, j)
        _wa_pair_finish(o_ref, j, s, _wa_pair_max(s, sk), sk, v_ref[...])


def _wa_ctx_call(qw, kw, vw, sink, nb, n):
    ctx_blk = CTX_LEN // WA_BLOCK
    ctx0_q = nb * n // WA_BLOCK
    ctx0_k = nb * n // CTX_LEN
    keys = pl.BlockSpec((CTX_LEN, LANES), lambda b, i: (ctx0_k + b, 0))
    return pl.pallas_call(
        _wa_ctx_kernel,
        grid=(nb, ctx_blk),
        in_specs=[pl.BlockSpec(memory_space=pltpu.SMEM),
                  pl.BlockSpec((WA_BLOCK, 2 * BRANCH_WIDTH), lambda b, i: (ctx0_q + b * ctx_blk + i, 0)),
                  keys, keys],
        out_specs=pl.BlockSpec((WA_BLOCK, BRANCH_WIDTH), lambda b, i: (b * ctx_blk + i, 0)),
        out_shape=jax.ShapeDtypeStruct((nb * CTX_LEN, BRANCH_WIDTH), BF16),
        compiler_params=_params(2),
        name="window_attn_ctx",
    )(sink, qw, kw, vw)


def _pool_band():
    ext = POOL_SUB + 2 * POOL_HALO
    t = np.arange(POOL_SUB)[:, None]
    d = np.arange(ext)[None, :] - POOL_HALO - t
    return np.stack([((d >= -(w // 2)) & (d < w - w // 2)) for w in POOL_WINDOWS]).astype(np.float32)


def _merge_kernel(x_ref, mod_ref, g0_ref, g1_ref, wg_ref, yal_ref, yax_ref, ycl_ref, ycx_ref,
                  u_ref, up_ref, un_ref, band_ref, pw_ref, ps_ref, wb_ref, wo_ref, o_ref, *, n_lat, n):
    t = pl.program_id(0)
    x = x_ref[...]
    h = _modulate(x, g0_ref[...], mod_ref[0, 0:1, :], mod_ref[0, 1:2, :]).astype(BF16)
    is_lat = t < n_lat
    ya = jnp.where(is_lat, yal_ref[...], yax_ref[...])
    yc = jnp.where(is_lat, ycl_ref[...], ycx_ref[...])

    u = u_ref[...]
    halo_one = jnp.ones((POOL_HALO, POOL_GROUP), BF16)
    halo_zero = jnp.zeros((POOL_HALO, POOL_GROUP), BF16)
    yb_parts = []
    for sb in range(TM // POOL_SUB):
        r0 = sb * POOL_SUB
        prev = up_ref[...] if sb == 0 else u[r0 - POOL_HALO:r0]
        nxt = un_ref[...] if r0 + POOL_SUB == TM else u[r0 + POOL_SUB:r0 + POOL_SUB + POOL_HALO]
        cur = u[r0:r0 + POOL_SUB]
        g_row = t * TM + r0
        seq_row = jnp.where(is_lat, g_row % n, g_row % CTX_LEN)
        seq_len = jnp.where(is_lat, n, CTX_LEN)
        prev_ok = seq_row != 0
        next_ok = seq_row + POOL_SUB != seq_len
        ext = jnp.concatenate([jnp.where(prev_ok, prev, jnp.zeros_like(prev)), cur,
                               jnp.where(next_ok, nxt, jnp.zeros_like(nxt))], axis=0)
        ones = jnp.concatenate([jnp.where(prev_ok, halo_one, halo_zero),
                                jnp.ones((POOL_SUB, POOL_GROUP), BF16),
                                jnp.where(next_ok, halo_one, halo_zero)], axis=0)
        cols = []
        for g in range(POOL_GROUPS):
            sl = slice(g * POOL_GROUP, (g + 1) * POOL_GROUP)
            both = jnp.dot(band_ref[g], jnp.concatenate([ext[:, sl], ones], axis=1), preferred_element_type=F32)
            pooled = (both[:, :POOL_GROUP] / both[:, POOL_GROUP:] - cur[:, sl].astype(F32)).astype(BF16)
            cols.append(jnp.dot(pooled, pw_ref[g], preferred_element_type=F32) * ps_ref[:, sl])
        yb_parts.append(jnp.concatenate(cols, axis=1))
    yb = jnp.concatenate(yb_parts, axis=0).astype(BF16)

    branches = (ya, yb, yc)
    m = None
    for k in range(N_BRANCH):
        gate = jnp.dot(h, wg_ref[:, k * D_MODEL:(k + 1) * D_MODEL], preferred_element_type=F32)
        gate = 1.0 / (1.0 + jnp.exp(-gate))
        term = gate * jnp.dot(branches[k], wb_ref[k], preferred_element_type=F32)
        m = term if m is None else m + term
    mix = jnp.dot(m.astype(BF16), wo_ref[...], preferred_element_type=F32)
    o_ref[...] = x + mod_ref[0, 2:3, :] * _rms(mix, g1_ref[...])


def _merge_call(tok, mod, g0, g1, wg, ya_lat, ya_ctx, yc_lat, yc_ctx, u, band, pool_w, pool_scale, wb, wo,
                nb, n, n_tiles):
    nt, d = tok.shape
    n_lat, mod_map, _ = _tile_maps(nb, n, CTX_LEN)
    row = lambda t: (t, 0)
    per_tile = TM // POOL_HALO
    last = nt // POOL_HALO - 1
    wide = pl.BlockSpec((TM, BRANCH_WIDTH), row)
    n_ctx = ya_ctx.shape[0] // TM
    lat_rows = pl.BlockSpec((TM, BRANCH_WIDTH), lambda t: (jnp.minimum(t, n_lat - 1), 0))
    ctx_rows = pl.BlockSpec((TM, BRANCH_WIDTH), lambda t: (jnp.clip(t - n_lat, 0, n_ctx - 1), 0))
    kern = functools.partial(_merge_kernel, n_lat=n_lat, n=n)
    return pl.pallas_call(
        kern,
        grid=(n_tiles,),
        in_specs=[pl.BlockSpec((TM, d), row),
                  pl.BlockSpec((1, 6, d), mod_map),
                  _const_spec((1, d)), _const_spec((1, d)),
                  _const_spec(wg.shape),
                  lat_rows, ctx_rows, lat_rows, ctx_rows, wide,
                  pl.BlockSpec((POOL_HALO, BRANCH_WIDTH), lambda t: (jnp.maximum(t * per_tile - 1, 0), 0)),
                  pl.BlockSpec((POOL_HALO, BRANCH_WIDTH), lambda t: (jnp.minimum((t + 1) * per_tile, last), 0)),
                  _const_spec(band.shape), _const_spec(pool_w.shape), _const_spec((1, BRANCH_WIDTH)),
                  _const_spec(wb.shape), _const_spec(wo.shape)],
        out_specs=pl.BlockSpec((TM, d), row),
        out_shape=jax.ShapeDtypeStruct((n_tiles * TM, d), F32),
        compiler_params=_params(1),
        name="pool_merge",
    )(tok, mod, g0, g1, wg, ya_lat, ya_ctx, yc_lat, yc_ctx, u, u, u, band, pool_w, pool_scale, wb, wo)


def _ffn_kernel(x_ref, mod_ref, g2_ref, g3_ref, wi_ref, wo_ref, o_ref):
    x = x_ref[...]
    h = _modulate(x, g2_ref[...], mod_ref[0, 3:4, :], mod_ref[0, 4:5, :]).astype(BF16)
    step = 256
    acc = None
    for c in range(D_FF // step):
        gate = jnp.dot(h, wi_ref[:, c * step:(c + 1) * step], preferred_element_type=F32)
        up = jnp.dot(h, wi_ref[:, D_FF + c * step:D_FF + (c + 1) * step], preferred_element_type=F32)
        act = (gate * (1.0 / (1.0 + jnp.exp(-gate))) * up).astype(BF16)
        part = jnp.dot(act, wo_ref[c * step:(c + 1) * step, :], preferred_element_type=F32)
        acc = part if acc is None else acc + part
    o_ref[...] = x + mod_ref[0, 5:6, :] * _rms(acc, g3_ref[...])


def _ffn_call(tok, mod, g2, g3, wi, wo, nb, n):
    nt, d = tok.shape
    _, mod_map, _ = _tile_maps(nb, n, CTX_LEN)
    row = lambda t: (t, 0)
    return pl.pallas_call(
        _ffn_kernel,
        grid=(nt // TM,),
        in_specs=[pl.BlockSpec((TM, d), row),
                  pl.BlockSpec((1, 6, d), mod_map),
                  _const_spec((1, d)), _const_spec((1, d)),
                  _const_spec(wi.shape), _const_spec(wo.shape)],
        out_specs=pl.BlockSpec((TM, d), row),
        out_shape=jax.ShapeDtypeStruct((nt, d), F32),
        compiler_params=_params(1),
        name="swiglu_ffn",
    )(tok, mod, g2, g3, wi, wo)


def _column_order():
    bw = BRANCH_WIDTH
    order = np.arange(PROJ_WIDTH)
    half = DA_QK_DIM // 2
    for base in (0, bw):
        for h in range(DA_HEADS):
            for p in range(2):
                for s in range(2):
                    for i in range(half):
                        order[base + h * DA_V_DIM + p * 64 + s * half + i] = (
                            base + h * DA_V_DIM + s * DA_QK_DIM + 2 * i + p)
    half = WA_HEAD_DIM // 2
    base = 4 * bw
    for j in range(WA_HEADS // 2):
        for p in range(2):
            for g in range(2):
                for i in range(half):
                    order[base + j * LANES + p * 64 + g * half + i] = (
                        base + (j + (WA_HEADS // 2) * g) * WA_HEAD_DIM + 2 * i + p)
    base = 5 * bw
    for p in range(2):
        for kv in range(WA_KV_HEADS):
            for i in range(half):
                order[base + p * 64 + kv * half + i] = base + kv * WA_HEAD_DIM + 2 * i + p
    return order


def _wa_row_order():
    order = np.arange(BRANCH_WIDTH)
    for j in range(WA_HEADS // 2):
        for g in range(2):
            for dd in range(WA_HEAD_DIM):
                order[j * LANES + g * WA_HEAD_DIM + dd] = (j + (WA_HEADS // 2) * g) * WA_HEAD_DIM + dd
    return order


def _rope_tables(n):
    rows = n // GRID_W
    row = jnp.repeat(jnp.arange(rows, dtype=F32), GRID_W)
    col = jnp.tile(jnp.arange(GRID_W, dtype=F32), rows)
    n_freq = DA_QK_DIM // 4
    inv = ROPE_THETA ** (-jnp.arange(n_freq, dtype=F32) / n_freq)
    ang = jnp.concatenate([row[:, None] * inv, col[:, None] * inv], axis=-1)
    cos = jnp.tile(jnp.cos(ang), (1, 4))
    sin = jnp.tile(jnp.sin(ang), (1, 4)) * jnp.where(jnp.arange(LANES) < 64, -1.0, 1.0)
    cos = jnp.concatenate([cos, jnp.ones((TM, LANES), F32)], axis=0)
    sin = jnp.concatenate([sin, jnp.zeros((TM, LANES), F32)], axis=0)
    return cos, sin


def kernel(x, c, ctx, c_ctx, w_ada, b_ada, norm_g, w_in, da_lambda, da_subln_g,
           pool_w, pool_scale, wa_sink, w_branch, w_out, w_ffn_in, w_ffn_out):
    nb, n, d = x.shape
    depth = w_ada.shape[0]
    assert d == D_MODEL and ctx.shape[1] == CTX_LEN
    assert n % TM == 0 and (nb * CTX_LEN) % TM == 0 and n % CK == 0 and n % GRID_W == 0
    assert n % TQ == 0 and n % WA_BLOCK == 0
    assert TM % POOL_SUB == 0 and CTX_LEN % POOL_SUB == 0 and (nb * n) % CTX_LEN == 0
    assert WA_HEAD_DIM == DA_QK_DIM

    tok = jnp.concatenate([x.reshape(nb * n, d), ctx.reshape(nb * CTX_LEN, d)], axis=0)
    cvec = jnp.concatenate([c, c_ctx[None, :]], axis=0)
    mod = _mod_call(cvec, w_ada, b_ada).reshape(depth, nb + 1, 6, d)
    cos, sin = _rope_tables(n)
    band = jnp.asarray(_pool_band(), BF16)

    w1 = w_in[:, :, _column_order()].astype(BF16)
    wg = w_in[:, :, PROJ_WIDTH:].astype(BF16)
    wb = jnp.concatenate([w_branch[:, :2], w_branch[:, 2:, _wa_row_order()]], axis=1).astype(BF16)
    wo = w_out.astype(BF16)
    wfi = w_ffn_in.astype(BF16)
    wfo = w_ffn_out.astype(BF16)
    pw = pool_w.astype(BF16)

    n_tiles_all = tok.shape[0] // TM
    n_tiles_lat = nb * n // TM
    for l in range(depth):
        last = l == depth - 1
        lam_init = 0.8 - 0.6 * math.exp(-0.3 * l)
        g = norm_g[l].reshape(4, 1, d)
        qa, ka, va, u, qw, kw, vw = _proj_call(tok, mod[l], g[0], w1[l], cos, sin, nb, n)
        g_sub = da_subln_g[l].reshape(1, DA_V_DIM)
        ya = _da_call(qa, ka, va, da_lambda[l], g_sub, lam_init, nb, n)
        ya_ctx = ya if last else _da_ctx_call(qa, ka, va, da_lambda[l], g_sub, lam_init, nb, n)
        yc = _wa_call(qw, kw, vw, wa_sink[l], nb, n)
        yc_ctx = yc if last else _wa_ctx_call(qw, kw, vw, wa_sink[l], nb, n)
        n_tiles = n_tiles_lat if last else n_tiles_all
        tok = _merge_call(tok, mod[l], g[0], g[1], wg[l], ya, ya_ctx, yc, yc_ctx, u, band, pw[l],
                          pool_scale[l].reshape(1, BRANCH_WIDTH), wb[l], wo[l], nb, n, n_tiles)
        tok = _ffn_call(tok, mod[l], g[2], g[3], wfi[l], wfo[l], nb, n)
    return tok[:nb * n].reshape(nb, n, d)
```

```python
import functools
import math

import numpy as np
import jax
import jax.numpy as jnp
from jax import lax
from jax.experimental import pallas as pl
from jax.experimental.pallas import tpu as pltpu

F32 = jnp.float32
BF16 = jnp.bfloat16

D_MODEL = 1024
CTX_LEN = 256
GRID_W = 64
EPS = 1e-6
ROPE_THETA = 10000.0
BRANCH_WIDTH = D_MODEL // 2
N_BRANCH = 3
DA_HEADS = 4
DA_V_DIM = BRANCH_WIDTH // DA_HEADS
DA_QK_DIM = DA_V_DIM // 2
POOL_WINDOWS = (2, 4, 8, 16)
POOL_GROUPS = 4
POOL_GROUP = BRANCH_WIDTH // POOL_GROUPS
WA_HEAD_DIM = 64
WA_HEADS = BRANCH_WIDTH // WA_HEAD_DIM
WA_KV_HEADS = 2
WA_GROUP = WA_HEADS // WA_KV_HEADS
WA_WINDOW = 128
WA_BLOCK = 128
D_FF = ((8 * D_MODEL + 3 * 256 - 1) // (3 * 256)) * 256
PROJ_WIDTH = 5 * BRANCH_WIDTH + 2 * WA_KV_HEADS * WA_HEAD_DIM
GATE_WIDTH = N_BRANCH * D_MODEL

LANES = 128
BF16_SUBLANES = 16
VMEM_LIMIT = 56 * 1024 * 1024

TM = 512
TQ = 512
CK = 256
POOL_SUB = 128
POOL_HALO = BF16_SUBLANES


def _params(n_axes):
    return pltpu.CompilerParams(dimension_semantics=("arbitrary",) * n_axes,
                                vmem_limit_bytes=VMEM_LIMIT)


def _const_spec(shape):
    nd = len(shape)
    return pl.BlockSpec(shape, lambda *_: (0,) * nd, pipeline_mode=pl.Buffered(1))


def _rms(x, g):
    return x * lax.rsqrt(jnp.mean(x * x, axis=-1, keepdims=True) + EPS) * g


def _modulate(x, g, shift, scale):
    return _rms(x, g) * (1.0 + scale) + shift


def _mod_kernel(c_ref, w_ref, b_ref, o_ref):
    c = c_ref[...]
    a = c * (1.0 / (1.0 + jnp.exp(-c)))
    o_ref[0] = jnp.dot(a, w_ref[0], preferred_element_type=F32,
                       precision=lax.Precision.HIGHEST) + b_ref[0]


def _mod_call(cvec, w_ada, b_ada):
    depth, d, width = w_ada.shape
    rows = cvec.shape[0]
    tn = 1536
    return pl.pallas_call(
        _mod_kernel,
        grid=(depth, width // tn),
        in_specs=[pl.BlockSpec((rows, d), lambda l, j: (0, 0)),
                  pl.BlockSpec((1, d, tn), lambda l, j: (l, 0, j)),
                  pl.BlockSpec((1, 1, tn), lambda l, j: (l, 0, j))],
        out_specs=pl.BlockSpec((1, rows, tn), lambda l, j: (l, 0, j)),
        out_shape=jax.ShapeDtypeStruct((depth, rows, width), F32),
        compiler_params=_params(2),
        name="adaln_mod",
    )(cvec, w_ada, b_ada.reshape(depth, 1, width))


def _proj_kernel(x_ref, mod_ref, g_ref, w_ref, c_ref, s_ref,
                 qa_ref, ka_ref, va_ref, u_ref, qw_ref, kw_ref, vw_ref):
    h = _modulate(x_ref[...], g_ref[...], mod_ref[0, 0:1, :], mod_ref[0, 1:2, :]).astype(BF16)
    cos = c_ref[...]
    sin = s_ref[...]

    def proj(lo, width):
        return jnp.dot(h, w_ref[:, lo:lo + width], preferred_element_type=F32)

    def rope(yj):
        return yj * cos + pltpu.roll(yj, LANES // 2, 1) * sin

    def key_store(ref, y):
        for j in range(y.shape[1] // LANES):
            ref[:, j * LANES:(j + 1) * LANES] = rope(y[:, j * LANES:(j + 1) * LANES]).astype(BF16)

    lane = lax.broadcasted_iota(jnp.int32, (TM, LANES), 1)
    first = (lane & (DA_QK_DIM // 2)) == 0

    def query_store(ref, y, scale):
        for j in range(y.shape[1] // LANES):
            r = rope(y[:, j * LANES:(j + 1) * LANES]) * scale
            ref[:, (2 * j) * LANES:(2 * j + 1) * LANES] = jnp.where(first, r, 0.0).astype(BF16)
            ref[:, (2 * j + 1) * LANES:(2 * j + 2) * LANES] = jnp.where(first, 0.0, r).astype(BF16)

    bw = BRANCH_WIDTH
    ya, yk, yv, yu, yq = (proj(j * bw, bw) for j in range(5))
    kvw = proj(5 * bw, 2 * LANES)
    query_store(qa_ref, ya, DA_QK_DIM ** -0.5 * math.log2(math.e))
    key_store(ka_ref, yk)
    va_ref[...] = yv.astype(BF16)
    u_ref[...] = yu.astype(BF16)
    query_store(qw_ref, yq, WA_HEAD_DIM ** -0.5 * math.log2(math.e))
    key_store(kw_ref, kvw[:, :LANES])
    vw_ref[...] = kvw[:, LANES:].astype(BF16)


def _tile_maps(nb, n, ctx):
    n_lat = nb * n // TM
    per_seq = n // TM

    def mod_map(t):
        return (jnp.where(t < n_lat, t // per_seq, nb), 0, 0)

    def rope_map(t):
        return (jnp.where(t < n_lat, t % per_seq, per_seq), 0)

    return n_lat, mod_map, rope_map


def _proj_call(tok, mod, g, w1, cos, sin, nb, n):
    nt, d = tok.shape
    _, mod_map, rope_map = _tile_maps(nb, n, CTX_LEN)
    row = lambda t: (t, 0)
    spec = lambda w: pl.BlockSpec((TM, w), row)
    sd = lambda w: jax.ShapeDtypeStruct((nt, w), BF16)
    widths = (2 * BRANCH_WIDTH, BRANCH_WIDTH, BRANCH_WIDTH, BRANCH_WIDTH, 2 * BRANCH_WIDTH, LANES, LANES)
    return pl.pallas_call(
        _proj_kernel,
        grid=(nt // TM,),
        in_specs=[pl.BlockSpec((TM, d), row),
                  pl.BlockSpec((1, 6, d), mod_map),
                  _const_spec((1, d)),
                  _const_spec(w1.shape),
                  pl.BlockSpec((TM, LANES), rope_map),
                  pl.BlockSpec((TM, LANES), rope_map)],
        out_specs=[spec(w) for w in widths],
        out_shape=[sd(w) for w in widths],
        compiler_params=_params(1),
        name="in_proj",
    )(tok, mod, g, w1, cos, sin)


def _da_lambda(dl_ref, lam_init):
    lp = dl_ref[...]
    return (jnp.exp(jnp.sum(lp[0:1] * lp[1:2], axis=-1, keepdims=True))
            - jnp.exp(jnp.sum(lp[2:3] * lp[3:4], axis=-1, keepdims=True)) + lam_init)


def _da_scores(k, qs):
    return lax.dot_general(k, qs, (((1,), (1,)), ((), ())), preferred_element_type=F32)


def _da_fold(x, op):
    return op(x.reshape(x.shape[0] // 8, 8, x.shape[1]), axis=0)


def _da_pv(v, e):
    return lax.dot_general(v, e.astype(BF16), (((0,), (0,)), ((), ())), preferred_element_type=F32)


def _da_finish(o_ref, g_ref, lam, lam_init, l8, acc):
    tq = o_ref.shape[0]
    l = jnp.sum(l8, axis=0, keepdims=True)
    ot = acc[:, :tq] / l[:, :tq] - lam * (acc[:, tq:] / l[:, tq:])
    o_ref[...] = (_rms(ot.T, g_ref[...]) * (1.0 - lam_init)).astype(BF16)


def _da_kernel(dl_ref, g_ref, q_ref, kl_ref, kc_ref, vl_ref, vc_ref, o_ref, s_scr, m_scr, *, lam_init):
    t = pl.program_id(0)
    lam = _da_lambda(dl_ref, lam_init)
    n_lat = kl_ref.shape[0]
    qs = jnp.concatenate([q_ref[:, :LANES], q_ref[:, LANES:]], axis=0)

    @pl.when(t == 0)
    def _():
        s_scr[1] = jnp.zeros(s_scr.shape[1:], F32)
        m_scr[1] = jnp.zeros(m_scr.shape[1:], F32)

    def step(wr, rd):
        m_prev = m_scr[rd]
        m8 = l8 = acc = None
        for c in range(n_lat // CK):
            rows = slice(c * CK, (c + 1) * CK)
            s = _da_scores(kl_ref[rows, :], qs)
            s_scr[wr, rows, :] = s
            f = _da_fold(s, jnp.max)
            m8 = f if m8 is None else jnp.maximum(m8, f)
            e = jnp.exp2(s_scr[rd, rows, :] - m_prev)
            f = _da_fold(e, jnp.sum)
            p = _da_pv(vl_ref[rows, :], e)
            l8 = f if l8 is None else l8 + f
            acc = p if acc is None else acc + p
        rows = slice(n_lat, n_lat + CTX_LEN)
        sc = _da_scores(kc_ref[...], qs)
        s_scr[wr, rows, :] = sc
        m_scr[wr] = jnp.max(jnp.maximum(m8, _da_fold(sc, jnp.max)), axis=0, keepdims=True)
        e = jnp.exp2(s_scr[rd, rows, :] - m_prev)
        _da_finish(o_ref, g_ref, lam, lam_init, l8 + _da_fold(e, jnp.sum), acc + _da_pv(vc_ref[...], e))

    @pl.when(t % 2 == 0)
    def _():
        step(0, 1)

    @pl.when(t % 2 == 1)
    def _():
        step(1, 0)


def _da_call(qa, ka, va, da_lambda, g_sub, lam_init, nb, n):
    nt = qa.shape[0]
    n_q = n // TQ
    units = nb * DA_HEADS * n_q
    ctx0 = nb * n // CTX_LEN

    def unit(u):
        return u // (n_q * DA_HEADS), (u // n_q) % DA_HEADS, u % n_q

    def scored(t):
        return unit(jnp.minimum(t, units - 1))

    def finished(t):
        return unit(jnp.maximum(t - 1, 0))

    def tile_map(which):
        def m(t):
            b, h, i = which(t)
            return (b * n_q + i, h)
        return m

    def lat(which):
        return pl.BlockSpec((n, DA_V_DIM), lambda t: which(t)[:2])

    def ctx(which):
        def m(t):
            b, h, _ = which(t)
            return (ctx0 + b, h)
        return pl.BlockSpec((CTX_LEN, DA_V_DIM), m)

    kern = functools.partial(_da_kernel, lam_init=lam_init)
    return pl.pallas_call(
        kern,
        grid=(units + 1,),
        in_specs=[pl.BlockSpec(da_lambda.shape, lambda t: (0, 0)),
                  pl.BlockSpec((1, DA_V_DIM), lambda t: (0, 0)),
                  pl.BlockSpec((TQ, 2 * DA_V_DIM), tile_map(scored)),
                  lat(scored), ctx(scored), lat(finished), ctx(finished)],
        out_specs=pl.BlockSpec((TQ, DA_V_DIM), tile_map(finished)),
        out_shape=jax.ShapeDtypeStruct((nb * n, BRANCH_WIDTH), BF16),
        scratch_shapes=[pltpu.VMEM((2, n + CTX_LEN, 2 * TQ), F32), pltpu.VMEM((2, 1, 2 * TQ), F32)],
        compiler_params=_params(1),
        name="diff_attn",
    )(da_lambda, g_sub, qa, ka, ka, va, va)


def _da_ctx_kernel(dl_ref, g_ref, q_ref, k_ref, v_ref, o_ref, *, lam_init):
    qs = jnp.concatenate([q_ref[:, :LANES], q_ref[:, LANES:]], axis=0)
    s = _da_scores(k_ref[...], qs)
    e = jnp.exp2(s - jnp.max(s, axis=0, keepdims=True))
    _da_finish(o_ref, g_ref, _da_lambda(dl_ref, lam_init), lam_init, _da_fold(e, jnp.sum), _da_pv(v_ref[...], e))


def _da_ctx_call(qa, ka, va, da_lambda, g_sub, lam_init, nb, n):
    ctx0 = nb * n // CTX_LEN
    blk = lambda w: pl.BlockSpec((CTX_LEN, w), lambda b, h: (ctx0 + b, h))
    kern = functools.partial(_da_ctx_kernel, lam_init=lam_init)
    return pl.pallas_call(
        kern,
        grid=(nb, DA_HEADS),
        in_specs=[pl.BlockSpec(da_lambda.shape, lambda b, h: (0, 0)),
                  pl.BlockSpec((1, DA_V_DIM), lambda b, h: (0, 0)),
                  blk(2 * DA_V_DIM), blk(DA_V_DIM), blk(DA_V_DIM)],
        out_specs=pl.BlockSpec((CTX_LEN, DA_V_DIM), lambda b, h: (b, h)),
        out_shape=jax.ShapeDtypeStruct((nb * CTX_LEN, BRANCH_WIDTH), BF16),
        compiler_params=_params(2),
        name="diff_attn_ctx",
    )(da_lambda, g_sub, qa, ka, va)


def _wa_pair_queries(q_ref, j):
    return jnp.concatenate([q_ref[:, (2 * j) * LANES:(2 * j + 1) * LANES],
                            q_ref[:, (2 * j + 1) * LANES:(2 * j + 2) * LANES]], axis=0)


def _wa_pair_sinks(sink_ref, j):
    lane = lax.broadcasted_iota(jnp.int32, (1, 2 * WA_BLOCK), 1)
    return jnp.where(lane < WA_BLOCK, sink_ref[j], sink_ref[j + WA_HEADS // 2]) * math.log2(math.e)


def _wa_pair_max(s, sk):
    return jnp.maximum(jnp.max(_da_fold(s, jnp.max), axis=0, keepdims=True), sk)


def _wa_pair_finish(o_ref, j, s, m, sk, v):
    e = jnp.exp2(s - m)
    den = jnp.sum(_da_fold(e, jnp.sum), axis=0, keepdims=True) + jnp.exp2(sk - m)
    ot = _da_pv(v, e) / den
    first = lax.broadcasted_iota(jnp.int32, (LANES, WA_BLOCK), 0) < WA_HEAD_DIM
    o_ref[:, j * LANES:(j + 1) * LANES] = jnp.where(first, ot[:, :WA_BLOCK], ot[:, WA_BLOCK:]).T.astype(BF16)


def _wa_band_bias():
    band = 3 * WA_BLOCK
    key = np.arange(band + CTX_LEN)[:, None]
    qpos = np.arange(2 * WA_BLOCK)[None, :] % WA_BLOCK
    ok = (key >= band) | (np.abs(key - WA_BLOCK - qpos) <= WA_WINDOW)
    return np.where(ok, 0.0, -np.inf).astype(np.float32)


def _wa_kernel(sink_ref, bias_ref, q_ref, kp_ref, kc_ref, kn_ref, kx_ref, vp_ref, vc_ref, vn_ref, vx_ref,
               o_ref, s_scr, m_scr, *, n_blk, units):
    t = pl.program_id(0)
    i = lax.rem(jnp.minimum(t, units - 1), n_blk)

    @pl.when(t == 0)
    def _():
        s_scr[1] = jnp.zeros(s_scr.shape[1:], F32)
        m_scr[1] = jnp.zeros(m_scr.shape[1:], F32)

    def step(wr, rd):
        k = jnp.concatenate([kp_ref[...], kc_ref[...], kn_ref[...], kx_ref[...]], axis=0)
        v = jnp.concatenate([vp_ref[...], vc_ref[...], vn_ref[...], vx_ref[...]], axis=0)
        out = jnp.full((WA_BLOCK, 2 * WA_BLOCK), -jnp.inf, F32)
        bias = jnp.concatenate([jnp.where(i > 0, bias_ref[:WA_BLOCK, :], out),
                                bias_ref[WA_BLOCK:2 * WA_BLOCK, :],
                                jnp.where(i < n_blk - 1, bias_ref[2 * WA_BLOCK:3 * WA_BLOCK, :], out),
                                bias_ref[3 * WA_BLOCK:, :]], axis=0)
        for j in range(WA_HEADS // 2):
            cols = slice(j * 2 * WA_BLOCK, (j + 1) * 2 * WA_BLOCK)
            sk = _wa_pair_sinks(sink_ref, j)
            s = _da_scores(k, _wa_pair_queries(q_ref, j)) + bias
            s_scr[wr, :, cols] = s
            m_scr[wr, :, cols] = _wa_pair_max(s, sk)
            _wa_pair_finish(o_ref, j, s_scr[rd, :, cols], m_scr[rd, :, cols], sk, v)

    @pl.when(t % 2 == 0)
    def _():
        step(0, 1)

    @pl.when(t % 2 == 1)
    def _():
        step(1, 0)


def _wa_call(qw, kw, vw, sink, nb, n):
    n_blk = n // WA_BLOCK
    units = nb * n_blk
    ctx0 = nb * n // CTX_LEN

    def scored(t):
        u = jnp.minimum(t, units - 1)
        return u // n_blk, u % n_blk

    def finished(t):
        u = jnp.maximum(t - 1, 0)
        return u // n_blk, u % n_blk

    def q_map(which):
        def m(t):
            b, i = which(t)
            return (b * n_blk + i, 0)
        return m

    def near(which, delta):
        def m(t):
            b, i = which(t)
            return (b * n_blk + jnp.clip(i + delta, 0, n_blk - 1), 0)
        return pl.BlockSpec((WA_BLOCK, LANES), m)

    def ctx(which):
        return pl.BlockSpec((CTX_LEN, LANES), lambda t: (ctx0 + which(t)[0], 0))

    bias = jnp.asarray(_wa_band_bias())
    keys = bias.shape[0]
    kern = functools.partial(_wa_kernel, n_blk=n_blk, units=units)
    return pl.pallas_call(
        kern,
        grid=(units + 1,),
        in_specs=[pl.BlockSpec(memory_space=pltpu.SMEM),
                  _const_spec(bias.shape),
                  pl.BlockSpec((WA_BLOCK, 2 * BRANCH_WIDTH), q_map(scored)),
                  near(scored, -1), near(scored, 0), near(scored, 1), ctx(scored),
                  near(finished, -1), near(finished, 0), near(finished, 1), ctx(finished)],
        out_specs=pl.BlockSpec((WA_BLOCK, BRANCH_WIDTH), q_map(finished)),
        out_shape=jax.ShapeDtypeStruct((nb * n, BRANCH_WIDTH), BF16),
        scratch_shapes=[pltpu.VMEM((2, keys, WA_HEADS * WA_BLOCK), F32),
                        pltpu.VMEM((2, 1, WA_HEADS * WA_BLOCK), F32)],
        compiler_params=_params(1),
        name="window_attn",
    )(sink, bias, qw, kw, kw, kw, kw, vw, vw, vw, vw)


def _wa_ctx_kernel(sink_ref, q_ref, k_ref, v_ref, o_ref):
    k = k_ref[...]
    s_next = _da_scores(k, _wa_pair_queries(q_ref, 0))
    for j in range(WA_HEADS // 2):
        s = s_next
        if j + 1 < WA_HEADS // 2:
            s_next = _da_scores(k, _wa_pair_queries(q_ref, j + 1))
        sk = _wa_pair_sinks(sink_ref, j)
        _wa_pair_finish(o_ref, j, s, _wa_pair_max(s, sk), sk, v_ref[...])


def _wa_ctx_call(qw, kw, vw, sink, nb, n):
    ctx_blk = CTX_LEN // WA_BLOCK
    ctx0_q = nb * n // WA_BLOCK
    ctx0_k = nb * n // CTX_LEN
    keys = pl.BlockSpec((CTX_LEN, LANES), lambda b, i: (ctx0_k + b, 0))
    return pl.pallas_call(
        _wa_ctx_kernel,
        grid=(nb, ctx_blk),
        in_specs=[pl.BlockSpec(memory_space=pltpu.SMEM),
                  pl.BlockSpec((WA_BLOCK, 2 * BRANCH_WIDTH), lambda b, i: (ctx0_q + b * ctx_blk + i, 0)),
                  keys, keys],
        out_specs=pl.BlockSpec((WA_BLOCK, BRANCH_WIDTH), lambda b, i: (b * ctx_blk + i, 0)),
        out_shape=jax.ShapeDtypeStruct((nb * CTX_LEN, BRANCH_WIDTH), BF16),
        compiler_params=_params(2),
        name="window_attn_ctx",
    )(sink, qw, kw, vw)


def _pool_band():
    ext = POOL_SUB + 2 * POOL_HALO
    t = np.arange(POOL_SUB)[:, None]
    d = np.arange(ext)[None, :] - POOL_HALO - t
    return np.stack([((d >= -(w // 2)) & (d < w - w // 2)) for w in POOL_WINDOWS]).astype(np.float32)


def _merge_kernel(x_ref, mod_ref, g0_ref, g1_ref, wg_ref, yal_ref, yax_ref, ycl_ref, ycx_ref,
                  u_ref, up_ref, un_ref, band_ref, pw_ref, ps_ref, wb_ref, wo_ref, o_ref, *, n_lat, n):
    t = pl.program_id(0)
    x = x_ref[...]
    h = _modulate(x, g0_ref[...], mod_ref[0, 0:1, :], mod_ref[0, 1:2, :]).astype(BF16)
    is_lat = t < n_lat
    ya = jnp.where(is_lat, yal_ref[...], yax_ref[...])
    yc = jnp.where(is_lat, ycl_ref[...], ycx_ref[...])

    u = u_ref[...]
    halo_one = jnp.ones((POOL_HALO, POOL_GROUP), BF16)
    halo_zero = jnp.zeros((POOL_HALO, POOL_GROUP), BF16)
    n_sub = TM // POOL_SUB
    units = [(sb, g) for sb in range(n_sub) for g in range(POOL_GROUPS)]
    ext, ones = [], []
    for sb in range(n_sub):
        r0 = sb * POOL_SUB
        prev = up_ref[...] if sb == 0 else u[r0 - POOL_HALO:r0]
        nxt = un_ref[...] if r0 + POOL_SUB == TM else u[r0 + POOL_SUB:r0 + POOL_SUB + POOL_HALO]
        g_row = t * TM + r0
        seq_row = jnp.where(is_lat, g_row % n, g_row % CTX_LEN)
        seq_len = jnp.where(is_lat, n, CTX_LEN)
        prev_ok = seq_row != 0
        next_ok = seq_row + POOL_SUB != seq_len
        ext.append(jnp.concatenate([jnp.where(prev_ok, prev, jnp.zeros_like(prev)), u[r0:r0 + POOL_SUB],
                                    jnp.where(next_ok, nxt, jnp.zeros_like(nxt))], axis=0))
        ones.append(jnp.concatenate([jnp.where(prev_ok, halo_one, halo_zero),
                                     jnp.ones((POOL_SUB, POOL_GROUP), BF16),
                                     jnp.where(next_ok, halo_one, halo_zero)], axis=0))
    cols = lambda g: slice(g * POOL_GROUP, (g + 1) * POOL_GROUP)
    both = [jnp.dot(band_ref[g], jnp.concatenate([ext[sb][:, cols(g)], ones[sb]], axis=1),
                    preferred_element_type=F32) for sb, g in units]
    pooled = [(bt[:, :POOL_GROUP] / bt[:, POOL_GROUP:]
               - u[sb * POOL_SUB:(sb + 1) * POOL_SUB, cols(g)].astype(F32)).astype(BF16)
              for (sb, g), bt in zip(units, both)]
    outs = [jnp.dot(pq, pw_ref[g], preferred_element_type=F32) * ps_ref[:, cols(g)]
            for (sb, g), pq in zip(units, pooled)]
    yb = jnp.concatenate([jnp.concatenate(outs[sb * POOL_GROUPS:(sb + 1) * POOL_GROUPS], axis=1)
                          for sb in range(n_sub)], axis=0).astype(BF16)

    branches = (ya, yb, yc)
    m = None
    for k in range(N_BRANCH):
        gate = jnp.dot(h, wg_ref[:, k * D_MODEL:(k + 1) * D_MODEL], preferred_element_type=F32)
        gate = 1.0 / (1.0 + jnp.exp(-gate))
        term = gate * jnp.dot(branches[k], wb_ref[k], preferred_element_type=F32)
        m = term if m is None else m + term
    mix = jnp.dot(m.astype(BF16), wo_ref[...], preferred_element_type=F32)
    o_ref[...] = x + mod_ref[0, 2:3, :] * _rms(mix, g1_ref[...])


def _merge_call(tok, mod, g0, g1, wg, ya_lat, ya_ctx, yc_lat, yc_ctx, u, band, pool_w, pool_scale, wb, wo,
                nb, n, n_tiles):
    nt, d = tok.shape
    n_lat, mod_map, _ = _tile_maps(nb, n, CTX_LEN)
    row = lambda t: (t, 0)
    per_tile = TM // POOL_HALO
    last = nt // POOL_HALO - 1
    wide = pl.BlockSpec((TM, BRANCH_WIDTH), row)
    n_ctx = ya_ctx.shape[0] // TM
    lat_rows = pl.BlockSpec((TM, BRANCH_WIDTH), lambda t: (jnp.minimum(t, n_lat - 1), 0))
    ctx_rows = pl.BlockSpec((TM, BRANCH_WIDTH), lambda t: (jnp.clip(t - n_lat, 0, n_ctx - 1), 0))
    kern = functools.partial(_merge_kernel, n_lat=n_lat, n=n)
    return pl.pallas_call(
        kern,
        grid=(n_tiles,),
        in_specs=[pl.BlockSpec((TM, d), row),
                  pl.BlockSpec((1, 6, d), mod_map),
                  _const_spec((1, d)), _const_spec((1, d)),
                  _const_spec(wg.shape),
                  lat_rows, ctx_rows, lat_rows, ctx_rows, wide,
                  pl.BlockSpec((POOL_HALO, BRANCH_WIDTH), lambda t: (jnp.maximum(t * per_tile - 1, 0), 0)),
                  pl.BlockSpec((POOL_HALO, BRANCH_WIDTH), lambda t: (jnp.minimum((t + 1) * per_tile, last), 0)),
                  _const_spec(band.shape), _const_spec(pool_w.shape), _const_spec((1, BRANCH_WIDTH)),
                  _const_spec(wb.shape), _const_spec(wo.shape)],
        out_specs=pl.BlockSpec((TM, d), row),
        out_shape=jax.ShapeDtypeStruct((n_tiles * TM, d), F32),
        compiler_params=_params(1),
        name="pool_merge",
    )(tok, mod, g0, g1, wg, ya_lat, ya_ctx, yc_lat, yc_ctx, u, u, u, band, pool_w, pool_scale, wb, wo)


def _ffn_kernel(x_ref, mod_ref, g2_ref, g3_ref, wi_ref, wo_ref, o_ref):
    x = x_ref[...]
    h = _modulate(x, g2_ref[...], mod_ref[0, 3:4, :], mod_ref[0, 4:5, :]).astype(BF16)
    step = 256

    def gate_up(c):
        return (jnp.dot(h, wi_ref[:, c * step:(c + 1) * step], preferred_element_type=F32),
                jnp.dot(h, wi_ref[:, D_FF + c * step:D_FF + (c + 1) * step], preferred_element_type=F32))

    acc = None
    nxt = gate_up(0)
    for c in range(D_FF // step):
        gate, up = nxt
        if c + 1 < D_FF // step:
            nxt = gate_up(c + 1)
        act = (gate * (1.0 / (1.0 + jnp.exp(-gate))) * up).astype(BF16)
        part = jnp.dot(act, wo_ref[c * step:(c + 1) * step, :], preferred_element_type=F32)
        acc = part if acc is None else acc + part
    o_ref[...] = x + mod_ref[0, 5:6, :] * _rms(acc, g3_ref[...])


def _ffn_call(tok, mod, g2, g3, wi, wo, nb, n):
    nt, d = tok.shape
    _, mod_map, _ = _tile_maps(nb, n, CTX_LEN)
    row = lambda t: (t, 0)
    return pl.pallas_call(
        _ffn_kernel,
        grid=(nt // TM,),
        in_specs=[pl.BlockSpec((TM, d), row),
                  pl.BlockSpec((1, 6, d), mod_map),
                  _const_spec((1, d)), _const_spec((1, d)),
                  _const_spec(wi.shape), _const_spec(wo.shape)],
        out_specs=pl.BlockSpec((TM, d), row),
        out_shape=jax.ShapeDtypeStruct((nt, d), F32),
        compiler_params=_params(1),
        name="swiglu_ffn",
    )(tok, mod, g2, g3, wi, wo)


def _column_order():
    bw = BRANCH_WIDTH
    order = np.arange(PROJ_WIDTH)
    half = DA_QK_DIM // 2
    for base in (0, bw):
        for h in range(DA_HEADS):
            for p in range(2):
                for s in range(2):
                    for i in range(half):
                        order[base + h * DA_V_DIM + p * 64 + s * half + i] = (
                            base + h * DA_V_DIM + s * DA_QK_DIM + 2 * i + p)
    half = WA_HEAD_DIM // 2
    base = 4 * bw
    for j in range(WA_HEADS // 2):
        for p in range(2):
            for g in range(2):
                for i in range(half):
                    order[base + j * LANES + p * 64 + g * half + i] = (
                        base + (j + (WA_HEADS // 2) * g) * WA_HEAD_DIM + 2 * i + p)
    base = 5 * bw
    for p in range(2):
        for kv in range(WA_KV_HEADS):
            for i in range(half):
                order[base + p * 64 + kv * half + i] = base + kv * WA_HEAD_DIM + 2 * i + p
    return order


def _wa_row_order():
    order = np.arange(BRANCH_WIDTH)
    for j in range(WA_HEADS // 2):
        for g in range(2):
            for dd in range(WA_HEAD_DIM):
                order[j * LANES + g * WA_HEAD_DIM + dd] = (j + (WA_HEADS // 2) * g) * WA_HEAD_DIM + dd
    return order


def _rope_tables(n):
    rows = n // GRID_W
    row = jnp.repeat(jnp.arange(rows, dtype=F32), GRID_W)
    col = jnp.tile(jnp.arange(GRID_W, dtype=F32), rows)
    n_freq = DA_QK_DIM // 4
    inv = ROPE_THETA ** (-jnp.arange(n_freq, dtype=F32) / n_freq)
    ang = jnp.concatenate([row[:, None] * inv, col[:, None] * inv], axis=-1)
    cos = jnp.tile(jnp.cos(ang), (1, 4))
    sin = jnp.tile(jnp.sin(ang), (1, 4)) * jnp.where(jnp.arange(LANES) < 64, -1.0, 1.0)
    cos = jnp.concatenate([cos, jnp.ones((TM, LANES), F32)], axis=0)
    sin = jnp.concatenate([sin, jnp.zeros((TM, LANES), F32)], axis=0)
    return cos, sin


def kernel(x, c, ctx, c_ctx, w_ada, b_ada, norm_g, w_in, da_lambda, da_subln_g,
           pool_w, pool_scale, wa_sink, w_branch, w_out, w_ffn_in, w_ffn_out):
    nb, n, d = x.shape
    depth = w_ada.shape[0]
    assert d == D_MODEL and ctx.shape[1] == CTX_LEN
    assert n % TM == 0 and (nb * CTX_LEN) % TM == 0 and n % CK == 0 and n % GRID_W == 0
    assert n % TQ == 0 and n % WA_BLOCK == 0
    assert TM % POOL_SUB == 0 and CTX_LEN % POOL_SUB == 0 and (nb * n) % CTX_LEN == 0
    assert WA_HEAD_DIM == DA_QK_DIM

    tok = jnp.concatenate([x.reshape(nb * n, d), ctx.reshape(nb * CTX_LEN, d)], axis=0)
    cvec = jnp.concatenate([c, c_ctx[None, :]], axis=0)
    mod = _mod_call(cvec, w_ada, b_ada).reshape(depth, nb + 1, 6, d)
    cos, sin = _rope_tables(n)
    band = jnp.asarray(_pool_band(), BF16)

    w1 = w_in[:, :, _column_order()].astype(BF16)
    wg = w_in[:, :, PROJ_WIDTH:].astype(BF16)
    wb = jnp.concatenate([w_branch[:, :2], w_branch[:, 2:, _wa_row_order()]], axis=1).astype(BF16)
    wo = w_out.astype(BF16)
    wfi = w_ffn_in.astype(BF16)
    wfo = w_ffn_out.astype(BF16)
    pw = pool_w.astype(BF16)

    n_tiles_all = tok.shape[0] // TM
    n_tiles_lat = nb * n // TM
    for l in range(depth):
        last = l == depth - 1
        lam_init = 0.8 - 0.6 * math.exp(-0.3 * l)
        g = norm_g[l].reshape(4, 1, d)
        qa, ka, va, u, qw, kw, vw = _proj_call(tok, mod[l], g[0], w1[l], cos, sin, nb, n)
        g_sub = da_subln_g[l].reshape(1, DA_V_DIM)
        ya = _da_call(qa, ka, va, da_lambda[l], g_sub, lam_init, nb, n)
        ya_ctx = ya if last else _da_ctx_call(qa, ka, va, da_lambda[l], g_sub, lam_init, nb, n)
        yc = _wa_call(qw, kw, vw, wa_sink[l], nb, n)
        yc_ctx = yc if last else _wa_ctx_call(qw, kw, vw, wa_sink[l], nb, n)
        n_tiles = n_tiles_lat if last else n_tiles_all
        tok = _merge_call(tok, mod[l], g[0], g[1], wg[l], ya, ya_ctx, yc, yc_ctx, u, band, pw[l],
                          pool_scale[l].reshape(1, BRANCH_WIDTH), wb[l], wo[l], nb, n, n_tiles)
        tok = _ffn_call(tok, mod[l], g[2], g[3], wfi[l], wfo[l], nb, n)
    return tok[:nb * n].reshape(nb, n, d)
```

```python
import functools
import math

import numpy as np
import jax
import jax.numpy as jnp
from jax import lax
from jax.experimental import pallas as pl
from jax.experimental.pallas import tpu as pltpu

F32 = jnp.float32
BF16 = jnp.bfloat16

D_MODEL = 1024
CTX_LEN = 256
GRID_W = 64
EPS = 1e-6
ROPE_THETA = 10000.0
BRANCH_WIDTH = D_MODEL // 2
N_BRANCH = 3
DA_HEADS = 4
DA_V_DIM = BRANCH_WIDTH // DA_HEADS
DA_QK_DIM = DA_V_DIM // 2
POOL_WINDOWS = (2, 4, 8, 16)
POOL_GROUPS = 4
POOL_GROUP = BRANCH_WIDTH // POOL_GROUPS
WA_HEAD_DIM = 64
WA_HEADS = BRANCH_WIDTH // WA_HEAD_DIM
WA_KV_HEADS = 2
WA_GROUP = WA_HEADS // WA_KV_HEADS
WA_WINDOW = 128
WA_BLOCK = 128
D_FF = ((8 * D_MODEL + 3 * 256 - 1) // (3 * 256)) * 256
PROJ_WIDTH = 5 * BRANCH_WIDTH + 2 * WA_KV_HEADS * WA_HEAD_DIM
GATE_WIDTH = N_BRANCH * D_MODEL

LANES = 128
BF16_SUBLANES = 16
VMEM_LIMIT = 56 * 1024 * 1024

TM = 512
TQ = 512
CK = 256
WA_STEP = 4
POOL_SUB = 128
POOL_HALO = BF16_SUBLANES


def _params(n_axes):
    return pltpu.CompilerParams(dimension_semantics=("arbitrary",) * n_axes,
                                vmem_limit_bytes=VMEM_LIMIT)


def _const_spec(shape):
    nd = len(shape)
    return pl.BlockSpec(shape, lambda *_: (0,) * nd, pipeline_mode=pl.Buffered(1))


def _rms(x, g):
    return x * lax.rsqrt(jnp.mean(x * x, axis=-1, keepdims=True) + EPS) * g


def _modulate(x, g, shift, scale):
    return _rms(x, g) * (1.0 + scale) + shift


def _mod_kernel(c_ref, w_ref, b_ref, o_ref):
    c = c_ref[...]
    a = c * (1.0 / (1.0 + jnp.exp(-c)))
    o_ref[0] = jnp.dot(a, w_ref[0], preferred_element_type=F32,
                       precision=lax.Precision.HIGHEST) + b_ref[0]


def _mod_call(cvec, w_ada, b_ada):
    depth, d, width = w_ada.shape
    rows = cvec.shape[0]
    tn = 1536
    return pl.pallas_call(
        _mod_kernel,
        grid=(depth, width // tn),
        in_specs=[pl.BlockSpec((rows, d), lambda l, j: (0, 0)),
                  pl.BlockSpec((1, d, tn), lambda l, j: (l, 0, j)),
                  pl.BlockSpec((1, 1, tn), lambda l, j: (l, 0, j))],
        out_specs=pl.BlockSpec((1, rows, tn), lambda l, j: (l, 0, j)),
        out_shape=jax.ShapeDtypeStruct((depth, rows, width), F32),
        compiler_params=_params(2),
        name="adaln_mod",
    )(cvec, w_ada, b_ada.reshape(depth, 1, width))


def _proj_kernel(x_ref, mod_ref, g_ref, w_ref, c_ref, s_ref,
                 qa_ref, ka_ref, va_ref, u_ref, qw_ref, kw_ref, vw_ref):
    h = _modulate(x_ref[...], g_ref[...], mod_ref[0, 0:1, :], mod_ref[0, 1:2, :]).astype(BF16)
    cos = c_ref[...]
    sin = s_ref[...]

    def proj(lo, width):
        return jnp.dot(h, w_ref[:, lo:lo + width], preferred_element_type=F32)

    def rope(yj):
        return yj * cos + pltpu.roll(yj, LANES // 2, 1) * sin

    def key_store(ref, y):
        for j in range(y.shape[1] // LANES):
            ref[:, j * LANES:(j + 1) * LANES] = rope(y[:, j * LANES:(j + 1) * LANES]).astype(BF16)

    lane = lax.broadcasted_iota(jnp.int32, (TM, LANES), 1)
    first = (lane & (DA_QK_DIM // 2)) == 0

    def query_store(ref, y, scale):
        for j in range(y.shape[1] // LANES):
            r = rope(y[:, j * LANES:(j + 1) * LANES]) * scale
            ref[:, (2 * j) * LANES:(2 * j + 1) * LANES] = jnp.where(first, r, 0.0).astype(BF16)
            ref[:, (2 * j + 1) * LANES:(2 * j + 2) * LANES] = jnp.where(first, 0.0, r).astype(BF16)

    bw = BRANCH_WIDTH
    ya, yk, yv, yu, yq = (proj(j * bw, bw) for j in range(5))
    kvw = proj(5 * bw, 2 * LANES)
    query_store(qa_ref, ya, DA_QK_DIM ** -0.5 * math.log2(math.e))
    key_store(ka_ref, yk)
    va_ref[...] = yv.astype(BF16)
    u_ref[...] = yu.astype(BF16)
    query_store(qw_ref, yq, WA_HEAD_DIM ** -0.5 * math.log2(math.e))
    key_store(kw_ref, kvw[:, :LANES])
    vw_ref[...] = kvw[:, LANES:].astype(BF16)


def _tile_maps(nb, n, ctx):
    n_lat = nb * n // TM
    per_seq = n // TM

    def mod_map(t):
        return (jnp.where(t < n_lat, t // per_seq, nb), 0, 0)

    def rope_map(t):
        return (jnp.where(t < n_lat, t % per_seq, per_seq), 0)

    return n_lat, mod_map, rope_map


def _proj_call(tok, mod, g, w1, cos, sin, nb, n):
    nt, d = tok.shape
    _, mod_map, rope_map = _tile_maps(nb, n, CTX_LEN)
    row = lambda t: (t, 0)
    spec = lambda w: pl.BlockSpec((TM, w), row)
    sd = lambda w: jax.ShapeDtypeStruct((nt, w), BF16)
    widths = (2 * BRANCH_WIDTH, BRANCH_WIDTH, BRANCH_WIDTH, BRANCH_WIDTH, 2 * BRANCH_WIDTH, LANES, LANES)
    return pl.pallas_call(
        _proj_kernel,
        grid=(nt // TM,),
        in_specs=[pl.BlockSpec((TM, d), row),
                  pl.BlockSpec((1, 6, d), mod_map),
                  _const_spec((1, d)),
                  _const_spec(w1.shape),
                  pl.BlockSpec((TM, LANES), rope_map),
                  pl.BlockSpec((TM, LANES), rope_map)],
        out_specs=[spec(w) for w in widths],
        out_shape=[sd(w) for w in widths],
        compiler_params=_params(1),
        name="in_proj",
    )(tok, mod, g, w1, cos, sin)


def _da_lambda(dl_ref, lam_init):
    lp = dl_ref[...]
    return (jnp.exp(jnp.sum(lp[0:1] * lp[1:2], axis=-1, keepdims=True))
            - jnp.exp(jnp.sum(lp[2:3] * lp[3:4], axis=-1, keepdims=True)) + lam_init)


def _da_scores(k, qs):
    return lax.dot_general(k, qs, (((1,), (1,)), ((), ())), preferred_element_type=F32)


def _da_fold(x, op):
    return op(x.reshape(x.shape[0] // 8, 8, x.shape[1]), axis=0)


def _da_pv(v, e):
    return lax.dot_general(v, e.astype(BF16), (((0,), (0,)), ((), ())), preferred_element_type=F32)


def _da_finish(o_ref, g_ref, lam, lam_init, l8, acc):
    tq = o_ref.shape[0]
    l = jnp.sum(l8, axis=0, keepdims=True)
    ot = acc[:, :tq] / l[:, :tq] - lam * (acc[:, tq:] / l[:, tq:])
    o_ref[...] = (_rms(ot.T, g_ref[...]) * (1.0 - lam_init)).astype(BF16)


def _da_kernel(dl_ref, g_ref, q_ref, kl_ref, kc_ref, vl_ref, vc_ref, o_ref, s_scr, m_scr, l_scr, acc_scr, *,
               lam_init):
    t = pl.program_id(0)
    lam = _da_lambda(dl_ref, lam_init)
    n_lat = kl_ref.shape[0]
    qs = jnp.concatenate([q_ref[:, :LANES], q_ref[:, LANES:]], axis=0)

    @pl.when(t == 0)
    def _():
        s_scr[1] = jnp.zeros(s_scr.shape[1:], F32)
        m_scr[1] = jnp.zeros(m_scr.shape[1:], F32)
        l_scr[...] = jnp.ones(l_scr.shape, F32)
        acc_scr[...] = jnp.zeros(acc_scr.shape, F32)

    def step(wr, rd):
        _da_finish(o_ref, g_ref, lam, lam_init, l_scr[...], acc_scr[...])
        m_prev = m_scr[rd]
        m8 = l8 = acc = None
        for c in range(n_lat // CK):
            rows = slice(c * CK, (c + 1) * CK)
            s = _da_scores(kl_ref[rows, :], qs)
            s_scr[wr, rows, :] = s
            f = _da_fold(s, jnp.max)
            m8 = f if m8 is None else jnp.maximum(m8, f)
            e = jnp.exp2(s_scr[rd, rows, :] - m_prev)
            f = _da_fold(e, jnp.sum)
            p = _da_pv(vl_ref[rows, :], e)
            l8 = f if l8 is None else l8 + f
            acc = p if acc is None else acc + p
        rows = slice(n_lat, n_lat + CTX_LEN)
        sc = _da_scores(kc_ref[...], qs)
        s_scr[wr, rows, :] = sc
        m_scr[wr] = jnp.max(jnp.maximum(m8, _da_fold(sc, jnp.max)), axis=0, keepdims=True)
        e = jnp.exp2(s_scr[rd, rows, :] - m_prev)
        l_scr[...] = l8 + _da_fold(e, jnp.sum)
        acc_scr[...] = acc + _da_pv(vc_ref[...], e)

    @pl.when(t % 2 == 0)
    def _():
        step(0, 1)

    @pl.when(t % 2 == 1)
    def _():
        step(1, 0)


def _da_call(qa, ka, va, da_lambda, g_sub, lam_init, nb, n):
    n_q = n // TQ
    units = nb * DA_HEADS * n_q
    ctx0 = nb * n // CTX_LEN

    def unit(u):
        u = jnp.clip(u, 0, units - 1)
        return u // (n_q * DA_HEADS), (u // n_q) % DA_HEADS, u % n_q

    def scored(t):
        return unit(t)

    def finished(t):
        return unit(t - 1)

    def written(t):
        return unit(t - 2)

    def tile_map(which):
        def m(t):
            b, h, i = which(t)
            return (b * n_q + i, h)
        return m

    def lat(which):
        return pl.BlockSpec((n, DA_V_DIM), lambda t: which(t)[:2])

    def ctx(which):
        def m(t):
            b, h, _ = which(t)
            return (ctx0 + b, h)
        return pl.BlockSpec((CTX_LEN, DA_V_DIM), m)

    kern = functools.partial(_da_kernel, lam_init=lam_init)
    return pl.pallas_call(
        kern,
        grid=(units + 2,),
        in_specs=[pl.BlockSpec(da_lambda.shape, lambda t: (0, 0)),
                  pl.BlockSpec((1, DA_V_DIM), lambda t: (0, 0)),
                  pl.BlockSpec((TQ, 2 * DA_V_DIM), tile_map(scored)),
                  lat(scored), ctx(scored), lat(finished), ctx(finished)],
        out_specs=pl.BlockSpec((TQ, DA_V_DIM), tile_map(written)),
        out_shape=jax.ShapeDtypeStruct((nb * n, BRANCH_WIDTH), BF16),
        scratch_shapes=[pltpu.VMEM((2, n + CTX_LEN, 2 * TQ), F32), pltpu.VMEM((2, 1, 2 * TQ), F32),
                        pltpu.VMEM((8, 2 * TQ), F32), pltpu.VMEM((DA_V_DIM, 2 * TQ), F32)],
        compiler_params=_params(1),
        name="diff_attn",
    )(da_lambda, g_sub, qa, ka, ka, va, va)


def _da_ctx_kernel(dl_ref, g_ref, q_ref, k_ref, v_ref, o_ref, *, lam_init):
    qs = jnp.concatenate([q_ref[:, :LANES], q_ref[:, LANES:]], axis=0)
    s = _da_scores(k_ref[...], qs)
    e = jnp.exp2(s - jnp.max(s, axis=0, keepdims=True))
    _da_finish(o_ref, g_ref, _da_lambda(dl_ref, lam_init), lam_init, _da_fold(e, jnp.sum), _da_pv(v_ref[...], e))


def _da_ctx_call(qa, ka, va, da_lambda, g_sub, lam_init, nb, n):
    ctx0 = nb * n // CTX_LEN
    blk = lambda w: pl.BlockSpec((CTX_LEN, w), lambda b, h: (ctx0 + b, h))
    kern = functools.partial(_da_ctx_kernel, lam_init=lam_init)
    return pl.pallas_call(
        kern,
        grid=(nb, DA_HEADS),
        in_specs=[pl.BlockSpec(da_lambda.shape, lambda b, h: (0, 0)),
                  pl.BlockSpec((1, DA_V_DIM), lambda b, h: (0, 0)),
                  blk(2 * DA_V_DIM), blk(DA_V_DIM), blk(DA_V_DIM)],
        out_specs=pl.BlockSpec((CTX_LEN, DA_V_DIM), lambda b, h: (b, h)),
        out_shape=jax.ShapeDtypeStruct((nb * CTX_LEN, BRANCH_WIDTH), BF16),
        compiler_params=_params(2),
        name="diff_attn_ctx",
    )(da_lambda, g_sub, qa, ka, va)


def _wa_pair_queries(q_ref, j, rows=slice(0, WA_BLOCK)):
    return jnp.concatenate([q_ref[rows, (2 * j) * LANES:(2 * j + 1) * LANES],
                            q_ref[rows, (2 * j + 1) * LANES:(2 * j + 2) * LANES]], axis=0)


def _wa_pair_sinks(sink_ref, j):
    lane = lax.broadcasted_iota(jnp.int32, (1, 2 * WA_BLOCK), 1)
    return jnp.where(lane < WA_BLOCK, sink_ref[j], sink_ref[j + WA_HEADS // 2]) * math.log2(math.e)


def _wa_pair_max(s, sk):
    return jnp.maximum(jnp.max(_da_fold(s, jnp.max), axis=0, keepdims=True), sk)


def _wa_pair_finish(o_ref, j, s, m, sk, v, rows=slice(0, WA_BLOCK)):
    e = jnp.exp2(s - m)
    den = jnp.sum(_da_fold(e, jnp.sum), axis=0, keepdims=True) + jnp.exp2(sk - m)
    ot = _da_pv(v, e) / den
    first = lax.broadcasted_iota(jnp.int32, (LANES, WA_BLOCK), 0) < WA_HEAD_DIM
    o_ref[rows, j * LANES:(j + 1) * LANES] = jnp.where(first, ot[:, :WA_BLOCK], ot[:, WA_BLOCK:]).T.astype(BF16)


def _wa_band_bias():
    band = 3 * WA_BLOCK
    key = np.arange(band + CTX_LEN)[:, None]
    qpos = np.arange(2 * WA_BLOCK)[None, :] % WA_BLOCK
    ok = (key >= band) | (np.abs(key - WA_BLOCK - qpos) <= WA_WINDOW)
    return np.where(ok, 0.0, -np.inf).astype(np.float32)


def _wa_kernel(sink_ref, bias_ref, q_ref, kp_ref, kc_ref, kn_ref, kx_ref, vp_ref, vc_ref, vn_ref, vx_ref,
               o_ref, s_scr, m_scr, *, n_steps, units):
    t = pl.program_id(0)
    i = lax.rem(jnp.minimum(t, units - 1), n_steps)

    @pl.when(t == 0)
    def _():
        s_scr[1] = jnp.zeros(s_scr.shape[1:], F32)
        m_scr[1] = jnp.zeros(m_scr.shape[1:], F32)

    def step(wr, rd):
        def gather(p_ref, c_ref, n_ref, x_ref):
            blocks = [p_ref[...]] + [c_ref[b * WA_BLOCK:(b + 1) * WA_BLOCK, :] for b in range(WA_STEP)] + [n_ref[...]]
            return [jnp.concatenate(blocks[b:b + 3] + [x_ref[...]], axis=0) for b in range(WA_STEP)]

        ks = gather(kp_ref, kc_ref, kn_ref, kx_ref)
        vs = gather(vp_ref, vc_ref, vn_ref, vx_ref)
        band = 3 * WA_BLOCK
        out = jnp.full((WA_BLOCK, 2 * WA_BLOCK), -jnp.inf, F32)
        for b in range(WA_STEP):
            rows = slice(b * WA_BLOCK, (b + 1) * WA_BLOCK)
            lo = bias_ref[:WA_BLOCK, :]
            hi = bias_ref[2 * WA_BLOCK:band, :]
            if b == 0:
                lo = jnp.where(i > 0, lo, out)
            if b == WA_STEP - 1:
                hi = jnp.where(i < n_steps - 1, hi, out)
            bias = jnp.concatenate([lo, bias_ref[WA_BLOCK:2 * WA_BLOCK, :], hi, bias_ref[band:, :]], axis=0)
            for j in range(WA_HEADS // 2):
                c0 = (b * (WA_HEADS // 2) + j) * 2 * WA_BLOCK
                cols = slice(c0, c0 + 2 * WA_BLOCK)
                sk = _wa_pair_sinks(sink_ref, j)
                s = _da_scores(ks[b], _wa_pair_queries(q_ref, j, rows)) + bias
                s_scr[wr, :, cols] = s
                m_scr[wr, :, cols] = _wa_pair_max(s, sk)
                _wa_pair_finish(o_ref, j, s_scr[rd, :, cols], m_scr[rd, :, cols], sk, vs[b], rows)

    @pl.when(t % 2 == 0)
    def _():
        step(0, 1)

    @pl.when(t % 2 == 1)
    def _():
        step(1, 0)


def _wa_call(qw, kw, vw, sink, nb, n):
    n_blk = n // WA_BLOCK
    n_steps = n_blk // WA_STEP
    units = nb * n_steps
    ctx0 = nb * n // CTX_LEN

    def scored(t):
        u = jnp.minimum(t, units - 1)
        return u // n_steps, u % n_steps

    def finished(t):
        u = jnp.maximum(t - 1, 0)
        return u // n_steps, u % n_steps

    def q_map(which):
        def m(t):
            b, i = which(t)
            return (b * n_steps + i, 0)
        return m

    def edge(which, delta):
        def m(t):
            b, i = which(t)
            return (b * n_blk + jnp.clip(i * WA_STEP + delta, 0, n_blk - 1), 0)
        return pl.BlockSpec((WA_BLOCK, LANES), m)

    def own(which):
        return pl.BlockSpec((WA_STEP * WA_BLOCK, LANES), q_map(which))

    def ctx(which):
        return pl.BlockSpec((CTX_LEN, LANES), lambda t: (ctx0 + which(t)[0], 0))

    bias = jnp.asarray(_wa_band_bias())
    keys = bias.shape[0]
    width = WA_STEP * WA_HEADS * WA_BLOCK
    kern = functools.partial(_wa_kernel, n_steps=n_steps, units=units)
    return pl.pallas_call(
        kern,
        grid=(units + 1,),
        in_specs=[pl.BlockSpec(memory_space=pltpu.SMEM),
                  _const_spec(bias.shape),
                  pl.BlockSpec((WA_STEP * WA_BLOCK, 2 * BRANCH_WIDTH), q_map(scored)),
                  edge(scored, -1), own(scored), edge(scored, WA_STEP), ctx(scored),
                  edge(finished, -1), own(finished), edge(finished, WA_STEP), ctx(finished)],
        out_specs=pl.BlockSpec((WA_STEP * WA_BLOCK, BRANCH_WIDTH), q_map(finished)),
        out_shape=jax.ShapeDtypeStruct((nb * n, BRANCH_WIDTH), BF16),
        scratch_shapes=[pltpu.VMEM((2, keys, width), F32), pltpu.VMEM((2, 1, width), F32)],
        compiler_params=_params(1),
        name="window_attn",
    )(sink, bias, qw, kw, kw, kw, kw, vw, vw, vw, vw)


def _wa_ctx_kernel(sink_ref, q_ref, k_ref, v_ref, o_ref):
    k = k_ref[...]
    s_next = _da_scores(k, _wa_pair_queries(q_ref, 0))
    for j in range(WA_HEADS // 2):
        s = s_next
        if j + 1 < WA_HEADS // 2:
            s_next = _da_scores(k, _wa_pair_queries(q_ref, j + 1))
        sk = _wa_pair_sinks(sink_ref, j)
        _wa_pair_finish(o_ref, j, s, _wa_pair_max(s, sk), sk, v_ref[...])


def _wa_ctx_call(qw, kw, vw, sink, nb, n):
    ctx_blk = CTX_LEN // WA_BLOCK
    ctx0_q = nb * n // WA_BLOCK
    ctx0_k = nb * n // CTX_LEN
    keys = pl.BlockSpec((CTX_LEN, LANES), lambda b, i: (ctx0_k + b, 0))
    return pl.pallas_call(
        _wa_ctx_kernel,
        grid=(nb, ctx_blk),
        in_specs=[pl.BlockSpec(memory_space=pltpu.SMEM),
                  pl.BlockSpec((WA_BLOCK, 2 * BRANCH_WIDTH), lambda b, i: (ctx0_q + b * ctx_blk + i, 0)),
                  keys, keys],
        out_specs=pl.BlockSpec((WA_BLOCK, BRANCH_WIDTH), lambda b, i: (b * ctx_blk + i, 0)),
        out_shape=jax.ShapeDtypeStruct((nb * CTX_LEN, BRANCH_WIDTH), BF16),
        compiler_params=_params(2),
        name="window_attn_ctx",
    )(sink, qw, kw, vw)


def _pool_band():
    ext = POOL_SUB + 2 * POOL_HALO
    t = np.arange(POOL_SUB)[:, None]
    d = np.arange(ext)[None, :] - POOL_HALO - t
    return np.stack([((d >= -(w // 2)) & (d < w - w // 2)) for w in POOL_WINDOWS]).astype(np.float32)


def _merge_kernel(x_ref, mod_ref, g0_ref, g1_ref, wg_ref, yal_ref, yax_ref, ycl_ref, ycx_ref,
                  u_ref, up_ref, un_ref, band_ref, pw_ref, ps_ref, wb_ref, wo_ref, o_ref, *, n_lat, n):
    t = pl.program_id(0)
    x = x_ref[...]
    h = _modulate(x, g0_ref[...], mod_ref[0, 0:1, :], mod_ref[0, 1:2, :]).astype(BF16)
    is_lat = t < n_lat
    ya = jnp.where(is_lat, yal_ref[...], yax_ref[...])
    yc = jnp.where(is_lat, ycl_ref[...], ycx_ref[...])

    u = u_ref[...]
    halo_one = jnp.ones((POOL_HALO, POOL_GROUP), BF16)
    halo_zero = jnp.zeros((POOL_HALO, POOL_GROUP), BF16)
    n_sub = TM // POOL_SUB
    units = [(sb, g) for sb in range(n_sub) for g in range(POOL_GROUPS)]
    ext, ones = [], []
    for sb in range(n_sub):
        r0 = sb * POOL_SUB
        prev = up_ref[...] if sb == 0 else u[r0 - POOL_HALO:r0]
        nxt = un_ref[...] if r0 + POOL_SUB == TM else u[r0 + POOL_SUB:r0 + POOL_SUB + POOL_HALO]
        g_row = t * TM + r0
        seq_row = jnp.where(is_lat, g_row % n, g_row % CTX_LEN)
        seq_len = jnp.where(is_lat, n, CTX_LEN)
        prev_ok = seq_row != 0
        next_ok = seq_row + POOL_SUB != seq_len
        ext.append(jnp.concatenate([jnp.where(prev_ok, prev, jnp.zeros_like(prev)), u[r0:r0 + POOL_SUB],
                                    jnp.where(next_ok, nxt, jnp.zeros_like(nxt))], axis=0))
        ones.append(jnp.concatenate([jnp.where(prev_ok, halo_one, halo_zero),
                                     jnp.ones((POOL_SUB, POOL_GROUP), BF16),
                                     jnp.where(next_ok, halo_one, halo_zero)], axis=0))
    cols = lambda g: slice(g * POOL_GROUP, (g + 1) * POOL_GROUP)
    both = [jnp.dot(band_ref[g], jnp.concatenate([ext[sb][:, cols(g)], ones[sb]], axis=1),
                    preferred_element_type=F32) for sb, g in units]
    pooled = [(bt[:, :POOL_GROUP] / bt[:, POOL_GROUP:]
               - u[sb * POOL_SUB:(sb + 1) * POOL_SUB, cols(g)].astype(F32)).astype(BF16)
              for (sb, g), bt in zip(units, both)]
    outs = [jnp.dot(pq, pw_ref[g], preferred_element_type=F32) * ps_ref[:, cols(g)]
            for (sb, g), pq in zip(units, pooled)]
    yb = jnp.concatenate([jnp.concatenate(outs[sb * POOL_GROUPS:(sb + 1) * POOL_GROUPS], axis=1)
                          for sb in range(n_sub)], axis=0).astype(BF16)

    branches = (ya, yb, yc)
    m = None
    for k in range(N_BRANCH):
        gate = jnp.dot(h, wg_ref[:, k * D_MODEL:(k + 1) * D_MODEL], preferred_element_type=F32)
        gate = 1.0 / (1.0 + jnp.exp(-gate))
        term = gate * jnp.dot(branches[k], wb_ref[k], preferred_element_type=F32)
        m = term if m is None else m + term
    mix = jnp.dot(m.astype(BF16), wo_ref[...], preferred_element_type=F32)
    o_ref[...] = x + mod_ref[0, 2:3, :] * _rms(mix, g1_ref[...])


def _merge_call(tok, mod, g0, g1, wg, ya_lat, ya_ctx, yc_lat, yc_ctx, u, band, pool_w, pool_scale, wb, wo,
                nb, n, n_tiles):
    nt, d = tok.shape
    n_lat, mod_map, _ = _tile_maps(nb, n, CTX_LEN)
    row = lambda t: (t, 0)
    per_tile = TM // POOL_HALO
    last = nt // POOL_HALO - 1
    wide = pl.BlockSpec((TM, BRANCH_WIDTH), row)
    n_ctx = ya_ctx.shape[0] // TM
    lat_rows = pl.BlockSpec((TM, BRANCH_WIDTH), lambda t: (jnp.minimum(t, n_lat - 1), 0))
    ctx_rows = pl.BlockSpec((TM, BRANCH_WIDTH), lambda t: (jnp.clip(t - n_lat, 0, n_ctx - 1), 0))
    kern = functools.partial(_merge_kernel, n_lat=n_lat, n=n)
    return pl.pallas_call(
        kern,
        grid=(n_tiles,),
        in_specs=[pl.BlockSpec((TM, d), row),
                  pl.BlockSpec((1, 6, d), mod_map),
                  _const_spec((1, d)), _const_spec((1, d)),
                  _const_spec(wg.shape),
                  lat_rows, ctx_rows, lat_rows, ctx_rows, wide,
                  pl.BlockSpec((POOL_HALO, BRANCH_WIDTH), lambda t: (jnp.maximum(t * per_tile - 1, 0), 0)),
                  pl.BlockSpec((POOL_HALO, BRANCH_WIDTH), lambda t: (jnp.minimum((t + 1) * per_tile, last), 0)),
                  _const_spec(band.shape), _const_spec(pool_w.shape), _const_spec((1, BRANCH_WIDTH)),
                  _const_spec(wb.shape), _const_spec(wo.shape)],
        out_specs=pl.BlockSpec((TM, d), row),
        out_shape=jax.ShapeDtypeStruct((n_tiles * TM, d), F32),
        compiler_params=_params(1),
        name="pool_merge",
    )(tok, mod, g0, g1, wg, ya_lat, ya_ctx, yc_lat, yc_ctx, u, u, u, band, pool_w, pool_scale, wb, wo)


def _ffn_kernel(x_ref, mod_ref, g2_ref, g3_ref, wi_ref, wo_ref, o_ref):
    x = x_ref[...]
    h = _modulate(x, g2_ref[...], mod_ref[0, 3:4, :], mod_ref[0, 4:5, :]).astype(BF16)
    step = 256

    def gate_up(c):
        return (jnp.dot(h, wi_ref[:, c * step:(c + 1) * step], preferred_element_type=F32),
                jnp.dot(h, wi_ref[:, D_FF + c * step:D_FF + (c + 1) * step], preferred_element_type=F32))

    acc = None
    nxt = gate_up(0)
    for c in range(D_FF // step):
        gate, up = nxt
        if c + 1 < D_FF // step:
            nxt = gate_up(c + 1)
        act = (gate * (1.0 / (1.0 + jnp.exp(-gate))) * up).astype(BF16)
        part = jnp.dot(act, wo_ref[c * step:(c + 1) * step, :], preferred_element_type=F32)
        acc = part if acc is None else acc + part
    o_ref[...] = x + mod_ref[0, 5:6, :] * _rms(acc, g3_ref[...])


def _ffn_call(tok, mod, g2, g3, wi, wo, nb, n):
    nt, d = tok.shape
    _, mod_map, _ = _tile_maps(nb, n, CTX_LEN)
    row = lambda t: (t, 0)
    return pl.pallas_call(
        _ffn_kernel,
        grid=(nt // TM,),
        in_specs=[pl.BlockSpec((TM, d), row),
                  pl.BlockSpec((1, 6, d), mod_map),
                  _const_spec((1, d)), _const_spec((1, d)),
                  _const_spec(wi.shape), _const_spec(wo.shape)],
        out_specs=pl.BlockSpec((TM, d), row),
        out_shape=jax.ShapeDtypeStruct((nt, d), F32),
        compiler_params=_params(1),
        name="swiglu_ffn",
    )(tok, mod, g2, g3, wi, wo)


def _column_order():
    bw = BRANCH_WIDTH
    order = np.arange(PROJ_WIDTH)
    half = DA_QK_DIM // 2
    for base in (0, bw):
        for h in range(DA_HEADS):
            for p in range(2):
                for s in range(2):
                    for i in range(half):
                        order[base + h * DA_V_DIM + p * 64 + s * half + i] = (
                            base + h * DA_V_DIM + s * DA_QK_DIM + 2 * i + p)
    half = WA_HEAD_DIM // 2
    base = 4 * bw
    for j in range(WA_HEADS // 2):
        for p in range(2):
            for g in range(2):
                for i in range(half):
                    order[base + j * LANES + p * 64 + g * half + i] = (
                        base + (j + (WA_HEADS // 2) * g) * WA_HEAD_DIM + 2 * i + p)
    base = 5 * bw
    for p in range(2):
        for kv in range(WA_KV_HEADS):
            for i in range(half):
                order[base + p * 64 + kv * half + i] = base + kv * WA_HEAD_DIM + 2 * i + p
    return order


def _wa_row_order():
    order = np.arange(BRANCH_WIDTH)
    for j in range(WA_HEADS // 2):
        for g in range(2):
            for dd in range(WA_HEAD_DIM):
                order[j * LANES + g * WA_HEAD_DIM + dd] = (j + (WA_HEADS // 2) * g) * WA_HEAD_DIM + dd
    return order


def _rope_tables(n):
    rows = n // GRID_W
    row = jnp.repeat(jnp.arange(rows, dtype=F32), GRID_W)
    col = jnp.tile(jnp.arange(GRID_W, dtype=F32), rows)
    n_freq = DA_QK_DIM // 4
    inv = ROPE_THETA ** (-jnp.arange(n_freq, dtype=F32) / n_freq)
    ang = jnp.concatenate([row[:, None] * inv, col[:, None] * inv], axis=-1)
    cos = jnp.tile(jnp.cos(ang), (1, 4))
    sin = jnp.tile(jnp.sin(ang), (1, 4)) * jnp.where(jnp.arange(LANES) < 64, -1.0, 1.0)
    cos = jnp.concatenate([cos, jnp.ones((TM, LANES), F32)], axis=0)
    sin = jnp.concatenate([sin, jnp.zeros((TM, LANES), F32)], axis=0)
    return cos, sin


def kernel(x, c, ctx, c_ctx, w_ada, b_ada, norm_g, w_in, da_lambda, da_subln_g,
           pool_w, pool_scale, wa_sink, w_branch, w_out, w_ffn_in, w_ffn_out):
    nb, n, d = x.shape
    depth = w_ada.shape[0]
    assert d == D_MODEL and ctx.shape[1] == CTX_LEN
    assert n % TM == 0 and (nb * CTX_LEN) % TM == 0 and n % CK == 0 and n % GRID_W == 0
    assert n % TQ == 0 and n % (WA_STEP * WA_BLOCK) == 0
    assert TM % POOL_SUB == 0 and CTX_LEN % POOL_SUB == 0 and (nb * n) % CTX_LEN == 0
    assert WA_HEAD_DIM == DA_QK_DIM

    tok = jnp.concatenate([x.reshape(nb * n, d), ctx.reshape(nb * CTX_LEN, d)], axis=0)
    cvec = jnp.concatenate([c, c_ctx[None, :]], axis=0)
    mod = _mod_call(cvec, w_ada, b_ada).reshape(depth, nb + 1, 6, d)
    cos, sin = _rope_tables(n)
    band = jnp.asarray(_pool_band(), BF16)

    w1 = w_in[:, :, _column_order()].astype(BF16)
    wg = w_in[:, :, PROJ_WIDTH:].astype(BF16)
    wb = jnp.concatenate([w_branch[:, :2], w_branch[:, 2:, _wa_row_order()]], axis=1).astype(BF16)
    wo = w_out.astype(BF16)
    wfi = w_ffn_in.astype(BF16)
    wfo = w_ffn_out.astype(BF16)
    pw = pool_w.astype(BF16)

    n_tiles_all = tok.shape[0] // TM
    n_tiles_lat = nb * n // TM
    for l in range(depth):
        last = l == depth - 1
        lam_init = 0.8 - 0.6 * math.exp(-0.3 * l)
        g = norm_g[l].reshape(4, 1, d)
        qa, ka, va, u, qw, kw, vw = _proj_call(tok, mod[l], g[0], w1[l], cos, sin, nb, n)
        g_sub = da_subln_g[l].reshape(1, DA_V_DIM)
        ya = _da_call(qa, ka, va, da_lambda[l], g_sub, lam_init, nb, n)
        ya_ctx = ya if last else _da_ctx_call(qa, ka, va, da_lambda[l], g_sub, lam_init, nb, n)
        yc = _wa_call(qw, kw, vw, wa_sink[l], nb, n)
        yc_ctx = yc if last else _wa_ctx_call(qw, kw, vw, wa_sink[l], nb, n)
        n_tiles = n_tiles_lat if last else n_tiles_all
        tok = _merge_call(tok, mod[l], g[0], g[1], wg[l], ya, ya_ctx, yc, yc_ctx, u, band, pw[l],
                          pool_scale[l].reshape(1, BRANCH_WIDTH), wb[l], wo[l], nb, n, n_tiles)
        tok = _ffn_call(tok, mod[l], g[2], g[3], wfi[l], wfo[l], nb, n)
    return tok[:nb * n].reshape(nb, n, d)
```

```python
import functools
import math

import numpy as np
import jax
import jax.numpy as jnp
from jax import lax
from jax.experimental import pallas as pl
from jax.experimental.pallas import tpu as pltpu

F32 = jnp.float32
BF16 = jnp.bfloat16

D_MODEL = 1024
CTX_LEN = 256
GRID_W = 64
EPS = 1e-6
ROPE_THETA = 10000.0
BRANCH_WIDTH = D_MODEL // 2
N_BRANCH = 3
DA_HEADS = 4
DA_V_DIM = BRANCH_WIDTH // DA_HEADS
DA_QK_DIM = DA_V_DIM // 2
POOL_WINDOWS = (2, 4, 8, 16)
POOL_GROUPS = 4
POOL_GROUP = BRANCH_WIDTH // POOL_GROUPS
WA_HEAD_DIM = 64
WA_HEADS = BRANCH_WIDTH // WA_HEAD_DIM
WA_KV_HEADS = 2
WA_GROUP = WA_HEADS // WA_KV_HEADS
WA_WINDOW = 128
WA_BLOCK = 128
D_FF = ((8 * D_MODEL + 3 * 256 - 1) // (3 * 256)) * 256
PROJ_WIDTH = 5 * BRANCH_WIDTH + 2 * WA_KV_HEADS * WA_HEAD_DIM
GATE_WIDTH = N_BRANCH * D_MODEL

LANES = 128
BF16_SUBLANES = 16
VMEM_LIMIT = 56 * 1024 * 1024

TM = 512
TQ = 512
CK = 256
WA_STEP = 4
POOL_SUB = 128
POOL_HALO = BF16_SUBLANES


def _params(n_axes):
    return pltpu.CompilerParams(dimension_semantics=("arbitrary",) * n_axes,
                                vmem_limit_bytes=VMEM_LIMIT)


def _const_spec(shape):
    nd = len(shape)
    return pl.BlockSpec(shape, lambda *_: (0,) * nd, pipeline_mode=pl.Buffered(1))


def _rms(x, g):
    return x * lax.rsqrt(jnp.mean(x * x, axis=-1, keepdims=True) + EPS) * g


def _modulate(x, g, shift, scale):
    return _rms(x, g) * (1.0 + scale) + shift


def _mod_kernel(c_ref, w_ref, b_ref, o_ref):
    c = c_ref[...]
    a = c * (1.0 / (1.0 + jnp.exp(-c)))
    o_ref[0] = jnp.dot(a, w_ref[0], preferred_element_type=F32,
                       precision=lax.Precision.HIGHEST) + b_ref[0]


def _mod_call(cvec, w_ada, b_ada):
    depth, d, width = w_ada.shape
    rows = cvec.shape[0]
    tn = 1536
    return pl.pallas_call(
        _mod_kernel,
        grid=(depth, width // tn),
        in_specs=[pl.BlockSpec((rows, d), lambda l, j: (0, 0)),
                  pl.BlockSpec((1, d, tn), lambda l, j: (l, 0, j)),
                  pl.BlockSpec((1, 1, tn), lambda l, j: (l, 0, j))],
        out_specs=pl.BlockSpec((1, rows, tn), lambda l, j: (l, 0, j)),
        out_shape=jax.ShapeDtypeStruct((depth, rows, width), F32),
        compiler_params=_params(2),
        name="adaln_mod",
    )(cvec, w_ada, b_ada.reshape(depth, 1, width))


def _proj_kernel(x_ref, mod_ref, g_ref, w_ref, c_ref, s_ref,
                 qa_ref, ka_ref, va_ref, u_ref, qw_ref, kw_ref, vw_ref):
    h = _modulate(x_ref[...], g_ref[...], mod_ref[0, 0:1, :], mod_ref[0, 1:2, :]).astype(BF16)
    cos = c_ref[...]
    sin = s_ref[...]

    def proj(lo, width):
        return jnp.dot(h, w_ref[:, lo:lo + width], preferred_element_type=F32)

    def rope(yj):
        return yj * cos + pltpu.roll(yj, LANES // 2, 1) * sin

    def key_store(ref, y):
        for j in range(y.shape[1] // LANES):
            ref[:, j * LANES:(j + 1) * LANES] = rope(y[:, j * LANES:(j + 1) * LANES]).astype(BF16)

    lane = lax.broadcasted_iota(jnp.int32, (TM, LANES), 1)
    first = (lane & (DA_QK_DIM // 2)) == 0

    def query_store(ref, y, scale):
        for j in range(y.shape[1] // LANES):
            r = rope(y[:, j * LANES:(j + 1) * LANES]) * scale
            ref[:, (2 * j) * LANES:(2 * j + 1) * LANES] = jnp.where(first, r, 0.0).astype(BF16)
            ref[:, (2 * j + 1) * LANES:(2 * j + 2) * LANES] = jnp.where(first, 0.0, r).astype(BF16)

    bw = BRANCH_WIDTH
    ya, yk, yv, yu, yq = (proj(j * bw, bw) for j in range(5))
    kvw = proj(5 * bw, 2 * LANES)
    query_store(qa_ref, ya, DA_QK_DIM ** -0.5 * math.log2(math.e))
    key_store(ka_ref, yk)
    va_ref[...] = yv.T.astype(BF16)
    u_ref[...] = yu.astype(BF16)
    query_store(qw_ref, yq, WA_HEAD_DIM ** -0.5 * math.log2(math.e))
    key_store(kw_ref, kvw[:, :LANES])
    vw_ref[...] = kvw[:, LANES:].astype(BF16)


def _tile_maps(nb, n, ctx):
    n_lat = nb * n // TM
    per_seq = n // TM

    def mod_map(t):
        return (jnp.where(t < n_lat, t // per_seq, nb), 0, 0)

    def rope_map(t):
        return (jnp.where(t < n_lat, t % per_seq, per_seq), 0)

    return n_lat, mod_map, rope_map


def _proj_call(tok, mod, g, w1, cos, sin, nb, n):
    nt, d = tok.shape
    _, mod_map, rope_map = _tile_maps(nb, n, CTX_LEN)
    row = lambda t: (t, 0)
    spec = lambda w: pl.BlockSpec((TM, w), row)
    sd = lambda w: jax.ShapeDtypeStruct((nt, w), BF16)
    widths = (2 * BRANCH_WIDTH, BRANCH_WIDTH, BRANCH_WIDTH, BRANCH_WIDTH, 2 * BRANCH_WIDTH, LANES, LANES)
    return pl.pallas_call(
        _proj_kernel,
        grid=(nt // TM,),
        in_specs=[pl.BlockSpec((TM, d), row),
                  pl.BlockSpec((1, 6, d), mod_map),
                  _const_spec((1, d)),
                  _const_spec(w1.shape),
                  pl.BlockSpec((TM, LANES), rope_map),
                  pl.BlockSpec((TM, LANES), rope_map)],
        out_specs=[pl.BlockSpec((BRANCH_WIDTH, TM), lambda t: (0, t)) if j == 2 else spec(w)
                   for j, w in enumerate(widths)],
        out_shape=[jax.ShapeDtypeStruct((BRANCH_WIDTH, nt), BF16) if j == 2 else sd(w)
                   for j, w in enumerate(widths)],
        compiler_params=_params(1),
        name="in_proj",
    )(tok, mod, g, w1, cos, sin)


def _da_lambda(dl_ref, lam_init):
    lp = dl_ref[...]
    return (jnp.exp(jnp.sum(lp[0:1] * lp[1:2], axis=-1, keepdims=True))
            - jnp.exp(jnp.sum(lp[2:3] * lp[3:4], axis=-1, keepdims=True)) + lam_init)


def _da_scores(k, qs):
    return lax.dot_general(k, qs, (((1,), (1,)), ((), ())), preferred_element_type=F32)


def _da_fold(x, op):
    return op(x.reshape(x.shape[0] // 8, 8, x.shape[1]), axis=0)


def _da_pv(v, e):
    return lax.dot_general(v, e.astype(BF16), (((0,), (0,)), ((), ())), preferred_element_type=F32)


def _da_pvt(vt, e):
    return jnp.dot(vt, e.astype(BF16), preferred_element_type=F32)


def _da_finish(o_ref, g_ref, lam, lam_init, l8, acc):
    tq = o_ref.shape[0]
    l = jnp.sum(l8, axis=0, keepdims=True)
    ot = acc[:, :tq] / l[:, :tq] - lam * (acc[:, tq:] / l[:, tq:])
    o_ref[...] = (_rms(ot.T, g_ref[...]) * (1.0 - lam_init)).astype(BF16)


def _da_kernel(dl_ref, g_ref, q_ref, kl_ref, kc_ref, vl_ref, vc_ref, o_ref, s_scr, m_scr, l_scr, acc_scr, *,
               lam_init):
    t = pl.program_id(0)
    lam = _da_lambda(dl_ref, lam_init)
    n_lat = kl_ref.shape[0]
    qs = jnp.concatenate([q_ref[:, :LANES], q_ref[:, LANES:]], axis=0)

    @pl.when(t == 0)
    def _():
        s_scr[1] = jnp.zeros(s_scr.shape[1:], F32)
        m_scr[1] = jnp.zeros(m_scr.shape[1:], F32)
        l_scr[...] = jnp.ones(l_scr.shape, F32)
        acc_scr[...] = jnp.zeros(acc_scr.shape, F32)

    def step(wr, rd):
        _da_finish(o_ref, g_ref, lam, lam_init, l_scr[...], acc_scr[...])
        m_prev = m_scr[rd]
        m8 = l8 = acc = None
        for c in range(n_lat // CK):
            rows = slice(c * CK, (c + 1) * CK)
            s = _da_scores(kl_ref[rows, :], qs)
            s_scr[wr, rows, :] = s
            f = _da_fold(s, jnp.max)
            m8 = f if m8 is None else jnp.maximum(m8, f)
            e = jnp.exp2(s_scr[rd, rows, :] - m_prev)
            f = _da_fold(e, jnp.sum)
            p = _da_pvt(vl_ref[:, rows], e)
            l8 = f if l8 is None else l8 + f
            acc = p if acc is None else acc + p
        rows = slice(n_lat, n_lat + CTX_LEN)
        sc = _da_scores(kc_ref[...], qs)
        s_scr[wr, rows, :] = sc
        m_scr[wr] = jnp.max(jnp.maximum(m8, _da_fold(sc, jnp.max)), axis=0, keepdims=True)
        e = jnp.exp2(s_scr[rd, rows, :] - m_prev)
        l_scr[...] = l8 + _da_fold(e, jnp.sum)
        acc_scr[...] = acc + _da_pvt(vc_ref[...], e)

    @pl.when(t % 2 == 0)
    def _():
        step(0, 1)

    @pl.when(t % 2 == 1)
    def _():
        step(1, 0)


def _da_call(qa, ka, va, da_lambda, g_sub, lam_init, nb, n):
    n_q = n // TQ
    units = nb * DA_HEADS * n_q
    ctx0 = nb * n // CTX_LEN

    def unit(u):
        u = jnp.clip(u, 0, units - 1)
        return u // (n_q * DA_HEADS), (u // n_q) % DA_HEADS, u % n_q

    def scored(t):
        return unit(t)

    def finished(t):
        return unit(t - 1)

    def written(t):
        return unit(t - 2)

    def tile_map(which):
        def m(t):
            b, h, i = which(t)
            return (b * n_q + i, h)
        return m

    def lat(which):
        return pl.BlockSpec((n, DA_V_DIM), lambda t: which(t)[:2])

    def ctx(which):
        def m(t):
            b, h, _ = which(t)
            return (ctx0 + b, h)
        return pl.BlockSpec((CTX_LEN, DA_V_DIM), m)

    def lat_t(which):
        return pl.BlockSpec((DA_V_DIM, n), lambda t: which(t)[1::-1])

    def ctx_t(which):
        def m(t):
            b, h, _ = which(t)
            return (h, ctx0 + b)
        return pl.BlockSpec((DA_V_DIM, CTX_LEN), m)

    kern = functools.partial(_da_kernel, lam_init=lam_init)
    return pl.pallas_call(
        kern,
        grid=(units + 2,),
        in_specs=[pl.BlockSpec(da_lambda.shape, lambda t: (0, 0)),
                  pl.BlockSpec((1, DA_V_DIM), lambda t: (0, 0)),
                  pl.BlockSpec((TQ, 2 * DA_V_DIM), tile_map(scored)),
                  lat(scored), ctx(scored), lat_t(finished), ctx_t(finished)],
        out_specs=pl.BlockSpec((TQ, DA_V_DIM), tile_map(written)),
        out_shape=jax.ShapeDtypeStruct((nb * n, BRANCH_WIDTH), BF16),
        scratch_shapes=[pltpu.VMEM((2, n + CTX_LEN, 2 * TQ), F32), pltpu.VMEM((2, 1, 2 * TQ), F32),
                        pltpu.VMEM((8, 2 * TQ), F32), pltpu.VMEM((DA_V_DIM, 2 * TQ), F32)],
        compiler_params=_params(1),
        name="diff_attn",
    )(da_lambda, g_sub, qa, ka, ka, va, va)


def _da_ctx_kernel(dl_ref, g_ref, q_ref, k_ref, v_ref, o_ref, *, lam_init):
    lam = _da_lambda(dl_ref, lam_init)
    heads = range(DA_HEADS)
    d = DA_V_DIM
    qs = [jnp.concatenate([q_ref[:, 2 * h * d:(2 * h + 1) * d], q_ref[:, (2 * h + 1) * d:(2 * h + 2) * d]], axis=0)
          for h in heads]
    ss = [_da_scores(k_ref[:, h * d:(h + 1) * d], qs[h]) for h in heads]
    es = [jnp.exp2(s - jnp.max(s, axis=0, keepdims=True)) for s in ss]
    accs = [_da_pvt(v_ref[h * d:(h + 1) * d, :], es[h]) for h in heads]
    for h in heads:
        _da_finish(o_ref.at[:, h * d:(h + 1) * d], g_ref, lam, lam_init, _da_fold(es[h], jnp.sum), accs[h])


def _da_ctx_call(qa, ka, va, da_lambda, g_sub, lam_init, nb, n):
    ctx0 = nb * n // CTX_LEN
    blk = lambda w: pl.BlockSpec((CTX_LEN, w), lambda b: (ctx0 + b, 0))
    kern = functools.partial(_da_ctx_kernel, lam_init=lam_init)
    return pl.pallas_call(
        kern,
        grid=(nb,),
        in_specs=[pl.BlockSpec(da_lambda.shape, lambda b: (0, 0)),
                  pl.BlockSpec((1, DA_V_DIM), lambda b: (0, 0)),
                  blk(2 * BRANCH_WIDTH), blk(BRANCH_WIDTH),
                  pl.BlockSpec((BRANCH_WIDTH, CTX_LEN), lambda b: (0, ctx0 + b))],
        out_specs=pl.BlockSpec((CTX_LEN, BRANCH_WIDTH), lambda b: (b, 0)),
        out_shape=jax.ShapeDtypeStruct((nb * CTX_LEN, BRANCH_WIDTH), BF16),
        compiler_params=_params(1),
        name="diff_attn_ctx",
    )(da_lambda, g_sub, qa, ka, va)


def _wa_pair_queries(q_ref, j, rows=slice(0, WA_BLOCK)):
    return jnp.concatenate([q_ref[rows, (2 * j) * LANES:(2 * j + 1) * LANES],
                            q_ref[rows, (2 * j + 1) * LANES:(2 * j + 2) * LANES]], axis=0)


def _wa_pair_sinks(sink_ref, j):
    lane = lax.broadcasted_iota(jnp.int32, (1, 2 * WA_BLOCK), 1)
    return jnp.where(lane < WA_BLOCK, sink_ref[j], sink_ref[j + WA_HEADS // 2]) * math.log2(math.e)


def _wa_pair_max(s, sk):
    return jnp.maximum(jnp.max(_da_fold(s, jnp.max), axis=0, keepdims=True), sk)


def _wa_pair_finish(o_ref, j, s, m, sk, v, rows=slice(0, WA_BLOCK)):
    e = jnp.exp2(s - m)
    den = jnp.sum(_da_fold(e, jnp.sum), axis=0, keepdims=True) + jnp.exp2(sk - m)
    ot = _da_pv(v, e) / den
    first = lax.broadcasted_iota(jnp.int32, (LANES, WA_BLOCK), 0) < WA_HEAD_DIM
    o_ref[rows, j * LANES:(j + 1) * LANES] = jnp.where(first, ot[:, :WA_BLOCK], ot[:, WA_BLOCK:]).T.astype(BF16)


def _wa_band_bias():
    band = 3 * WA_BLOCK
    key = np.arange(band + CTX_LEN)[:, None]
    qpos = np.arange(2 * WA_BLOCK)[None, :] % WA_BLOCK
    ok = (key >= band) | (np.abs(key - WA_BLOCK - qpos) <= WA_WINDOW)
    return np.where(ok, 0.0, -np.inf).astype(np.float32)


def _wa_kernel(sink_ref, bias_ref, q_ref, kp_ref, kc_ref, kn_ref, kx_ref, vp_ref, vc_ref, vn_ref, vx_ref,
               o_ref, s_scr, m_scr, *, n_steps, units):
    t = pl.program_id(0)
    i = lax.rem(jnp.minimum(t, units - 1), n_steps)

    @pl.when(t == 0)
    def _():
        s_scr[1] = jnp.zeros(s_scr.shape[1:], F32)
        m_scr[1] = jnp.zeros(m_scr.shape[1:], F32)

    def step(wr, rd):
        def gather(p_ref, c_ref, n_ref, x_ref):
            blocks = [p_ref[...]] + [c_ref[b * WA_BLOCK:(b + 1) * WA_BLOCK, :] for b in range(WA_STEP)] + [n_ref[...]]
            return [jnp.concatenate(blocks[b:b + 3] + [x_ref[...]], axis=0) for b in range(WA_STEP)]

        ks = gather(kp_ref, kc_ref, kn_ref, kx_ref)
        vs = gather(vp_ref, vc_ref, vn_ref, vx_ref)
        band = 3 * WA_BLOCK
        out = jnp.full((WA_BLOCK, 2 * WA_BLOCK), -jnp.inf, F32)
        for b in range(WA_STEP):
            rows = slice(b * WA_BLOCK, (b + 1) * WA_BLOCK)
            lo = bias_ref[:WA_BLOCK, :]
            hi = bias_ref[2 * WA_BLOCK:band, :]
            if b == 0:
                lo = jnp.where(i > 0, lo, out)
            if b == WA_STEP - 1:
                hi = jnp.where(i < n_steps - 1, hi, out)
            bias = jnp.concatenate([lo, bias_ref[WA_BLOCK:2 * WA_BLOCK, :], hi, bias_ref[band:, :]], axis=0)
            for j in range(WA_HEADS // 2):
                c0 = (b * (WA_HEADS // 2) + j) * 2 * WA_BLOCK
                cols = slice(c0, c0 + 2 * WA_BLOCK)
                sk = _wa_pair_sinks(sink_ref, j)
                s = _da_scores(ks[b], _wa_pair_queries(q_ref, j, rows)) + bias
                s_scr[wr, :, cols] = s
                m_scr[wr, :, cols] = _wa_pair_max(s, sk)
                _wa_pair_finish(o_ref, j, s_scr[rd, :, cols], m_scr[rd, :, cols], sk, vs[b], rows)

    @pl.when(t % 2 == 0)
    def _():
        step(0, 1)

    @pl.when(t % 2 == 1)
    def _():
        step(1, 0)


def _wa_call(qw, kw, vw, sink, nb, n):
    n_blk = n // WA_BLOCK
    n_steps = n_blk // WA_STEP
    units = nb * n_steps
    ctx0 = nb * n // CTX_LEN

    def scored(t):
        u = jnp.minimum(t, units - 1)
        return u // n_steps, u % n_steps

    def finished(t):
        u = jnp.maximum(t - 1, 0)
        return u // n_steps, u % n_steps

    def q_map(which):
        def m(t):
            b, i = which(t)
            return (b * n_steps + i, 0)
        return m

    def edge(which, delta):
        def m(t):
            b, i = which(t)
            return (b * n_blk + jnp.clip(i * WA_STEP + delta, 0, n_blk - 1), 0)
        return pl.BlockSpec((WA_BLOCK, LANES), m)

    def own(which):
        return pl.BlockSpec((WA_STEP * WA_BLOCK, LANES), q_map(which))

    def ctx(which):
        return pl.BlockSpec((CTX_LEN, LANES), lambda t: (ctx0 + which(t)[0], 0))

    bias = jnp.asarray(_wa_band_bias())
    keys = bias.shape[0]
    width = WA_STEP * WA_HEADS * WA_BLOCK
    kern = functools.partial(_wa_kernel, n_steps=n_steps, units=units)
    return pl.pallas_call(
        kern,
        grid=(units + 1,),
        in_specs=[pl.BlockSpec(memory_space=pltpu.SMEM),
                  _const_spec(bias.shape),
                  pl.BlockSpec((WA_STEP * WA_BLOCK, 2 * BRANCH_WIDTH), q_map(scored)),
                  edge(scored, -1), own(scored), edge(scored, WA_STEP), ctx(scored),
                  edge(finished, -1), own(finished), edge(finished, WA_STEP), ctx(finished)],
        out_specs=pl.BlockSpec((WA_STEP * WA_BLOCK, BRANCH_WIDTH), q_map(finished)),
        out_shape=jax.ShapeDtypeStruct((nb * n, BRANCH_WIDTH), BF16),
        scratch_shapes=[pltpu.VMEM((2, keys, width), F32), pltpu.VMEM((2, 1, width), F32)],
        compiler_params=_params(1),
        name="window_attn",
    )(sink, bias, qw, kw, kw, kw, kw, vw, vw, vw, vw)


def _wa_ctx_kernel(sink_ref, q_ref, k_ref, v_ref, o_ref):
    k = k_ref[...]
    units = [(slice(b * WA_BLOCK, (b + 1) * WA_BLOCK), j)
             for b in range(CTX_LEN // WA_BLOCK) for j in range(WA_HEADS // 2)]
    ss = [_da_scores(k, _wa_pair_queries(q_ref, j, rows)) for rows, j in units]
    for (rows, j), s in zip(units, ss):
        sk = _wa_pair_sinks(sink_ref, j)
        _wa_pair_finish(o_ref, j, s, _wa_pair_max(s, sk), sk, v_ref[...], rows)


def _wa_ctx_call(qw, kw, vw, sink, nb, n):
    ctx0 = nb * n // CTX_LEN
    rows = lambda w: pl.BlockSpec((CTX_LEN, w), lambda b: (ctx0 + b, 0))
    return pl.pallas_call(
        _wa_ctx_kernel,
        grid=(nb,),
        in_specs=[pl.BlockSpec(memory_space=pltpu.SMEM), rows(2 * BRANCH_WIDTH), rows(LANES), rows(LANES)],
        out_specs=pl.BlockSpec((CTX_LEN, BRANCH_WIDTH), lambda b: (b, 0)),
        out_shape=jax.ShapeDtypeStruct((nb * CTX_LEN, BRANCH_WIDTH), BF16),
        compiler_params=_params(1),
        name="window_attn_ctx",
    )(sink, qw, kw, vw)


def _pool_band():
    ext = POOL_SUB + 2 * POOL_HALO
    t = np.arange(POOL_SUB)[:, None]
    d = np.arange(ext)[None, :] - POOL_HALO - t
    return np.stack([((d >= -(w // 2)) & (d < w - w // 2)) for w in POOL_WINDOWS]).astype(np.float32)


def _merge_kernel(x_ref, mod_ref, g0_ref, g1_ref, wg_ref, yal_ref, yax_ref, ycl_ref, ycx_ref,
                  u_ref, up_ref, un_ref, band_ref, pw_ref, ps_ref, wb_ref, wo_ref, o_ref, *, n_lat, n):
    t = pl.program_id(0)
    x = x_ref[...]
    h = _modulate(x, g0_ref[...], mod_ref[0, 0:1, :], mod_ref[0, 1:2, :]).astype(BF16)
    is_lat = t < n_lat
    ya = jnp.where(is_lat, yal_ref[...], yax_ref[...])
    yc = jnp.where(is_lat, ycl_ref[...], ycx_ref[...])

    u = u_ref[...]
    halo_one = jnp.ones((POOL_HALO, POOL_GROUP), BF16)
    halo_zero = jnp.zeros((POOL_HALO, POOL_GROUP), BF16)
    n_sub = TM // POOL_SUB
    units = [(sb, g) for sb in range(n_sub) for g in range(POOL_GROUPS)]
    ext, ones = [], []
    for sb in range(n_sub):
        r0 = sb * POOL_SUB
        prev = up_ref[...] if sb == 0 else u[r0 - POOL_HALO:r0]
        nxt = un_ref[...] if r0 + POOL_SUB == TM else u[r0 + POOL_SUB:r0 + POOL_SUB + POOL_HALO]
        g_row = t * TM + r0
        seq_row = jnp.where(is_lat, g_row % n, g_row % CTX_LEN)
        seq_len = jnp.where(is_lat, n, CTX_LEN)
        prev_ok = seq_row != 0
        next_ok = seq_row + POOL_SUB != seq_len
        ext.append(jnp.concatenate([jnp.where(prev_ok, prev, jnp.zeros_like(prev)), u[r0:r0 + POOL_SUB],
                                    jnp.where(next_ok, nxt, jnp.zeros_like(nxt))], axis=0))
        ones.append(jnp.concatenate([jnp.where(prev_ok, halo_one, halo_zero),
                                     jnp.ones((POOL_SUB, POOL_GROUP), BF16),
                                     jnp.where(next_ok, halo_one, halo_zero)], axis=0))
    cols = lambda g: slice(g * POOL_GROUP, (g + 1) * POOL_GROUP)
    both = [jnp.dot(band_ref[g], jnp.concatenate([ext[sb][:, cols(g)], ones[sb]], axis=1),
                    preferred_element_type=F32) for sb, g in units]
    pooled = [(bt[:, :POOL_GROUP] / bt[:, POOL_GROUP:]
               - u[sb * POOL_SUB:(sb + 1) * POOL_SUB, cols(g)].astype(F32)).astype(BF16)
              for (sb, g), bt in zip(units, both)]
    outs = [jnp.dot(pq, pw_ref[g], preferred_element_type=F32) * ps_ref[:, cols(g)]
            for (sb, g), pq in zip(units, pooled)]
    yb = jnp.concatenate([jnp.concatenate(outs[sb * POOL_GROUPS:(sb + 1) * POOL_GROUPS], axis=1)
                          for sb in range(n_sub)], axis=0).astype(BF16)

    branches = (ya, yb, yc)
    m = None
    for k in range(N_BRANCH):
        gate = jnp.dot(h, wg_ref[:, k * D_MODEL:(k + 1) * D_MODEL], preferred_element_type=F32)
        gate = 1.0 / (1.0 + jnp.exp(-gate))
        term = gate * jnp.dot(branches[k], wb_ref[k], preferred_element_type=F32)
        m = term if m is None else m + term
    mix = jnp.dot(m.astype(BF16), wo_ref[...], preferred_element_type=F32)
    o_ref[...] = x + mod_ref[0, 2:3, :] * _rms(mix, g1_ref[...])


def _merge_call(tok, mod, g0, g1, wg, ya_lat, ya_ctx, yc_lat, yc_ctx, u, band, pool_w, pool_scale, wb, wo,
                nb, n, n_tiles):
    nt, d = tok.shape
    n_lat, mod_map, _ = _tile_maps(nb, n, CTX_LEN)
    row = lambda t: (t, 0)
    per_tile = TM // POOL_HALO
    last = nt // POOL_HALO - 1
    wide = pl.BlockSpec((TM, BRANCH_WIDTH), row)
    n_ctx = ya_ctx.shape[0] // TM
    lat_rows = pl.BlockSpec((TM, BRANCH_WIDTH), lambda t: (jnp.minimum(t, n_lat - 1), 0))
    ctx_rows = pl.BlockSpec((TM, BRANCH_WIDTH), lambda t: (jnp.clip(t - n_lat, 0, n_ctx - 1), 0))
    kern = functools.partial(_merge_kernel, n_lat=n_lat, n=n)
    return pl.pallas_call(
        kern,
        grid=(n_tiles,),
        in_specs=[pl.BlockSpec((TM, d), row),
                  pl.BlockSpec((1, 6, d), mod_map),
                  _const_spec((1, d)), _const_spec((1, d)),
                  _const_spec(wg.shape),
                  lat_rows, ctx_rows, lat_rows, ctx_rows, wide,
                  pl.BlockSpec((POOL_HALO, BRANCH_WIDTH), lambda t: (jnp.maximum(t * per_tile - 1, 0), 0)),
                  pl.BlockSpec((POOL_HALO, BRANCH_WIDTH), lambda t: (jnp.minimum((t + 1) * per_tile, last), 0)),
                  _const_spec(band.shape), _const_spec(pool_w.shape), _const_spec((1, BRANCH_WIDTH)),
                  _const_spec(wb.shape), _const_spec(wo.shape)],
        out_specs=pl.BlockSpec((TM, d), row),
        out_shape=jax.ShapeDtypeStruct((n_tiles * TM, d), F32),
        compiler_params=_params(1),
        name="pool_merge",
    )(tok, mod, g0, g1, wg, ya_lat, ya_ctx, yc_lat, yc_ctx, u, u, u, band, pool_w, pool_scale, wb, wo)


def _ffn_kernel(x_ref, mod_ref, g2_ref, g3_ref, wi_ref, wo_ref, o_ref):
    x = x_ref[...]
    h = _modulate(x, g2_ref[...], mod_ref[0, 3:4, :], mod_ref[0, 4:5, :]).astype(BF16)
    step = 256

    def gate_up(c):
        return (jnp.dot(h, wi_ref[:, c * step:(c + 1) * step], preferred_element_type=F32),
                jnp.dot(h, wi_ref[:, D_FF + c * step:D_FF + (c + 1) * step], preferred_element_type=F32))

    acc = None
    nxt = gate_up(0)
    for c in range(D_FF // step):
        gate, up = nxt
        if c + 1 < D_FF // step:
            nxt = gate_up(c + 1)
        act = (gate * (1.0 / (1.0 + jnp.exp(-gate))) * up).astype(BF16)
        part = jnp.dot(act, wo_ref[c * step:(c + 1) * step, :], preferred_element_type=F32)
        acc = part if acc is None else acc + part
    o_ref[...] = x + mod_ref[0, 5:6, :] * _rms(acc, g3_ref[...])


def _ffn_call(tok, mod, g2, g3, wi, wo, nb, n):
    nt, d = tok.shape
    _, mod_map, _ = _tile_maps(nb, n, CTX_LEN)
    row = lambda t: (t, 0)
    return pl.pallas_call(
        _ffn_kernel,
        grid=(nt // TM,),
        in_specs=[pl.BlockSpec((TM, d), row),
                  pl.BlockSpec((1, 6, d), mod_map),
                  _const_spec((1, d)), _const_spec((1, d)),
                  _const_spec(wi.shape), _const_spec(wo.shape)],
        out_specs=pl.BlockSpec((TM, d), row),
        out_shape=jax.ShapeDtypeStruct((nt, d), F32),
        compiler_params=_params(1),
        name="swiglu_ffn",
    )(tok, mod, g2, g3, wi, wo)


def _column_order():
    bw = BRANCH_WIDTH
    order = np.arange(PROJ_WIDTH)
    half = DA_QK_DIM // 2
    for base in (0, bw):
        for h in range(DA_HEADS):
            for p in range(2):
                for s in range(2):
                    for i in range(half):
                        order[base + h * DA_V_DIM + p * 64 + s * half + i] = (
                            base + h * DA_V_DIM + s * DA_QK_DIM + 2 * i + p)
    half = WA_HEAD_DIM // 2
    base = 4 * bw
    for j in range(WA_HEADS // 2):
        for p in range(2):
            for g in range(2):
                for i in range(half):
                    order[base + j * LANES + p * 64 + g * half + i] = (
                        base + (j + (WA_HEADS // 2) * g) * WA_HEAD_DIM + 2 * i + p)
    base = 5 * bw
    for p in range(2):
        for kv in range(WA_KV_HEADS):
            for i in range(half):
                order[base + p * 64 + kv * half + i] = base + kv * WA_HEAD_DIM + 2 * i + p
    return order


def _wa_row_order():
    order = np.arange(BRANCH_WIDTH)
    for j in range(WA_HEADS // 2):
        for g in range(2):
            for dd in range(WA_HEAD_DIM):
                order[j * LANES + g * WA_HEAD_DIM + dd] = (j + (WA_HEADS // 2) * g) * WA_HEAD_DIM + dd
    return order


def _rope_tables(n):
    rows = n // GRID_W
    row = jnp.repeat(jnp.arange(rows, dtype=F32), GRID_W)
    col = jnp.tile(jnp.arange(GRID_W, dtype=F32), rows)
    n_freq = DA_QK_DIM // 4
    inv = ROPE_THETA ** (-jnp.arange(n_freq, dtype=F32) / n_freq)
    ang = jnp.concatenate([row[:, None] * inv, col[:, None] * inv], axis=-1)
    cos = jnp.tile(jnp.cos(ang), (1, 4))
    sin = jnp.tile(jnp.sin(ang), (1, 4)) * jnp.where(jnp.arange(LANES) < 64, -1.0, 1.0)
    cos = jnp.concatenate([cos, jnp.ones((TM, LANES), F32)], axis=0)
    sin = jnp.concatenate([sin, jnp.zeros((TM, LANES), F32)], axis=0)
    return cos, sin


def kernel(x, c, ctx, c_ctx, w_ada, b_ada, norm_g, w_in, da_lambda, da_subln_g,
           pool_w, pool_scale, wa_sink, w_branch, w_out, w_ffn_in, w_ffn_out):
    nb, n, d = x.shape
    depth = w_ada.shape[0]
    assert d == D_MODEL and ctx.shape[1] == CTX_LEN
    assert n % TM == 0 and (nb * CTX_LEN) % TM == 0 and n % CK == 0 and n % GRID_W == 0
    assert n % TQ == 0 and n % (WA_STEP * WA_BLOCK) == 0
    assert TM % POOL_SUB == 0 and CTX_LEN % POOL_SUB == 0 and (nb * n) % CTX_LEN == 0
    assert WA_HEAD_DIM == DA_QK_DIM

    tok = jnp.concatenate([x.reshape(nb * n, d), ctx.reshape(nb * CTX_LEN, d)], axis=0)
    cvec = jnp.concatenate([c, c_ctx[None, :]], axis=0)
    mod = _mod_call(cvec, w_ada, b_ada).reshape(depth, nb + 1, 6, d)
    cos, sin = _rope_tables(n)
    band = jnp.asarray(_pool_band(), BF16)

    w1 = w_in[:, :, _column_order()].astype(BF16)
    wg = w_in[:, :, PROJ_WIDTH:].astype(BF16)
    wb = jnp.concatenate([w_branch[:, :2], w_branch[:, 2:, _wa_row_order()]], axis=1).astype(BF16)
    wo = w_out.astype(BF16)
    wfi = w_ffn_in.astype(BF16)
    wfo = w_ffn_out.astype(BF16)
    pw = pool_w.astype(BF16)

    n_tiles_all = tok.shape[0] // TM
    n_tiles_lat = nb * n // TM
    for l in range(depth):
        last = l == depth - 1
        lam_init = 0.8 - 0.6 * math.exp(-0.3 * l)
        g = norm_g[l].reshape(4, 1, d)
        qa, ka, va, u, qw, kw, vw = _proj_call(tok, mod[l], g[0], w1[l], cos, sin, nb, n)
        g_sub = da_subln_g[l].reshape(1, DA_V_DIM)
        ya = _da_call(qa, ka, va, da_lambda[l], g_sub, lam_init, nb, n)
        ya_ctx = ya if last else _da_ctx_call(qa, ka, va, da_lambda[l], g_sub, lam_init, nb, n)
        yc = _wa_call(qw, kw, vw, wa_sink[l], nb, n)
        yc_ctx = yc if last else _wa_ctx_call(qw, kw, vw, wa_sink[l], nb, n)
        n_tiles = n_tiles_lat if last else n_tiles_all
        tok = _merge_call(tok, mod[l], g[0], g[1], wg[l], ya, ya_ctx, yc, yc_ctx, u, band, pw[l],
                          pool_scale[l].reshape(1, BRANCH_WIDTH), wb[l], wo[l], nb, n, n_tiles)
        tok = _ffn_call(tok, mod[l], g[2], g[3], wfi[l], wfo[l], nb, n)
    return tok[:nb * n].reshape(nb, n, d)
```

```python
import functools
import math

import numpy as np
import jax
import jax.numpy as jnp
from jax import lax
from jax.experimental import pallas as pl
from jax.experimental.pallas import tpu as pltpu

F32 = jnp.float32
BF16 = jnp.bfloat16

D_MODEL = 1024
CTX_LEN = 256
GRID_W = 64
EPS = 1e-6
ROPE_THETA = 10000.0
BRANCH_WIDTH = D_MODEL // 2
N_BRANCH = 3
DA_HEADS = 4
DA_V_DIM = BRANCH_WIDTH // DA_HEADS
DA_QK_DIM = DA_V_DIM // 2
POOL_WINDOWS = (2, 4, 8, 16)
POOL_GROUPS = 4
POOL_GROUP = BRANCH_WIDTH // POOL_GROUPS
WA_HEAD_DIM = 64
WA_HEADS = BRANCH_WIDTH // WA_HEAD_DIM
WA_KV_HEADS = 2
WA_GROUP = WA_HEADS // WA_KV_HEADS
WA_WINDOW = 128
WA_BLOCK = 128
D_FF = ((8 * D_MODEL + 3 * 256 - 1) // (3 * 256)) * 256
PROJ_WIDTH = 5 * BRANCH_WIDTH + 2 * WA_KV_HEADS * WA_HEAD_DIM
GATE_WIDTH = N_BRANCH * D_MODEL

LANES = 128
BF16_SUBLANES = 16
VMEM_LIMIT = 56 * 1024 * 1024

TM = 1024
TQ = 512
CK = 256
WA_STEP = 4
POOL_SUB = 128
POOL_HALO = BF16_SUBLANES


def _params(n_axes):
    return pltpu.CompilerParams(dimension_semantics=("arbitrary",) * n_axes,
                                vmem_limit_bytes=VMEM_LIMIT)


def _const_spec(shape):
    nd = len(shape)
    return pl.BlockSpec(shape, lambda *_: (0,) * nd, pipeline_mode=pl.Buffered(1))


def _rms(x, g):
    return x * lax.rsqrt(jnp.mean(x * x, axis=-1, keepdims=True) + EPS) * g


def _modulate(x, g, shift, scale):
    return _rms(x, g) * (1.0 + scale) + shift


def _mod_kernel(c_ref, w_ref, b_ref, o_ref):
    c = c_ref[...]
    a = c * (1.0 / (1.0 + jnp.exp(-c)))
    o_ref[0] = jnp.dot(a, w_ref[0], preferred_element_type=F32,
                       precision=lax.Precision.HIGHEST) + b_ref[0]


def _mod_call(cvec, w_ada, b_ada):
    depth, d, width = w_ada.shape
    rows = cvec.shape[0]
    tn = 1536
    return pl.pallas_call(
        _mod_kernel,
        grid=(depth, width // tn),
        in_specs=[pl.BlockSpec((rows, d), lambda l, j: (0, 0)),
                  pl.BlockSpec((1, d, tn), lambda l, j: (l, 0, j)),
                  pl.BlockSpec((1, 1, tn), lambda l, j: (l, 0, j))],
        out_specs=pl.BlockSpec((1, rows, tn), lambda l, j: (l, 0, j)),
        out_shape=jax.ShapeDtypeStruct((depth, rows, width), F32),
        compiler_params=_params(2),
        name="adaln_mod",
    )(cvec, w_ada, b_ada.reshape(depth, 1, width))


def _proj_kernel(x_ref, mod_ref, g_ref, w_ref, c_ref, s_ref,
                 qa_ref, ka_ref, va_ref, u_ref, qw_ref, kw_ref, vw_ref):
    h = _modulate(x_ref[...], g_ref[...], mod_ref[0, 0:1, :], mod_ref[0, 1:2, :]).astype(BF16)
    cos = c_ref[...]
    sin = s_ref[...]

    def proj(lo, width):
        return jnp.dot(h, w_ref[:, lo:lo + width], preferred_element_type=F32)

    def rope(yj):
        return yj * cos + pltpu.roll(yj, LANES // 2, 1) * sin

    def key_store(ref, y):
        for j in range(y.shape[1] // LANES):
            ref[:, j * LANES:(j + 1) * LANES] = rope(y[:, j * LANES:(j + 1) * LANES]).astype(BF16)

    lane = lax.broadcasted_iota(jnp.int32, (TM, LANES), 1)
    first = (lane & (DA_QK_DIM // 2)) == 0

    def query_store(ref, y, scale):
        for j in range(y.shape[1] // LANES):
            r = rope(y[:, j * LANES:(j + 1) * LANES]) * scale
            ref[:, (2 * j) * LANES:(2 * j + 1) * LANES] = jnp.where(first, r, 0.0).astype(BF16)
            ref[:, (2 * j + 1) * LANES:(2 * j + 2) * LANES] = jnp.where(first, 0.0, r).astype(BF16)

    bw = BRANCH_WIDTH
    ya, yk, yv, yu, yq = (proj(j * bw, bw) for j in range(5))
    kvw = proj(5 * bw, 2 * LANES)
    query_store(qa_ref, ya, DA_QK_DIM ** -0.5 * math.log2(math.e))
    key_store(ka_ref, yk)
    va_ref[...] = yv.T.astype(BF16)
    u_ref[...] = yu.astype(BF16)
    query_store(qw_ref, yq, WA_HEAD_DIM ** -0.5 * math.log2(math.e))
    key_store(kw_ref, kvw[:, :LANES])
    vw_ref[...] = kvw[:, LANES:].astype(BF16)


def _tile_maps(nb, n, ctx):
    n_lat = nb * n // TM
    per_seq = n // TM

    def mod_map(t):
        return (jnp.where(t < n_lat, t // per_seq, nb), 0, 0)

    def rope_map(t):
        return (jnp.where(t < n_lat, t % per_seq, per_seq), 0)

    return n_lat, mod_map, rope_map


def _proj_call(tok, mod, g, w1, cos, sin, nb, n):
    nt, d = tok.shape
    _, mod_map, rope_map = _tile_maps(nb, n, CTX_LEN)
    row = lambda t: (t, 0)
    spec = lambda w: pl.BlockSpec((TM, w), row)
    sd = lambda w: jax.ShapeDtypeStruct((nt, w), BF16)
    widths = (2 * BRANCH_WIDTH, BRANCH_WIDTH, BRANCH_WIDTH, BRANCH_WIDTH, 2 * BRANCH_WIDTH, LANES, LANES)
    return pl.pallas_call(
        _proj_kernel,
        grid=(nt // TM,),
        in_specs=[pl.BlockSpec((TM, d), row),
                  pl.BlockSpec((1, 6, d), mod_map),
                  _const_spec((1, d)),
                  _const_spec(w1.shape),
                  pl.BlockSpec((TM, LANES), rope_map),
                  pl.BlockSpec((TM, LANES), rope_map)],
        out_specs=[pl.BlockSpec((BRANCH_WIDTH, TM), lambda t: (0, t)) if j == 2 else spec(w)
                   for j, w in enumerate(widths)],
        out_shape=[jax.ShapeDtypeStruct((BRANCH_WIDTH, nt), BF16) if j == 2 else sd(w)
                   for j, w in enumerate(widths)],
        compiler_params=_params(1),
        name="in_proj",
    )(tok, mod, g, w1, cos, sin)


def _da_lambda(dl_ref, lam_init):
    lp = dl_ref[...]
    return (jnp.exp(jnp.sum(lp[0:1] * lp[1:2], axis=-1, keepdims=True))
            - jnp.exp(jnp.sum(lp[2:3] * lp[3:4], axis=-1, keepdims=True)) + lam_init)


def _da_scores(k, qs):
    return lax.dot_general(k, qs, (((1,), (1,)), ((), ())), preferred_element_type=F32)


def _da_fold(x, op):
    return op(x.reshape(x.shape[0] // 8, 8, x.shape[1]), axis=0)


def _da_pv(v, e):
    return lax.dot_general(v, e.astype(BF16), (((0,), (0,)), ((), ())), preferred_element_type=F32)


def _da_pvt(vt, e):
    return jnp.dot(vt, e.astype(BF16), preferred_element_type=F32)


def _da_finish(o_ref, g_ref, lam, lam_init, l8, acc):
    tq = o_ref.shape[0]
    l = jnp.sum(l8, axis=0, keepdims=True)
    ot = acc[:, :tq] / l[:, :tq] - lam * (acc[:, tq:] / l[:, tq:])
    o_ref[...] = (_rms(ot.T, g_ref[...]) * (1.0 - lam_init)).astype(BF16)


def _da_kernel(dl_ref, g_ref, q_ref, kl_ref, kc_ref, vl_ref, vc_ref, o_ref, s_scr, m_scr, l_scr, acc_scr, *,
               lam_init):
    t = pl.program_id(0)
    lam = _da_lambda(dl_ref, lam_init)
    n_lat = kl_ref.shape[0]
    qs = jnp.concatenate([q_ref[:, :LANES], q_ref[:, LANES:]], axis=0)

    @pl.when(t == 0)
    def _():
        s_scr[1] = jnp.zeros(s_scr.shape[1:], F32)
        m_scr[1] = jnp.zeros(m_scr.shape[1:], F32)
        l_scr[...] = jnp.ones(l_scr.shape, F32)
        acc_scr[...] = jnp.zeros(acc_scr.shape, F32)

    def step(wr, rd):
        _da_finish(o_ref, g_ref, lam, lam_init, l_scr[...], acc_scr[...])
        m_prev = m_scr[rd]
        m8 = l8 = acc = None
        for c in range(n_lat // CK):
            rows = slice(c * CK, (c + 1) * CK)
            s = _da_scores(kl_ref[rows, :], qs)
            s_scr[wr, rows, :] = s
            f = _da_fold(s, jnp.max)
            m8 = f if m8 is None else jnp.maximum(m8, f)
            e = jnp.exp2(s_scr[rd, rows, :] - m_prev)
            f = _da_fold(e, jnp.sum)
            p = _da_pvt(vl_ref[:, rows], e)
            l8 = f if l8 is None else l8 + f
            acc = p if acc is None else acc + p
        rows = slice(n_lat, n_lat + CTX_LEN)
        sc = _da_scores(kc_ref[...], qs)
        s_scr[wr, rows, :] = sc
        m_scr[wr] = jnp.max(jnp.maximum(m8, _da_fold(sc, jnp.max)), axis=0, keepdims=True)
        e = jnp.exp2(s_scr[rd, rows, :] - m_prev)
        l_scr[...] = l8 + _da_fold(e, jnp.sum)
        acc_scr[...] = acc + _da_pvt(vc_ref[...], e)

    @pl.when(t % 2 == 0)
    def _():
        step(0, 1)

    @pl.when(t % 2 == 1)
    def _():
        step(1, 0)


def _da_call(qa, ka, va, da_lambda, g_sub, lam_init, nb, n):
    n_q = n // TQ
    units = nb * DA_HEADS * n_q
    ctx0 = nb * n // CTX_LEN

    def unit(u):
        u = jnp.clip(u, 0, units - 1)
        return u // (n_q * DA_HEADS), (u // n_q) % DA_HEADS, u % n_q

    def scored(t):
        return unit(t)

    def finished(t):
        return unit(t - 1)

    def written(t):
        return unit(t - 2)

    def tile_map(which):
        def m(t):
            b, h, i = which(t)
            return (b * n_q + i, h)
        return m

    def lat(which):
        return pl.BlockSpec((n, DA_V_DIM), lambda t: which(t)[:2])

    def ctx(which):
        def m(t):
            b, h, _ = which(t)
            return (ctx0 + b, h)
        return pl.BlockSpec((CTX_LEN, DA_V_DIM), m)

    def lat_t(which):
        return pl.BlockSpec((DA_V_DIM, n), lambda t: which(t)[1::-1])

    def ctx_t(which):
        def m(t):
            b, h, _ = which(t)
            return (h, ctx0 + b)
        return pl.BlockSpec((DA_V_DIM, CTX_LEN), m)

    kern = functools.partial(_da_kernel, lam_init=lam_init)
    return pl.pallas_call(
        kern,
        grid=(units + 2,),
        in_specs=[pl.BlockSpec(da_lambda.shape, lambda t: (0, 0)),
                  pl.BlockSpec((1, DA_V_DIM), lambda t: (0, 0)),
                  pl.BlockSpec((TQ, 2 * DA_V_DIM), tile_map(scored)),
                  lat(scored), ctx(scored), lat_t(finished), ctx_t(finished)],
        out_specs=pl.BlockSpec((TQ, DA_V_DIM), tile_map(written)),
        out_shape=jax.ShapeDtypeStruct((nb * n, BRANCH_WIDTH), BF16),
        scratch_shapes=[pltpu.VMEM((2, n + CTX_LEN, 2 * TQ), F32), pltpu.VMEM((2, 1, 2 * TQ), F32),
                        pltpu.VMEM((8, 2 * TQ), F32), pltpu.VMEM((DA_V_DIM, 2 * TQ), F32)],
        compiler_params=_params(1),
        name="diff_attn",
    )(da_lambda, g_sub, qa, ka, ka, va, va)


def _da_ctx_kernel(dl_ref, g_ref, q_ref, k_ref, v_ref, o_ref, *, lam_init):
    lam = _da_lambda(dl_ref, lam_init)
    heads = range(DA_HEADS)
    d = DA_V_DIM
    qs = [jnp.concatenate([q_ref[:, 2 * h * d:(2 * h + 1) * d], q_ref[:, (2 * h + 1) * d:(2 * h + 2) * d]], axis=0)
          for h in heads]
    ss = [_da_scores(k_ref[:, h * d:(h + 1) * d], qs[h]) for h in heads]
    es = [jnp.exp2(s - jnp.max(s, axis=0, keepdims=True)) for s in ss]
    accs = [_da_pvt(v_ref[h * d:(h + 1) * d, :], es[h]) for h in heads]
    for h in heads:
        _da_finish(o_ref.at[:, h * d:(h + 1) * d], g_ref, lam, lam_init, _da_fold(es[h], jnp.sum), accs[h])


def _da_ctx_call(qa, ka, va, da_lambda, g_sub, lam_init, nb, n):
    ctx0 = nb * n // CTX_LEN
    blk = lambda w: pl.BlockSpec((CTX_LEN, w), lambda b: (ctx0 + b, 0))
    kern = functools.partial(_da_ctx_kernel, lam_init=lam_init)
    return pl.pallas_call(
        kern,
        grid=(nb,),
        in_specs=[pl.BlockSpec(da_lambda.shape, lambda b: (0, 0)),
                  pl.BlockSpec((1, DA_V_DIM), lambda b: (0, 0)),
                  blk(2 * BRANCH_WIDTH), blk(BRANCH_WIDTH),
                  pl.BlockSpec((BRANCH_WIDTH, CTX_LEN), lambda b: (0, ctx0 + b))],
        out_specs=pl.BlockSpec((CTX_LEN, BRANCH_WIDTH), lambda b: (b, 0)),
        out_shape=jax.ShapeDtypeStruct((nb * CTX_LEN, BRANCH_WIDTH), BF16),
        compiler_params=_params(1),
        name="diff_attn_ctx",
    )(da_lambda, g_sub, qa, ka, va)


def _wa_pair_queries(q_ref, j, rows=slice(0, WA_BLOCK)):
    return jnp.concatenate([q_ref[rows, (2 * j) * LANES:(2 * j + 1) * LANES],
                            q_ref[rows, (2 * j + 1) * LANES:(2 * j + 2) * LANES]], axis=0)


def _wa_pair_sinks(sink_ref, j):
    lane = lax.broadcasted_iota(jnp.int32, (1, 2 * WA_BLOCK), 1)
    return jnp.where(lane < WA_BLOCK, sink_ref[j], sink_ref[j + WA_HEADS // 2]) * math.log2(math.e)


def _wa_pair_max(s, sk):
    return jnp.maximum(jnp.max(_da_fold(s, jnp.max), axis=0, keepdims=True), sk)


def _wa_pair_finish(o_ref, j, s, m, sk, v, rows=slice(0, WA_BLOCK)):
    e = jnp.exp2(s - m)
    den = jnp.sum(_da_fold(e, jnp.sum), axis=0, keepdims=True) + jnp.exp2(sk - m)
    ot = _da_pv(v, e) / den
    first = lax.broadcasted_iota(jnp.int32, (LANES, WA_BLOCK), 0) < WA_HEAD_DIM
    o_ref[rows, j * LANES:(j + 1) * LANES] = jnp.where(first, ot[:, :WA_BLOCK], ot[:, WA_BLOCK:]).T.astype(BF16)


def _wa_band_bias():
    band = 3 * WA_BLOCK
    key = np.arange(band + CTX_LEN)[:, None]
    qpos = np.arange(2 * WA_BLOCK)[None, :] % WA_BLOCK
    ok = (key >= band) | (np.abs(key - WA_BLOCK - qpos) <= WA_WINDOW)
    return np.where(ok, 0.0, -np.inf).astype(np.float32)


def _wa_kernel(sink_ref, bias_ref, q_ref, kp_ref, kc_ref, kn_ref, kx_ref, vp_ref, vc_ref, vn_ref, vx_ref,
               o_ref, s_scr, m_scr, *, n_steps, units):
    t = pl.program_id(0)
    i = lax.rem(jnp.minimum(t, units - 1), n_steps)

    @pl.when(t == 0)
    def _():
        s_scr[1] = jnp.zeros(s_scr.shape[1:], F32)
        m_scr[1] = jnp.zeros(m_scr.shape[1:], F32)

    def step(wr, rd):
        def gather(p_ref, c_ref, n_ref, x_ref):
            blocks = [p_ref[...]] + [c_ref[b * WA_BLOCK:(b + 1) * WA_BLOCK, :] for b in range(WA_STEP)] + [n_ref[...]]
            return [jnp.concatenate(blocks[b:b + 3] + [x_ref[...]], axis=0) for b in range(WA_STEP)]

        ks = gather(kp_ref, kc_ref, kn_ref, kx_ref)
        vs = gather(vp_ref, vc_ref, vn_ref, vx_ref)
        band = 3 * WA_BLOCK
        out = jnp.full((WA_BLOCK, 2 * WA_BLOCK), -jnp.inf, F32)
        for b in range(WA_STEP):
            rows = slice(b * WA_BLOCK, (b + 1) * WA_BLOCK)
            lo = bias_ref[:WA_BLOCK, :]
            hi = bias_ref[2 * WA_BLOCK:band, :]
            if b == 0:
                lo = jnp.where(i > 0, lo, out)
            if b == WA_STEP - 1:
                hi = jnp.where(i < n_steps - 1, hi, out)
            bias = jnp.concatenate([lo, bias_ref[WA_BLOCK:2 * WA_BLOCK, :], hi, bias_ref[band:, :]], axis=0)
            for j in range(WA_HEADS // 2):
                c0 = (b * (WA_HEADS // 2) + j) * 2 * WA_BLOCK
                cols = slice(c0, c0 + 2 * WA_BLOCK)
                sk = _wa_pair_sinks(sink_ref, j)
                s = _da_scores(ks[b], _wa_pair_queries(q_ref, j, rows)) + bias
                s_scr[wr, :, cols] = s
                m_scr[wr, :, cols] = _wa_pair_max(s, sk)
                _wa_pair_finish(o_ref, j, s_scr[rd, :, cols], m_scr[rd, :, cols], sk, vs[b], rows)

    @pl.when(t % 2 == 0)
    def _():
        step(0, 1)

    @pl.when(t % 2 == 1)
    def _():
        step(1, 0)


def _wa_call(qw, kw, vw, sink, nb, n):
    n_blk = n // WA_BLOCK
    n_steps = n_blk // WA_STEP
    units = nb * n_steps
    ctx0 = nb * n // CTX_LEN

    def scored(t):
        u = jnp.minimum(t, units - 1)
        return u // n_steps, u % n_steps

    def finished(t):
        u = jnp.maximum(t - 1, 0)
        return u // n_steps, u % n_steps

    def q_map(which):
        def m(t):
            b, i = which(t)
            return (b * n_steps + i, 0)
        return m

    def edge(which, delta):
        def m(t):
            b, i = which(t)
            return (b * n_blk + jnp.clip(i * WA_STEP + delta, 0, n_blk - 1), 0)
        return pl.BlockSpec((WA_BLOCK, LANES), m)

    def own(which):
        return pl.BlockSpec((WA_STEP * WA_BLOCK, LANES), q_map(which))

    def ctx(which):
        return pl.BlockSpec((CTX_LEN, LANES), lambda t: (ctx0 + which(t)[0], 0))

    bias = jnp.asarray(_wa_band_bias())
    keys = bias.shape[0]
    width = WA_STEP * WA_HEADS * WA_BLOCK
    kern = functools.partial(_wa_kernel, n_steps=n_steps, units=units)
    return pl.pallas_call(
        kern,
        grid=(units + 1,),
        in_specs=[pl.BlockSpec(memory_space=pltpu.SMEM),
                  _const_spec(bias.shape),
                  pl.BlockSpec((WA_STEP * WA_BLOCK, 2 * BRANCH_WIDTH), q_map(scored)),
                  edge(scored, -1), own(scored), edge(scored, WA_STEP), ctx(scored),
                  edge(finished, -1), own(finished), edge(finished, WA_STEP), ctx(finished)],
        out_specs=pl.BlockSpec((WA_STEP * WA_BLOCK, BRANCH_WIDTH), q_map(finished)),
        out_shape=jax.ShapeDtypeStruct((nb * n, BRANCH_WIDTH), BF16),
        scratch_shapes=[pltpu.VMEM((2, keys, width), F32), pltpu.VMEM((2, 1, width), F32)],
        compiler_params=_params(1),
        name="window_attn",
    )(sink, bias, qw, kw, kw, kw, kw, vw, vw, vw, vw)


def _wa_ctx_kernel(sink_ref, q_ref, k_ref, v_ref, o_ref):
    k = k_ref[...]
    units = [(slice(b * WA_BLOCK, (b + 1) * WA_BLOCK), j)
             for b in range(CTX_LEN // WA_BLOCK) for j in range(WA_HEADS // 2)]
    ss = [_da_scores(k, _wa_pair_queries(q_ref, j, rows)) for rows, j in units]
    for (rows, j), s in zip(units, ss):
        sk = _wa_pair_sinks(sink_ref, j)
        _wa_pair_finish(o_ref, j, s, _wa_pair_max(s, sk), sk, v_ref[...], rows)


def _wa_ctx_call(qw, kw, vw, sink, nb, n):
    ctx0 = nb * n // CTX_LEN
    rows = lambda w: pl.BlockSpec((CTX_LEN, w), lambda b: (ctx0 + b, 0))
    return pl.pallas_call(
        _wa_ctx_kernel,
        grid=(nb,),
        in_specs=[pl.BlockSpec(memory_space=pltpu.SMEM), rows(2 * BRANCH_WIDTH), rows(LANES), rows(LANES)],
        out_specs=pl.BlockSpec((CTX_LEN, BRANCH_WIDTH), lambda b: (b, 0)),
        out_shape=jax.ShapeDtypeStruct((nb * CTX_LEN, BRANCH_WIDTH), BF16),
        compiler_params=_params(1),
        name="window_attn_ctx",
    )(sink, qw, kw, vw)


def _pool_band():
    ext = POOL_SUB + 2 * POOL_HALO
    t = np.arange(POOL_SUB)[:, None]
    d = np.arange(ext)[None, :] - POOL_HALO - t
    return np.stack([((d >= -(w // 2)) & (d < w - w // 2)) for w in POOL_WINDOWS]).astype(np.float32)


def _merge_kernel(x_ref, mod_ref, g0_ref, g1_ref, wg_ref, yal_ref, yax_ref, ycl_ref, ycx_ref,
                  u_ref, up_ref, un_ref, band_ref, pw_ref, ps_ref, wb_ref, wo_ref, o_ref, *, n_lat, n):
    t = pl.program_id(0)
    x = x_ref[...]
    h = _modulate(x, g0_ref[...], mod_ref[0, 0:1, :], mod_ref[0, 1:2, :]).astype(BF16)
    is_lat = t < n_lat
    ya = jnp.where(is_lat, yal_ref[...], yax_ref[...])
    yc = jnp.where(is_lat, ycl_ref[...], ycx_ref[...])

    u = u_ref[...]
    halo_one = jnp.ones((POOL_HALO, POOL_GROUP), BF16)
    halo_zero = jnp.zeros((POOL_HALO, POOL_GROUP), BF16)
    n_sub = TM // POOL_SUB
    units = [(sb, g) for sb in range(n_sub) for g in range(POOL_GROUPS)]
    ext, ones = [], []
    for sb in range(n_sub):
        r0 = sb * POOL_SUB
        prev = up_ref[...] if sb == 0 else u[r0 - POOL_HALO:r0]
        nxt = un_ref[...] if r0 + POOL_SUB == TM else u[r0 + POOL_SUB:r0 + POOL_SUB + POOL_HALO]
        g_row = t * TM + r0
        seq_row = jnp.where(is_lat, g_row % n, g_row % CTX_LEN)
        seq_len = jnp.where(is_lat, n, CTX_LEN)
        prev_ok = seq_row != 0
        next_ok = seq_row + POOL_SUB != seq_len
        ext.append(jnp.concatenate([jnp.where(prev_ok, prev, jnp.zeros_like(prev)), u[r0:r0 + POOL_SUB],
                                    jnp.where(next_ok, nxt, jnp.zeros_like(nxt))], axis=0))
        ones.append(jnp.concatenate([jnp.where(prev_ok, halo_one, halo_zero),
                                     jnp.ones((POOL_SUB, POOL_GROUP), BF16),
                                     jnp.where(next_ok, halo_one, halo_zero)], axis=0))
    cols = lambda g: slice(g * POOL_GROUP, (g + 1) * POOL_GROUP)
    both = [jnp.dot(band_ref[g], jnp.concatenate([ext[sb][:, cols(g)], ones[sb]], axis=1),
                    preferred_element_type=F32) for sb, g in units]
    pooled = [(bt[:, :POOL_GROUP] / bt[:, POOL_GROUP:]
               - u[sb * POOL_SUB:(sb + 1) * POOL_SUB, cols(g)].astype(F32)).astype(BF16)
              for (sb, g), bt in zip(units, both)]
    outs = [jnp.dot(pq, pw_ref[g], preferred_element_type=F32) * ps_ref[:, cols(g)]
            for (sb, g), pq in zip(units, pooled)]
    yb = jnp.concatenate([jnp.concatenate(outs[sb * POOL_GROUPS:(sb + 1) * POOL_GROUPS], axis=1)
                          for sb in range(n_sub)], axis=0).astype(BF16)

    branches = (ya, yb, yc)
    m = None
    for k in range(N_BRANCH):
        gate = jnp.dot(h, wg_ref[:, k * D_MODEL:(k + 1) * D_MODEL], preferred_element_type=F32)
        gate = 1.0 / (1.0 + jnp.exp(-gate))
        term = gate * jnp.dot(branches[k], wb_ref[k], preferred_element_type=F32)
        m = term if m is None else m + term
    mix = jnp.dot(m.astype(BF16), wo_ref[...], preferred_element_type=F32)
    o_ref[...] = x + mod_ref[0, 2:3, :] * _rms(mix, g1_ref[...])


def _merge_call(tok, mod, g0, g1, wg, ya_lat, ya_ctx, yc_lat, yc_ctx, u, band, pool_w, pool_scale, wb, wo,
                nb, n, n_tiles):
    nt, d = tok.shape
    n_lat, mod_map, _ = _tile_maps(nb, n, CTX_LEN)
    row = lambda t: (t, 0)
    per_tile = TM // POOL_HALO
    last = nt // POOL_HALO - 1
    wide = pl.BlockSpec((TM, BRANCH_WIDTH), row)
    n_ctx = ya_ctx.shape[0] // TM
    lat_rows = pl.BlockSpec((TM, BRANCH_WIDTH), lambda t: (jnp.minimum(t, n_lat - 1), 0))
    ctx_rows = pl.BlockSpec((TM, BRANCH_WIDTH), lambda t: (jnp.clip(t - n_lat, 0, n_ctx - 1), 0))
    kern = functools.partial(_merge_kernel, n_lat=n_lat, n=n)
    return pl.pallas_call(
        kern,
        grid=(n_tiles,),
        in_specs=[pl.BlockSpec((TM, d), row),
                  pl.BlockSpec((1, 6, d), mod_map),
                  _const_spec((1, d)), _const_spec((1, d)),
                  _const_spec(wg.shape),
                  lat_rows, ctx_rows, lat_rows, ctx_rows, wide,
                  pl.BlockSpec((POOL_HALO, BRANCH_WIDTH), lambda t: (jnp.maximum(t * per_tile - 1, 0), 0)),
                  pl.BlockSpec((POOL_HALO, BRANCH_WIDTH), lambda t: (jnp.minimum((t + 1) * per_tile, last), 0)),
                  _const_spec(band.shape), _const_spec(pool_w.shape), _const_spec((1, BRANCH_WIDTH)),
                  _const_spec(wb.shape), _const_spec(wo.shape)],
        out_specs=pl.BlockSpec((TM, d), row),
        out_shape=jax.ShapeDtypeStruct((n_tiles * TM, d), F32),
        compiler_params=_params(1),
        name="pool_merge",
    )(tok, mod, g0, g1, wg, ya_lat, ya_ctx, yc_lat, yc_ctx, u, u, u, band, pool_w, pool_scale, wb, wo)


def _ffn_kernel(x_ref, mod_ref, g2_ref, g3_ref, wi_ref, wo_ref, o_ref):
    x = x_ref[...]
    h = _modulate(x, g2_ref[...], mod_ref[0, 3:4, :], mod_ref[0, 4:5, :]).astype(BF16)
    step = 256

    def gate_up(c):
        return (jnp.dot(h, wi_ref[:, c * step:(c + 1) * step], preferred_element_type=F32),
                jnp.dot(h, wi_ref[:, D_FF + c * step:D_FF + (c + 1) * step], preferred_element_type=F32))

    acc = None
    nxt = gate_up(0)
    for c in range(D_FF // step):
        gate, up = nxt
        if c + 1 < D_FF // step:
            nxt = gate_up(c + 1)
        act = (gate * (1.0 / (1.0 + jnp.exp(-gate))) * up).astype(BF16)
        part = jnp.dot(act, wo_ref[c * step:(c + 1) * step, :], preferred_element_type=F32)
        acc = part if acc is None else acc + part
    o_ref[...] = x + mod_ref[0, 5:6, :] * _rms(acc, g3_ref[...])


def _ffn_call(tok, mod, g2, g3, wi, wo, nb, n):
    nt, d = tok.shape
    _, mod_map, _ = _tile_maps(nb, n, CTX_LEN)
    row = lambda t: (t, 0)
    return pl.pallas_call(
        _ffn_kernel,
        grid=(nt // TM,),
        in_specs=[pl.BlockSpec((TM, d), row),
                  pl.BlockSpec((1, 6, d), mod_map),
                  _const_spec((1, d)), _const_spec((1, d)),
                  _const_spec(wi.shape), _const_spec(wo.shape)],
        out_specs=pl.BlockSpec((TM, d), row),
        out_shape=jax.ShapeDtypeStruct((nt, d), F32),
        compiler_params=_params(1),
        name="swiglu_ffn",
    )(tok, mod, g2, g3, wi, wo)


def _permute_proj_columns(w):
    bw = BRANCH_WIDTH
    half = DA_QK_DIM // 2
    lead = w.shape[:-1]
    n = len(lead)
    keep = tuple(range(n))

    def diff_attn(x):
        x = x.reshape(lead + (DA_HEADS, 2, half, 2))
        return x.transpose(keep + (n, n + 3, n + 1, n + 2)).reshape(lead + (bw,))

    def win_queries(x):
        x = x.reshape(lead + (2, WA_HEADS // 2, half, 2))
        return x.transpose(keep + (n + 1, n + 3, n, n + 2)).reshape(lead + (bw,))

    def win_keys(x):
        x = x.reshape(lead + (WA_KV_HEADS, half, 2))
        return x.transpose(keep + (n + 2, n, n + 1)).reshape(lead + (LANES,))

    return jnp.concatenate([diff_attn(w[..., :bw]), diff_attn(w[..., bw:2 * bw]), w[..., 2 * bw:4 * bw],
                            win_queries(w[..., 4 * bw:5 * bw]), win_keys(w[..., 5 * bw:5 * bw + LANES]),
                            w[..., 5 * bw + LANES:PROJ_WIDTH]], axis=-1)


def _permute_wa_rows(w):
    lead = w.shape[:-2]
    n = len(lead)
    x = w.reshape(lead + (2, WA_HEADS // 2, WA_HEAD_DIM, w.shape[-1]))
    return x.transpose(tuple(range(n)) + (n + 1, n, n + 2, n + 3)).reshape(w.shape)


def _rope_tables(n):
    rows = n // GRID_W
    row = jnp.repeat(jnp.arange(rows, dtype=F32), GRID_W)
    col = jnp.tile(jnp.arange(GRID_W, dtype=F32), rows)
    n_freq = DA_QK_DIM // 4
    inv = ROPE_THETA ** (-jnp.arange(n_freq, dtype=F32) / n_freq)
    ang = jnp.concatenate([row[:, None] * inv, col[:, None] * inv], axis=-1)
    cos = jnp.tile(jnp.cos(ang), (1, 4))
    sin = jnp.tile(jnp.sin(ang), (1, 4)) * jnp.where(jnp.arange(LANES) < 64, -1.0, 1.0)
    cos = jnp.concatenate([cos, jnp.ones((TM, LANES), F32)], axis=0)
    sin = jnp.concatenate([sin, jnp.zeros((TM, LANES), F32)], axis=0)
    return cos, sin


def kernel(x, c, ctx, c_ctx, w_ada, b_ada, norm_g, w_in, da_lambda, da_subln_g,
           pool_w, pool_scale, wa_sink, w_branch, w_out, w_ffn_in, w_ffn_out):
    nb, n, d = x.shape
    depth = w_ada.shape[0]
    assert d == D_MODEL and ctx.shape[1] == CTX_LEN
    assert n % TM == 0 and (nb * CTX_LEN) % TM == 0 and n % CK == 0 and n % GRID_W == 0
    assert n % TQ == 0 and n % (WA_STEP * WA_BLOCK) == 0
    assert TM % POOL_SUB == 0 and CTX_LEN % POOL_SUB == 0 and (nb * n) % CTX_LEN == 0
    assert WA_HEAD_DIM == DA_QK_DIM

    tok = jnp.concatenate([x.reshape(nb * n, d), ctx.reshape(nb * CTX_LEN, d)], axis=0)
    cvec = jnp.concatenate([c, c_ctx[None, :]], axis=0)
    mod = _mod_call(cvec, w_ada, b_ada).reshape(depth, nb + 1, 6, d)
    cos, sin = _rope_tables(n)
    band = jnp.asarray(_pool_band(), BF16)

    w1 = _permute_proj_columns(w_in[:, :, :PROJ_WIDTH]).astype(BF16)
    wg = w_in[:, :, PROJ_WIDTH:].astype(BF16)
    wb = jnp.concatenate([w_branch[:, :2], _permute_wa_rows(w_branch[:, 2:])], axis=1).astype(BF16)
    wo = w_out.astype(BF16)
    wfi = w_ffn_in.astype(BF16)
    wfo = w_ffn_out.astype(BF16)
    pw = pool_w.astype(BF16)

    n_tiles_all = tok.shape[0] // TM
    n_tiles_lat = nb * n // TM
    for l in range(depth):
        last = l == depth - 1
        lam_init = 0.8 - 0.6 * math.exp(-0.3 * l)
        g = norm_g[l].reshape(4, 1, d)
        qa, ka, va, u, qw, kw, vw = _proj_call(tok, mod[l], g[0], w1[l], cos, sin, nb, n)
        g_sub = da_subln_g[l].reshape(1, DA_V_DIM)
        ya = _da_call(qa, ka, va, da_lambda[l], g_sub, lam_init, nb, n)
        ya_ctx = ya if last else _da_ctx_call(qa, ka, va, da_lambda[l], g_sub, lam_init, nb, n)
        yc = _wa_call(qw, kw, vw, wa_sink[l], nb, n)
        yc_ctx = yc if last else _wa_ctx_call(qw, kw, vw, wa_sink[l], nb, n)
        n_tiles = n_tiles_lat if last else n_tiles_all
        tok = _merge_call(tok, mod[l], g[0], g[1], wg[l], ya, ya_ctx, yc, yc_ctx, u, band, pw[l],
                          pool_scale[l].reshape(1, BRANCH_WIDTH), wb[l], wo[l], nb, n, n_tiles)
        tok = _ffn_call(tok, mod[l], g[2], g[3], wfi[l], wfo[l], nb, n)
    return tok[:nb * n].reshape(nb, n, d)
```

```python
import functools
import math

import numpy as np
import jax
import jax.numpy as jnp
from jax import lax
from jax.experimental import pallas as pl
from jax.experimental.pallas import tpu as pltpu

F32 = jnp.float32
BF16 = jnp.bfloat16

D_MODEL = 1024
CTX_LEN = 256
GRID_W = 64
EPS = 1e-6
ROPE_THETA = 10000.0
BRANCH_WIDTH = D_MODEL // 2
N_BRANCH = 3
DA_HEADS = 4
DA_V_DIM = BRANCH_WIDTH // DA_HEADS
DA_QK_DIM = DA_V_DIM // 2
POOL_WINDOWS = (2, 4, 8, 16)
POOL_GROUPS = 4
POOL_GROUP = BRANCH_WIDTH // POOL_GROUPS
WA_HEAD_DIM = 64
WA_HEADS = BRANCH_WIDTH // WA_HEAD_DIM
WA_KV_HEADS = 2
WA_GROUP = WA_HEADS // WA_KV_HEADS
WA_WINDOW = 128
WA_BLOCK = 128
D_FF = ((8 * D_MODEL + 3 * 256 - 1) // (3 * 256)) * 256
PROJ_WIDTH = 5 * BRANCH_WIDTH + 2 * WA_KV_HEADS * WA_HEAD_DIM
GATE_WIDTH = N_BRANCH * D_MODEL

LANES = 128
BF16_SUBLANES = 16
VMEM_LIMIT = 56 * 1024 * 1024

TM = 1024
ROW_BLOCK = 256
MERGE_ROWS = 256
TQ = 512
CK = 256
WA_STEP = 4
POOL_SUB = 128
POOL_HALO = BF16_SUBLANES


def _params(n_axes):
    return pltpu.CompilerParams(dimension_semantics=("arbitrary",) * n_axes,
                                vmem_limit_bytes=VMEM_LIMIT)


def _const_spec(shape):
    nd = len(shape)
    return pl.BlockSpec(shape, lambda *_: (0,) * nd, pipeline_mode=pl.Buffered(1))


def _rms(x, g):
    return x * lax.rsqrt(jnp.mean(x * x, axis=-1, keepdims=True) + EPS) * g


def _modulate(x, g, shift, scale):
    return _rms(x, g) * (1.0 + scale) + shift


def _mod_kernel(c_ref, w_ref, b_ref, o_ref):
    c = c_ref[...]
    a = c * (1.0 / (1.0 + jnp.exp(-c)))
    o_ref[0] = jnp.dot(a, w_ref[0], preferred_element_type=F32,
                       precision=lax.Precision.HIGHEST) + b_ref[0]


def _mod_call(cvec, w_ada, b_ada):
    depth, d, width = w_ada.shape
    rows = cvec.shape[0]
    tn = 1536
    return pl.pallas_call(
        _mod_kernel,
        grid=(depth, width // tn),
        in_specs=[pl.BlockSpec((rows, d), lambda l, j: (0, 0)),
                  pl.BlockSpec((1, d, tn), lambda l, j: (l, 0, j)),
                  pl.BlockSpec((1, 1, tn), lambda l, j: (l, 0, j))],
        out_specs=pl.BlockSpec((1, rows, tn), lambda l, j: (l, 0, j)),
        out_shape=jax.ShapeDtypeStruct((depth, rows, width), F32),
        compiler_params=_params(2),
        name="adaln_mod",
    )(cvec, w_ada, b_ada.reshape(depth, 1, width))


def _proj_kernel(x_ref, mod_ref, g_ref, w_ref, c_ref, s_ref,
                 qa_ref, ka_ref, va_ref, u_ref, qw_ref, kw_ref, vw_ref):
    lane = lax.broadcasted_iota(jnp.int32, (ROW_BLOCK, LANES), 1)
    first = (lane & (DA_QK_DIM // 2)) == 0
    bw = BRANCH_WIDTH

    for rb in range(TM // ROW_BLOCK):
        rows = slice(rb * ROW_BLOCK, (rb + 1) * ROW_BLOCK)
        h = _modulate(x_ref[rows, :], g_ref[...], mod_ref[0, 0:1, :], mod_ref[0, 1:2, :]).astype(BF16)
        cos = c_ref[rows, :]
        sin = s_ref[rows, :]

        def proj(lo, width):
            return jnp.dot(h, w_ref[:, lo:lo + width], preferred_element_type=F32)

        def rope(yj):
            return yj * cos + pltpu.roll(yj, LANES // 2, 1) * sin

        def key_store(ref, y):
            for j in range(y.shape[1] // LANES):
                ref[rows, j * LANES:(j + 1) * LANES] = rope(y[:, j * LANES:(j + 1) * LANES]).astype(BF16)

        def query_store(ref, y, scale):
            for j in range(y.shape[1] // LANES):
                r = rope(y[:, j * LANES:(j + 1) * LANES]) * scale
                ref[rows, (2 * j) * LANES:(2 * j + 1) * LANES] = jnp.where(first, r, 0.0).astype(BF16)
                ref[rows, (2 * j + 1) * LANES:(2 * j + 2) * LANES] = jnp.where(first, 0.0, r).astype(BF16)

        ya, yk, yv, yu, yq = (proj(j * bw, bw) for j in range(5))
        kvw = proj(5 * bw, 2 * LANES)
        query_store(qa_ref, ya, DA_QK_DIM ** -0.5 * math.log2(math.e))
        key_store(ka_ref, yk)
        va_ref[:, rows] = yv.T.astype(BF16)
        u_ref[rows, :] = yu.astype(BF16)
        query_store(qw_ref, yq, WA_HEAD_DIM ** -0.5 * math.log2(math.e))
        key_store(kw_ref, kvw[:, :LANES])
        vw_ref[rows, :] = kvw[:, LANES:].astype(BF16)


def _tile_maps(nb, n, ctx):
    n_lat = nb * n // TM
    per_seq = n // TM

    def mod_map(t):
        return (jnp.where(t < n_lat, t // per_seq, nb), 0, 0)

    def rope_map(t):
        return (jnp.where(t < n_lat, t % per_seq, per_seq), 0)

    return n_lat, mod_map, rope_map


def _proj_call(tok, mod, g, w1, cos, sin, nb, n):
    nt, d = tok.shape
    _, mod_map, rope_map = _tile_maps(nb, n, CTX_LEN)
    row = lambda t: (t, 0)
    spec = lambda w: pl.BlockSpec((TM, w), row)
    sd = lambda w: jax.ShapeDtypeStruct((nt, w), BF16)
    widths = (2 * BRANCH_WIDTH, BRANCH_WIDTH, BRANCH_WIDTH, BRANCH_WIDTH, 2 * BRANCH_WIDTH, LANES, LANES)
    return pl.pallas_call(
        _proj_kernel,
        grid=(nt // TM,),
        in_specs=[pl.BlockSpec((TM, d), row),
                  pl.BlockSpec((1, 6, d), mod_map),
                  _const_spec((1, d)),
                  _const_spec(w1.shape),
                  pl.BlockSpec((TM, LANES), rope_map),
                  pl.BlockSpec((TM, LANES), rope_map)],
        out_specs=[pl.BlockSpec((BRANCH_WIDTH, TM), lambda t: (0, t)) if j == 2 else spec(w)
                   for j, w in enumerate(widths)],
        out_shape=[jax.ShapeDtypeStruct((BRANCH_WIDTH, nt), BF16) if j == 2 else sd(w)
                   for j, w in enumerate(widths)],
        compiler_params=_params(1),
        name="in_proj",
    )(tok, mod, g, w1, cos, sin)


def _da_lambda(dl_ref, lam_init):
    lp = dl_ref[...]
    return (jnp.exp(jnp.sum(lp[0:1] * lp[1:2], axis=-1, keepdims=True))
            - jnp.exp(jnp.sum(lp[2:3] * lp[3:4], axis=-1, keepdims=True)) + lam_init)


def _da_scores(k, qs):
    return lax.dot_general(k, qs, (((1,), (1,)), ((), ())), preferred_element_type=F32)


def _da_fold(x, op):
    return op(x.reshape(x.shape[0] // 8, 8, x.shape[1]), axis=0)


def _da_pv(v, e):
    return lax.dot_general(v, e.astype(BF16), (((0,), (0,)), ((), ())), preferred_element_type=F32)


def _da_pvt(vt, e):
    return jnp.dot(vt, e.astype(BF16), preferred_element_type=F32)


def _da_finish(o_ref, g_ref, lam, lam_init, l8, acc):
    tq = o_ref.shape[0]
    l = jnp.sum(l8, axis=0, keepdims=True)
    ot = acc[:, :tq] / l[:, :tq] - lam * (acc[:, tq:] / l[:, tq:])
    o_ref[...] = (_rms(ot.T, g_ref[...]) * (1.0 - lam_init)).astype(BF16)


def _da_kernel(dl_ref, g_ref, q_ref, kl_ref, kc_ref, vl_ref, vc_ref, o_ref, s_scr, m_scr, l_scr, acc_scr, *,
               lam_init):
    t = pl.program_id(0)
    lam = _da_lambda(dl_ref, lam_init)
    n_lat = kl_ref.shape[0]
    qs = jnp.concatenate([q_ref[:, :LANES], q_ref[:, LANES:]], axis=0)

    @pl.when(t == 0)
    def _():
        s_scr[1] = jnp.zeros(s_scr.shape[1:], F32)
        m_scr[1] = jnp.zeros(m_scr.shape[1:], F32)
        l_scr[...] = jnp.ones(l_scr.shape, F32)
        acc_scr[...] = jnp.zeros(acc_scr.shape, F32)

    def step(wr, rd):
        _da_finish(o_ref, g_ref, lam, lam_init, l_scr[...], acc_scr[...])
        m_prev = m_scr[rd]
        m8 = l8 = acc = None
        for c in range(n_lat // CK):
            rows = slice(c * CK, (c + 1) * CK)
            s = _da_scores(kl_ref[rows, :], qs)
            s_scr[wr, rows, :] = s
            f = _da_fold(s, jnp.max)
            m8 = f if m8 is None else jnp.maximum(m8, f)
            e = jnp.exp2(s_scr[rd, rows, :] - m_prev)
            f = _da_fold(e, jnp.sum)
            p = _da_pvt(vl_ref[:, rows], e)
            l8 = f if l8 is None else l8 + f
            acc = p if acc is None else acc + p
        rows = slice(n_lat, n_lat + CTX_LEN)
        sc = _da_scores(kc_ref[...], qs)
        s_scr[wr, rows, :] = sc
        m_scr[wr] = jnp.max(jnp.maximum(m8, _da_fold(sc, jnp.max)), axis=0, keepdims=True)
        e = jnp.exp2(s_scr[rd, rows, :] - m_prev)
        l_scr[...] = l8 + _da_fold(e, jnp.sum)
        acc_scr[...] = acc + _da_pvt(vc_ref[...], e)

    @pl.when(t % 2 == 0)
    def _():
        step(0, 1)

    @pl.when(t % 2 == 1)
    def _():
        step(1, 0)


def _da_call(qa, ka, va, da_lambda, g_sub, lam_init, nb, n):
    n_q = n // TQ
    units = nb * DA_HEADS * n_q
    ctx0 = nb * n // CTX_LEN

    def unit(u):
        u = jnp.clip(u, 0, units - 1)
        return u // (n_q * DA_HEADS), (u // n_q) % DA_HEADS, u % n_q

    def scored(t):
        return unit(t)

    def finished(t):
        return unit(t - 1)

    def written(t):
        return unit(t - 2)

    def tile_map(which):
        def m(t):
            b, h, i = which(t)
            return (b * n_q + i, h)
        return m

    def lat(which):
        return pl.BlockSpec((n, DA_V_DIM), lambda t: which(t)[:2])

    def ctx(which):
        def m(t):
            b, h, _ = which(t)
            return (ctx0 + b, h)
        return pl.BlockSpec((CTX_LEN, DA_V_DIM), m)

    def lat_t(which):
        return pl.BlockSpec((DA_V_DIM, n), lambda t: which(t)[1::-1])

    def ctx_t(which):
        def m(t):
            b, h, _ = which(t)
            return (h, ctx0 + b)
        return pl.BlockSpec((DA_V_DIM, CTX_LEN), m)

    kern = functools.partial(_da_kernel, lam_init=lam_init)
    return pl.pallas_call(
        kern,
        grid=(units + 2,),
        in_specs=[pl.BlockSpec(da_lambda.shape, lambda t: (0, 0)),
                  pl.BlockSpec((1, DA_V_DIM), lambda t: (0, 0)),
                  pl.BlockSpec((TQ, 2 * DA_V_DIM), tile_map(scored)),
                  lat(scored), ctx(scored), lat_t(finished), ctx_t(finished)],
        out_specs=pl.BlockSpec((TQ, DA_V_DIM), tile_map(written)),
        out_shape=jax.ShapeDtypeStruct((nb * n, BRANCH_WIDTH), BF16),
        scratch_shapes=[pltpu.VMEM((2, n + CTX_LEN, 2 * TQ), F32), pltpu.VMEM((2, 1, 2 * TQ), F32),
                        pltpu.VMEM((8, 2 * TQ), F32), pltpu.VMEM((DA_V_DIM, 2 * TQ), F32)],
        compiler_params=_params(1),
        name="diff_attn",
    )(da_lambda, g_sub, qa, ka, ka, va, va)


def _da_ctx_kernel(dl_ref, g_ref, q_ref, k_ref, v_ref, o_ref, *, lam_init):
    lam = _da_lambda(dl_ref, lam_init)
    heads = range(DA_HEADS)
    d = DA_V_DIM
    qs = [jnp.concatenate([q_ref[:, 2 * h * d:(2 * h + 1) * d], q_ref[:, (2 * h + 1) * d:(2 * h + 2) * d]], axis=0)
          for h in heads]
    ss = [_da_scores(k_ref[:, h * d:(h + 1) * d], qs[h]) for h in heads]
    es = [jnp.exp2(s - jnp.max(s, axis=0, keepdims=True)) for s in ss]
    accs = [_da_pvt(v_ref[h * d:(h + 1) * d, :], es[h]) for h in heads]
    for h in heads:
        _da_finish(o_ref.at[:, h * d:(h + 1) * d], g_ref, lam, lam_init, _da_fold(es[h], jnp.sum), accs[h])


def _da_ctx_call(qa, ka, va, da_lambda, g_sub, lam_init, nb, n):
    ctx0 = nb * n // CTX_LEN
    blk = lambda w: pl.BlockSpec((CTX_LEN, w), lambda b: (ctx0 + b, 0))
    kern = functools.partial(_da_ctx_kernel, lam_init=lam_init)
    return pl.pallas_call(
        kern,
        grid=(nb,),
        in_specs=[pl.BlockSpec(da_lambda.shape, lambda b: (0, 0)),
                  pl.BlockSpec((1, DA_V_DIM), lambda b: (0, 0)),
                  blk(2 * BRANCH_WIDTH), blk(BRANCH_WIDTH),
                  pl.BlockSpec((BRANCH_WIDTH, CTX_LEN), lambda b: (0, ctx0 + b))],
        out_specs=pl.BlockSpec((CTX_LEN, BRANCH_WIDTH), lambda b: (b, 0)),
        out_shape=jax.ShapeDtypeStruct((nb * CTX_LEN, BRANCH_WIDTH), BF16),
        compiler_params=_params(1),
        name="diff_attn_ctx",
    )(da_lambda, g_sub, qa, ka, va)


def _wa_pair_queries(q_ref, j, rows=slice(0, WA_BLOCK)):
    return jnp.concatenate([q_ref[rows, (2 * j) * LANES:(2 * j + 1) * LANES],
                            q_ref[rows, (2 * j + 1) * LANES:(2 * j + 2) * LANES]], axis=0)


def _wa_pair_sinks(sink_ref, j):
    lane = lax.broadcasted_iota(jnp.int32, (1, 2 * WA_BLOCK), 1)
    return jnp.where(lane < WA_BLOCK, sink_ref[j], sink_ref[j + WA_HEADS // 2]) * math.log2(math.e)


def _wa_pair_max(s, sk):
    return jnp.maximum(jnp.max(_da_fold(s, jnp.max), axis=0, keepdims=True), sk)


def _wa_pair_finish(o_ref, j, s, m, sk, v, rows=slice(0, WA_BLOCK)):
    e = jnp.exp2(s - m)
    den = jnp.sum(_da_fold(e, jnp.sum), axis=0, keepdims=True) + jnp.exp2(sk - m)
    ot = _da_pv(v, e) / den
    first = lax.broadcasted_iota(jnp.int32, (LANES, WA_BLOCK), 0) < WA_HEAD_DIM
    o_ref[rows, j * LANES:(j + 1) * LANES] = jnp.where(first, ot[:, :WA_BLOCK], ot[:, WA_BLOCK:]).T.astype(BF16)


def _wa_band_bias():
    band = 3 * WA_BLOCK
    key = np.arange(band + CTX_LEN)[:, None]
    qpos = np.arange(2 * WA_BLOCK)[None, :] % WA_BLOCK
    ok = (key >= band) | (np.abs(key - WA_BLOCK - qpos) <= WA_WINDOW)
    return np.where(ok, 0.0, -np.inf).astype(np.float32)


def _wa_kernel(sink_ref, bias_ref, q_ref, kp_ref, kc_ref, kn_ref, kx_ref, vp_ref, vc_ref, vn_ref, vx_ref,
               o_ref, s_scr, m_scr, *, n_steps, units):
    t = pl.program_id(0)
    i = lax.rem(jnp.minimum(t, units - 1), n_steps)

    @pl.when(t == 0)
    def _():
        s_scr[1] = jnp.zeros(s_scr.shape[1:], F32)
        m_scr[1] = jnp.zeros(m_scr.shape[1:], F32)

    def step(wr, rd):
        def gather(p_ref, c_ref, n_ref, x_ref):
            blocks = [p_ref[...]] + [c_ref[b * WA_BLOCK:(b + 1) * WA_BLOCK, :] for b in range(WA_STEP)] + [n_ref[...]]
            return [jnp.concatenate(blocks[b:b + 3] + [x_ref[...]], axis=0) for b in range(WA_STEP)]

        ks = gather(kp_ref, kc_ref, kn_ref, kx_ref)
        vs = gather(vp_ref, vc_ref, vn_ref, vx_ref)
        band = 3 * WA_BLOCK
        out = jnp.full((WA_BLOCK, 2 * WA_BLOCK), -jnp.inf, F32)
        for b in range(WA_STEP):
            rows = slice(b * WA_BLOCK, (b + 1) * WA_BLOCK)
            lo = bias_ref[:WA_BLOCK, :]
            hi = bias_ref[2 * WA_BLOCK:band, :]
            if b == 0:
                lo = jnp.where(i > 0, lo, out)
            if b == WA_STEP - 1:
                hi = jnp.where(i < n_steps - 1, hi, out)
            bias = jnp.concatenate([lo, bias_ref[WA_BLOCK:2 * WA_BLOCK, :], hi, bias_ref[band:, :]], axis=0)
            for j in range(WA_HEADS // 2):
                c0 = (b * (WA_HEADS // 2) + j) * 2 * WA_BLOCK
                cols = slice(c0, c0 + 2 * WA_BLOCK)
                sk = _wa_pair_sinks(sink_ref, j)
                s = _da_scores(ks[b], _wa_pair_queries(q_ref, j, rows)) + bias
                s_scr[wr, :, cols] = s
                m_scr[wr, :, cols] = _wa_pair_max(s, sk)
                _wa_pair_finish(o_ref, j, s_scr[rd, :, cols], m_scr[rd, :, cols], sk, vs[b], rows)

    @pl.when(t % 2 == 0)
    def _():
        step(0, 1)

    @pl.when(t % 2 == 1)
    def _():
        step(1, 0)


def _wa_call(qw, kw, vw, sink, nb, n):
    n_blk = n // WA_BLOCK
    n_steps = n_blk // WA_STEP
    units = nb * n_steps
    ctx0 = nb * n // CTX_LEN

    def scored(t):
        u = jnp.minimum(t, units - 1)
        return u // n_steps, u % n_steps

    def finished(t):
        u = jnp.maximum(t - 1, 0)
        return u // n_steps, u % n_steps

    def q_map(which):
        def m(t):
            b, i = which(t)
            return (b * n_steps + i, 0)
        return m

    def edge(which, delta):
        def m(t):
            b, i = which(t)
            return (b * n_blk + jnp.clip(i * WA_STEP + delta, 0, n_blk - 1), 0)
        return pl.BlockSpec((WA_BLOCK, LANES), m)

    def own(which):
        return pl.BlockSpec((WA_STEP * WA_BLOCK, LANES), q_map(which))

    def ctx(which):
        return pl.BlockSpec((CTX_LEN, LANES), lambda t: (ctx0 + which(t)[0], 0))

    bias = jnp.asarray(_wa_band_bias())
    keys = bias.shape[0]
    width = WA_STEP * WA_HEADS * WA_BLOCK
    kern = functools.partial(_wa_kernel, n_steps=n_steps, units=units)
    return pl.pallas_call(
        kern,
        grid=(units + 1,),
        in_specs=[pl.BlockSpec(memory_space=pltpu.SMEM),
                  _const_spec(bias.shape),
                  pl.BlockSpec((WA_STEP * WA_BLOCK, 2 * BRANCH_WIDTH), q_map(scored)),
                  edge(scored, -1), own(scored), edge(scored, WA_STEP), ctx(scored),
                  edge(finished, -1), own(finished), edge(finished, WA_STEP), ctx(finished)],
        out_specs=pl.BlockSpec((WA_STEP * WA_BLOCK, BRANCH_WIDTH), q_map(finished)),
        out_shape=jax.ShapeDtypeStruct((nb * n, BRANCH_WIDTH), BF16),
        scratch_shapes=[pltpu.VMEM((2, keys, width), F32), pltpu.VMEM((2, 1, width), F32)],
        compiler_params=_params(1),
        name="window_attn",
    )(sink, bias, qw, kw, kw, kw, kw, vw, vw, vw, vw)


def _wa_ctx_kernel(sink_ref, q_ref, k_ref, v_ref, o_ref):
    k = k_ref[...]
    units = [(slice(b * WA_BLOCK, (b + 1) * WA_BLOCK), j)
             for b in range(CTX_LEN // WA_BLOCK) for j in range(WA_HEADS // 2)]
    ss = [_da_scores(k, _wa_pair_queries(q_ref, j, rows)) for rows, j in units]
    for (rows, j), s in zip(units, ss):
        sk = _wa_pair_sinks(sink_ref, j)
        _wa_pair_finish(o_ref, j, s, _wa_pair_max(s, sk), sk, v_ref[...], rows)


def _wa_ctx_call(qw, kw, vw, sink, nb, n):
    ctx0 = nb * n // CTX_LEN
    rows = lambda w: pl.BlockSpec((CTX_LEN, w), lambda b: (ctx0 + b, 0))
    return pl.pallas_call(
        _wa_ctx_kernel,
        grid=(nb,),
        in_specs=[pl.BlockSpec(memory_space=pltpu.SMEM), rows(2 * BRANCH_WIDTH), rows(LANES), rows(LANES)],
        out_specs=pl.BlockSpec((CTX_LEN, BRANCH_WIDTH), lambda b: (b, 0)),
        out_shape=jax.ShapeDtypeStruct((nb * CTX_LEN, BRANCH_WIDTH), BF16),
        compiler_params=_params(1),
        name="window_attn_ctx",
    )(sink, qw, kw, vw)


def _pool_band():
    ext = POOL_SUB + 2 * POOL_HALO
    t = np.arange(POOL_SUB)[:, None]
    d = np.arange(ext)[None, :] - POOL_HALO - t
    return np.stack([((d >= -(w // 2)) & (d < w - w // 2)) for w in POOL_WINDOWS]).astype(np.float32)


def _merge_kernel(x_ref, mod_ref, g0_ref, g1_ref, wg_ref, yal_ref, yax_ref, ycl_ref, ycx_ref,
                  u_ref, up_ref, un_ref, band_ref, pw_ref, ps_ref, wb_ref, wo_ref, o_ref, *, n_lat, n):
    t = pl.program_id(0)
    is_lat = t < n_lat
    ya = jnp.where(is_lat, yal_ref[...], yax_ref[...])
    yc = jnp.where(is_lat, ycl_ref[...], ycx_ref[...])

    u = u_ref[...]
    halo_one = jnp.ones((POOL_HALO, POOL_GROUP), BF16)
    halo_zero = jnp.zeros((POOL_HALO, POOL_GROUP), BF16)
    n_sub = TM // POOL_SUB
    units = [(sb, g) for sb in range(n_sub) for g in range(POOL_GROUPS)]
    ext, ones = [], []
    for sb in range(n_sub):
        r0 = sb * POOL_SUB
        prev = up_ref[...] if sb == 0 else u[r0 - POOL_HALO:r0]
        nxt = un_ref[...] if r0 + POOL_SUB == TM else u[r0 + POOL_SUB:r0 + POOL_SUB + POOL_HALO]
        g_row = t * TM + r0
        seq_row = jnp.where(is_lat, g_row % n, g_row % CTX_LEN)
        seq_len = jnp.where(is_lat, n, CTX_LEN)
        prev_ok = seq_row != 0
        next_ok = seq_row + POOL_SUB != seq_len
        ext.append(jnp.concatenate([jnp.where(prev_ok, prev, jnp.zeros_like(prev)), u[r0:r0 + POOL_SUB],
                                    jnp.where(next_ok, nxt, jnp.zeros_like(nxt))], axis=0))
        ones.append(jnp.concatenate([jnp.where(prev_ok, halo_one, halo_zero),
                                     jnp.ones((POOL_SUB, POOL_GROUP), BF16),
                                     jnp.where(next_ok, halo_one, halo_zero)], axis=0))
    cols = lambda g: slice(g * POOL_GROUP, (g + 1) * POOL_GROUP)
    both = [jnp.dot(band_ref[g], jnp.concatenate([ext[sb][:, cols(g)], ones[sb]], axis=1),
                    preferred_element_type=F32) for sb, g in units]
    pooled = [(bt[:, :POOL_GROUP] / bt[:, POOL_GROUP:]
               - u[sb * POOL_SUB:(sb + 1) * POOL_SUB, cols(g)].astype(F32)).astype(BF16)
              for (sb, g), bt in zip(units, both)]
    outs = [jnp.dot(pq, pw_ref[g], preferred_element_type=F32) * ps_ref[:, cols(g)]
            for (sb, g), pq in zip(units, pooled)]
    yb = jnp.concatenate([jnp.concatenate(outs[sb * POOL_GROUPS:(sb + 1) * POOL_GROUPS], axis=1)
                          for sb in range(n_sub)], axis=0).astype(BF16)

    branches = (ya, yb, yc)
    for rb in range(TM // MERGE_ROWS):
        rows = slice(rb * MERGE_ROWS, (rb + 1) * MERGE_ROWS)
        x = x_ref[rows, :]
        h = _modulate(x, g0_ref[...], mod_ref[0, 0:1, :], mod_ref[0, 1:2, :]).astype(BF16)
        m = None
        for k in range(N_BRANCH):
            gate = jnp.dot(h, wg_ref[:, k * D_MODEL:(k + 1) * D_MODEL], preferred_element_type=F32)
            gate = 1.0 / (1.0 + jnp.exp(-gate))
            term = gate * jnp.dot(branches[k][rows], wb_ref[k], preferred_element_type=F32)
            m = term if m is None else m + term
        mix = jnp.dot(m.astype(BF16), wo_ref[...], preferred_element_type=F32)
        o_ref[rows, :] = x + mod_ref[0, 2:3, :] * _rms(mix, g1_ref[...])


def _merge_call(tok, mod, g0, g1, wg, ya_lat, ya_ctx, yc_lat, yc_ctx, u, band, pool_w, pool_scale, wb, wo,
                nb, n, n_tiles):
    nt, d = tok.shape
    n_lat, mod_map, _ = _tile_maps(nb, n, CTX_LEN)
    row = lambda t: (t, 0)
    per_tile = TM // POOL_HALO
    last = nt // POOL_HALO - 1
    wide = pl.BlockSpec((TM, BRANCH_WIDTH), row)
    n_ctx = ya_ctx.shape[0] // TM
    lat_rows = pl.BlockSpec((TM, BRANCH_WIDTH), lambda t: (jnp.minimum(t, n_lat - 1), 0))
    ctx_rows = pl.BlockSpec((TM, BRANCH_WIDTH), lambda t: (jnp.clip(t - n_lat, 0, n_ctx - 1), 0))
    kern = functools.partial(_merge_kernel, n_lat=n_lat, n=n)
    return pl.pallas_call(
        kern,
        grid=(n_tiles,),
        in_specs=[pl.BlockSpec((TM, d), row),
                  pl.BlockSpec((1, 6, d), mod_map),
                  _const_spec((1, d)), _const_spec((1, d)),
                  _const_spec(wg.shape),
                  lat_rows, ctx_rows, lat_rows, ctx_rows, wide,
                  pl.BlockSpec((POOL_HALO, BRANCH_WIDTH), lambda t: (jnp.maximum(t * per_tile - 1, 0), 0)),
                  pl.BlockSpec((POOL_HALO, BRANCH_WIDTH), lambda t: (jnp.minimum((t + 1) * per_tile, last), 0)),
                  _const_spec(band.shape), _const_spec(pool_w.shape), _const_spec((1, BRANCH_WIDTH)),
                  _const_spec(wb.shape), _const_spec(wo.shape)],
        out_specs=pl.BlockSpec((TM, d), row),
        out_shape=jax.ShapeDtypeStruct((n_tiles * TM, d), F32),
        compiler_params=_params(1),
        name="pool_merge",
    )(tok, mod, g0, g1, wg, ya_lat, ya_ctx, yc_lat, yc_ctx, u, u, u, band, pool_w, pool_scale, wb, wo)


def _ffn_kernel(x_ref, mod_ref, g2_ref, g3_ref, wi_ref, wo_ref, o_ref):
    x = x_ref[...]
    h = _modulate(x, g2_ref[...], mod_ref[0, 3:4, :], mod_ref[0, 4:5, :]).astype(BF16)
    step = 256

    def gate_up(c):
        return (jnp.dot(h, wi_ref[:, c * step:(c + 1) * step], preferred_element_type=F32),
                jnp.dot(h, wi_ref[:, D_FF + c * step:D_FF + (c + 1) * step], preferred_element_type=F32))

    acc = None
    nxt = gate_up(0)
    for c in range(D_FF // step):
        gate, up = nxt
        if c + 1 < D_FF // step:
            nxt = gate_up(c + 1)
        act = (gate * (1.0 / (1.0 + jnp.exp(-gate))) * up).astype(BF16)
        part = jnp.dot(act, wo_ref[c * step:(c + 1) * step, :], preferred_element_type=F32)
        acc = part if acc is None else acc + part
    o_ref[...] = x + mod_ref[0, 5:6, :] * _rms(acc, g3_ref[...])


def _ffn_call(tok, mod, g2, g3, wi, wo, nb, n):
    nt, d = tok.shape
    _, mod_map, _ = _tile_maps(nb, n, CTX_LEN)
    row = lambda t: (t, 0)
    return pl.pallas_call(
        _ffn_kernel,
        grid=(nt // TM,),
        in_specs=[pl.BlockSpec((TM, d), row),
                  pl.BlockSpec((1, 6, d), mod_map),
                  _const_spec((1, d)), _const_spec((1, d)),
                  _const_spec(wi.shape), _const_spec(wo.shape)],
        out_specs=pl.BlockSpec((TM, d), row),
        out_shape=jax.ShapeDtypeStruct((nt, d), F32),
        compiler_params=_params(1),
        name="swiglu_ffn",
    )(tok, mod, g2, g3, wi, wo)


def _permute_proj_columns(w):
    bw = BRANCH_WIDTH
    half = DA_QK_DIM // 2
    lead = w.shape[:-1]
    n = len(lead)
    keep = tuple(range(n))

    def diff_attn(x):
        x = x.reshape(lead + (DA_HEADS, 2, half, 2))
        return x.transpose(keep + (n, n + 3, n + 1, n + 2)).reshape(lead + (bw,))

    def win_queries(x):
        x = x.reshape(lead + (2, WA_HEADS // 2, half, 2))
        return x.transpose(keep + (n + 1, n + 3, n, n + 2)).reshape(lead + (bw,))

    def win_keys(x):
        x = x.reshape(lead + (WA_KV_HEADS, half, 2))
        return x.transpose(keep + (n + 2, n, n + 1)).reshape(lead + (LANES,))

    return jnp.concatenate([diff_attn(w[..., :bw]), diff_attn(w[..., bw:2 * bw]), w[..., 2 * bw:4 * bw],
                            win_queries(w[..., 4 * bw:5 * bw]), win_keys(w[..., 5 * bw:5 * bw + LANES]),
                            w[..., 5 * bw + LANES:PROJ_WIDTH]], axis=-1)


def _permute_wa_rows(w):
    lead = w.shape[:-2]
    n = len(lead)
    x = w.reshape(lead + (2, WA_HEADS // 2, WA_HEAD_DIM, w.shape[-1]))
    return x.transpose(tuple(range(n)) + (n + 1, n, n + 2, n + 3)).reshape(w.shape)


def _rope_tables(n):
    rows = n // GRID_W
    row = jnp.repeat(jnp.arange(rows, dtype=F32), GRID_W)
    col = jnp.tile(jnp.arange(GRID_W, dtype=F32), rows)
    n_freq = DA_QK_DIM // 4
    inv = ROPE_THETA ** (-jnp.arange(n_freq, dtype=F32) / n_freq)
    ang = jnp.concatenate([row[:, None] * inv, col[:, None] * inv], axis=-1)
    cos = jnp.tile(jnp.cos(ang), (1, 4))
    sin = jnp.tile(jnp.sin(ang), (1, 4)) * jnp.where(jnp.arange(LANES) < 64, -1.0, 1.0)
    cos = jnp.concatenate([cos, jnp.ones((TM, LANES), F32)], axis=0)
    sin = jnp.concatenate([sin, jnp.zeros((TM, LANES), F32)], axis=0)
    return cos, sin


def kernel(x, c, ctx, c_ctx, w_ada, b_ada, norm_g, w_in, da_lambda, da_subln_g,
           pool_w, pool_scale, wa_sink, w_branch, w_out, w_ffn_in, w_ffn_out):
    nb, n, d = x.shape
    depth = w_ada.shape[0]
    assert d == D_MODEL and ctx.shape[1] == CTX_LEN
    assert n % TM == 0 and (nb * CTX_LEN) % TM == 0 and n % CK == 0 and n % GRID_W == 0
    assert n % TQ == 0 and n % (WA_STEP * WA_BLOCK) == 0
    assert TM % POOL_SUB == 0 and CTX_LEN % POOL_SUB == 0 and (nb * n) % CTX_LEN == 0
    assert WA_HEAD_DIM == DA_QK_DIM

    tok = jnp.concatenate([x.reshape(nb * n, d), ctx.reshape(nb * CTX_LEN, d)], axis=0)
    cvec = jnp.concatenate([c, c_ctx[None, :]], axis=0)
    mod = _mod_call(cvec, w_ada, b_ada).reshape(depth, nb + 1, 6, d)
    cos, sin = _rope_tables(n)
    band = jnp.asarray(_pool_band(), BF16)

    w1 = _permute_proj_columns(w_in[:, :, :PROJ_WIDTH]).astype(BF16)
    wg = w_in[:, :, PROJ_WIDTH:].astype(BF16)
    wb = jnp.concatenate([w_branch[:, :2], _permute_wa_rows(w_branch[:, 2:])], axis=1).astype(BF16)
    wo = w_out.astype(BF16)
    wfi = w_ffn_in.astype(BF16)
    wfo = w_ffn_out.astype(BF16)
    pw = pool_w.astype(BF16)

    n_tiles_all = tok.shape[0] // TM
    n_tiles_lat = nb * n // TM
    for l in range(depth):
        last = l == depth - 1
        lam_init = 0.8 - 0.6 * math.exp(-0.3 * l)
        g = norm_g[l].reshape(4, 1, d)
        qa, ka, va, u, qw, kw, vw = _proj_call(tok, mod[l], g[0], w1[l], cos, sin, nb, n)
        g_sub = da_subln_g[l].reshape(1, DA_V_DIM)
        ya = _da_call(qa, ka, va, da_lambda[l], g_sub, lam_init, nb, n)
        ya_ctx = ya if last else _da_ctx_call(qa, ka, va, da_lambda[l], g_sub, lam_init, nb, n)
        yc = _wa_call(qw, kw, vw, wa_sink[l], nb, n)
        yc_ctx = yc if last else _wa_ctx_call(qw, kw, vw, wa_sink[l], nb, n)
        n_tiles = n_tiles_lat if last else n_tiles_all
        tok = _merge_call(tok, mod[l], g[0], g[1], wg[l], ya, ya_ctx, yc, yc_ctx, u, band, pw[l],
                          pool_scale[l].reshape(1, BRANCH_WIDTH), wb[l], wo[l], nb, n, n_tiles)
        tok = _ffn_call(tok, mod[l], g[2], g[3], wfi[l], wfo[l], nb, n)
    return tok[:nb * n].reshape(nb, n, d)
```

```python
import functools
import math

import numpy as np
import jax
import jax.numpy as jnp
from jax import lax
from jax.experimental import pallas as pl
from jax.experimental.pallas import tpu as pltpu

F32 = jnp.float32
BF16 = jnp.bfloat16

D_MODEL = 1024
CTX_LEN = 256
GRID_W = 64
EPS = 1e-6
ROPE_THETA = 10000.0
BRANCH_WIDTH = D_MODEL // 2
N_BRANCH = 3
DA_HEADS = 4
DA_V_DIM = BRANCH_WIDTH // DA_HEADS
DA_QK_DIM = DA_V_DIM // 2
POOL_WINDOWS = (2, 4, 8, 16)
POOL_GROUPS = 4
POOL_GROUP = BRANCH_WIDTH // POOL_GROUPS
WA_HEAD_DIM = 64
WA_HEADS = BRANCH_WIDTH // WA_HEAD_DIM
WA_KV_HEADS = 2
WA_GROUP = WA_HEADS // WA_KV_HEADS
WA_WINDOW = 128
WA_BLOCK = 128
D_FF = ((8 * D_MODEL + 3 * 256 - 1) // (3 * 256)) * 256
PROJ_WIDTH = 5 * BRANCH_WIDTH + 2 * WA_KV_HEADS * WA_HEAD_DIM
GATE_WIDTH = N_BRANCH * D_MODEL

LANES = 128
BF16_SUBLANES = 16
VMEM_LIMIT = 56 * 1024 * 1024

TM = 1024
ROW_BLOCK = 256
FFN_ROWS = 256
TQ = 512
CK = 256
WA_STEP = 4
POOL_SUB = 128
POOL_HALO = BF16_SUBLANES


def _params(n_axes):
    return pltpu.CompilerParams(dimension_semantics=("arbitrary",) * n_axes,
                                vmem_limit_bytes=VMEM_LIMIT)


def _const_spec(shape):
    nd = len(shape)
    return pl.BlockSpec(shape, lambda *_: (0,) * nd, pipeline_mode=pl.Buffered(1))


def _rms(x, g):
    return x * lax.rsqrt(jnp.mean(x * x, axis=-1, keepdims=True) + EPS) * g


def _modulate(x, g, shift, scale):
    return _rms(x, g) * (1.0 + scale) + shift


def _mod_kernel(c_ref, w_ref, b_ref, o_ref):
    c = c_ref[...]
    a = c * (1.0 / (1.0 + jnp.exp(-c)))
    o_ref[0] = jnp.dot(a, w_ref[0], preferred_element_type=F32,
                       precision=lax.Precision.HIGHEST) + b_ref[0]


def _mod_call(cvec, w_ada, b_ada):
    depth, d, width = w_ada.shape
    rows = cvec.shape[0]
    tn = 1536
    return pl.pallas_call(
        _mod_kernel,
        grid=(depth, width // tn),
        in_specs=[pl.BlockSpec((rows, d), lambda l, j: (0, 0)),
                  pl.BlockSpec((1, d, tn), lambda l, j: (l, 0, j)),
                  pl.BlockSpec((1, 1, tn), lambda l, j: (l, 0, j))],
        out_specs=pl.BlockSpec((1, rows, tn), lambda l, j: (l, 0, j)),
        out_shape=jax.ShapeDtypeStruct((depth, rows, width), F32),
        compiler_params=_params(2),
        name="adaln_mod",
    )(cvec, w_ada, b_ada.reshape(depth, 1, width))


def _proj_kernel(x_ref, mod_ref, g_ref, w_ref, c_ref, s_ref,
                 qa_ref, ka_ref, va_ref, u_ref, qw_ref, kw_ref, vw_ref):
    lane = lax.broadcasted_iota(jnp.int32, (ROW_BLOCK, LANES), 1)
    first = (lane & (DA_QK_DIM // 2)) == 0
    bw = BRANCH_WIDTH

    for rb in range(TM // ROW_BLOCK):
        rows = slice(rb * ROW_BLOCK, (rb + 1) * ROW_BLOCK)
        h = _modulate(x_ref[rows, :], g_ref[...], mod_ref[0, 0:1, :], mod_ref[0, 1:2, :]).astype(BF16)
        cos = c_ref[rows, :]
        sin = s_ref[rows, :]

        def proj(lo, width):
            return jnp.dot(h, w_ref[:, lo:lo + width], preferred_element_type=F32)

        def rope(yj):
            return yj * cos + pltpu.roll(yj, LANES // 2, 1) * sin

        def key_store(ref, y):
            for j in range(y.shape[1] // LANES):
                ref[rows, j * LANES:(j + 1) * LANES] = rope(y[:, j * LANES:(j + 1) * LANES]).astype(BF16)

        def query_store(ref, y, scale):
            for j in range(y.shape[1] // LANES):
                r = rope(y[:, j * LANES:(j + 1) * LANES]) * scale
                ref[rows, (2 * j) * LANES:(2 * j + 1) * LANES] = jnp.where(first, r, 0.0).astype(BF16)
                ref[rows, (2 * j + 1) * LANES:(2 * j + 2) * LANES] = jnp.where(first, 0.0, r).astype(BF16)

        ya, yk, yv, yu, yq = (proj(j * bw, bw) for j in range(5))
        kvw = proj(5 * bw, 2 * LANES)
        query_store(qa_ref, ya, DA_QK_DIM ** -0.5 * math.log2(math.e))
        key_store(ka_ref, yk)
        va_ref[:, rows] = yv.T.astype(BF16)
        u_ref[rows, :] = yu.astype(BF16)
        query_store(qw_ref, yq, WA_HEAD_DIM ** -0.5 * math.log2(math.e))
        key_store(kw_ref, kvw[:, :LANES])
        vw_ref[rows, :] = kvw[:, LANES:].astype(BF16)


def _tile_maps(nb, n, ctx):
    n_lat = nb * n // TM
    per_seq = n // TM

    def mod_map(t):
        return (jnp.where(t < n_lat, t // per_seq, nb), 0, 0)

    def rope_map(t):
        return (jnp.where(t < n_lat, t % per_seq, per_seq), 0)

    return n_lat, mod_map, rope_map


def _proj_call(tok, mod, g, w1, cos, sin, nb, n):
    nt, d = tok.shape
    _, mod_map, rope_map = _tile_maps(nb, n, CTX_LEN)
    row = lambda t: (t, 0)
    spec = lambda w: pl.BlockSpec((TM, w), row)
    sd = lambda w: jax.ShapeDtypeStruct((nt, w), BF16)
    widths = (2 * BRANCH_WIDTH, BRANCH_WIDTH, BRANCH_WIDTH, BRANCH_WIDTH, 2 * BRANCH_WIDTH, LANES, LANES)
    return pl.pallas_call(
        _proj_kernel,
        grid=(nt // TM,),
        in_specs=[pl.BlockSpec((TM, d), row),
                  pl.BlockSpec((1, 6, d), mod_map),
                  _const_spec((1, d)),
                  _const_spec(w1.shape),
                  pl.BlockSpec((TM, LANES), rope_map),
                  pl.BlockSpec((TM, LANES), rope_map)],
        out_specs=[pl.BlockSpec((BRANCH_WIDTH, TM), lambda t: (0, t)) if j == 2 else spec(w)
                   for j, w in enumerate(widths)],
        out_shape=[jax.ShapeDtypeStruct((BRANCH_WIDTH, nt), BF16) if j == 2 else sd(w)
                   for j, w in enumerate(widths)],
        compiler_params=_params(1),
        name="in_proj",
    )(tok, mod, g, w1, cos, sin)


def _da_lambda(dl_ref, lam_init):
    lp = dl_ref[...]
    return (jnp.exp(jnp.sum(lp[0:1] * lp[1:2], axis=-1, keepdims=True))
            - jnp.exp(jnp.sum(lp[2:3] * lp[3:4], axis=-1, keepdims=True)) + lam_init)


def _da_scores(k, qs):
    return lax.dot_general(k, qs, (((1,), (1,)), ((), ())), preferred_element_type=F32)


def _da_fold(x, op):
    return op(x.reshape(x.shape[0] // 8, 8, x.shape[1]), axis=0)


def _da_pv(v, e):
    return lax.dot_general(v, e.astype(BF16), (((0,), (0,)), ((), ())), preferred_element_type=F32)


def _da_pvt(vt, e):
    return jnp.dot(vt, e.astype(BF16), preferred_element_type=F32)


def _da_finish(o_ref, g_ref, lam, lam_init, l8, acc):
    tq = o_ref.shape[0]
    l = jnp.sum(l8, axis=0, keepdims=True)
    ot = acc[:, :tq] / l[:, :tq] - lam * (acc[:, tq:] / l[:, tq:])
    o_ref[...] = (_rms(ot.T, g_ref[...]) * (1.0 - lam_init)).astype(BF16)


def _da_kernel(dl_ref, g_ref, q_ref, kl_ref, kc_ref, vl_ref, vc_ref, o_ref, s_scr, m_scr, l_scr, acc_scr, *,
               lam_init):
    t = pl.program_id(0)
    lam = _da_lambda(dl_ref, lam_init)
    n_lat = kl_ref.shape[0]
    qs = jnp.concatenate([q_ref[:, :LANES], q_ref[:, LANES:]], axis=0)

    @pl.when(t == 0)
    def _():
        s_scr[1] = jnp.zeros(s_scr.shape[1:], F32)
        m_scr[1] = jnp.zeros(m_scr.shape[1:], F32)
        l_scr[...] = jnp.ones(l_scr.shape, F32)
        acc_scr[...] = jnp.zeros(acc_scr.shape, F32)

    def step(wr, rd):
        _da_finish(o_ref, g_ref, lam, lam_init, l_scr[...], acc_scr[...])
        m_prev = m_scr[rd]
        m8 = l8 = acc = None
        for c in range(n_lat // CK):
            rows = slice(c * CK, (c + 1) * CK)
            s = _da_scores(kl_ref[rows, :], qs)
            s_scr[wr, rows, :] = s
            f = _da_fold(s, jnp.max)
            m8 = f if m8 is None else jnp.maximum(m8, f)
            e = jnp.exp2(s_scr[rd, rows, :] - m_prev)
            f = _da_fold(e, jnp.sum)
            p = _da_pvt(vl_ref[:, rows], e)
            l8 = f if l8 is None else l8 + f
            acc = p if acc is None else acc + p
        rows = slice(n_lat, n_lat + CTX_LEN)
        sc = _da_scores(kc_ref[...], qs)
        s_scr[wr, rows, :] = sc
        m_scr[wr] = jnp.max(jnp.maximum(m8, _da_fold(sc, jnp.max)), axis=0, keepdims=True)
        e = jnp.exp2(s_scr[rd, rows, :] - m_prev)
        l_scr[...] = l8 + _da_fold(e, jnp.sum)
        acc_scr[...] = acc + _da_pvt(vc_ref[...], e)

    @pl.when(t % 2 == 0)
    def _():
        step(0, 1)

    @pl.when(t % 2 == 1)
    def _():
        step(1, 0)


def _da_call(qa, ka, va, da_lambda, g_sub, lam_init, nb, n):
    n_q = n // TQ
    units = nb * DA_HEADS * n_q
    ctx0 = nb * n // CTX_LEN

    def unit(u):
        u = jnp.clip(u, 0, units - 1)
        return u // (n_q * DA_HEADS), (u // n_q) % DA_HEADS, u % n_q

    def scored(t):
        return unit(t)

    def finished(t):
        return unit(t - 1)

    def written(t):
        return unit(t - 2)

    def tile_map(which):
        def m(t):
            b, h, i = which(t)
            return (b * n_q + i, h)
        return m

    def lat(which):
        return pl.BlockSpec((n, DA_V_DIM), lambda t: which(t)[:2])

    def ctx(which):
        def m(t):
            b, h, _ = which(t)
            return (ctx0 + b, h)
        return pl.BlockSpec((CTX_LEN, DA_V_DIM), m)

    def lat_t(which):
        return pl.BlockSpec((DA_V_DIM, n), lambda t: which(t)[1::-1])

    def ctx_t(which):
        def m(t):
            b, h, _ = which(t)
            return (h, ctx0 + b)
        return pl.BlockSpec((DA_V_DIM, CTX_LEN), m)

    kern = functools.partial(_da_kernel, lam_init=lam_init)
    return pl.pallas_call(
        kern,
        grid=(units + 2,),
        in_specs=[pl.BlockSpec(da_lambda.shape, lambda t: (0, 0)),
                  pl.BlockSpec((1, DA_V_DIM), lambda t: (0, 0)),
                  pl.BlockSpec((TQ, 2 * DA_V_DIM), tile_map(scored)),
                  lat(scored), ctx(scored), lat_t(finished), ctx_t(finished)],
        out_specs=pl.BlockSpec((TQ, DA_V_DIM), tile_map(written)),
        out_shape=jax.ShapeDtypeStruct((nb * n, BRANCH_WIDTH), BF16),
        scratch_shapes=[pltpu.VMEM((2, n + CTX_LEN, 2 * TQ), F32), pltpu.VMEM((2, 1, 2 * TQ), F32),
                        pltpu.VMEM((8, 2 * TQ), F32), pltpu.VMEM((DA_V_DIM, 2 * TQ), F32)],
        compiler_params=_params(1),
        name="diff_attn",
    )(da_lambda, g_sub, qa, ka, ka, va, va)


def _da_ctx_kernel(dl_ref, g_ref, q_ref, k_ref, v_ref, o_ref, *, lam_init):
    lam = _da_lambda(dl_ref, lam_init)
    heads = range(DA_HEADS)
    d = DA_V_DIM
    qs = [jnp.concatenate([q_ref[:, 2 * h * d:(2 * h + 1) * d], q_ref[:, (2 * h + 1) * d:(2 * h + 2) * d]], axis=0)
          for h in heads]
    ss = [_da_scores(k_ref[:, h * d:(h + 1) * d], qs[h]) for h in heads]
    es = [jnp.exp2(s - jnp.max(s, axis=0, keepdims=True)) for s in ss]
    accs = [_da_pvt(v_ref[h * d:(h + 1) * d, :], es[h]) for h in heads]
    for h in heads:
        _da_finish(o_ref.at[:, h * d:(h + 1) * d], g_ref, lam, lam_init, _da_fold(es[h], jnp.sum), accs[h])


def _da_ctx_call(qa, ka, va, da_lambda, g_sub, lam_init, nb, n):
    ctx0 = nb * n // CTX_LEN
    blk = lambda w: pl.BlockSpec((CTX_LEN, w), lambda b: (ctx0 + b, 0))
    kern = functools.partial(_da_ctx_kernel, lam_init=lam_init)
    return pl.pallas_call(
        kern,
        grid=(nb,),
        in_specs=[pl.BlockSpec(da_lambda.shape, lambda b: (0, 0)),
                  pl.BlockSpec((1, DA_V_DIM), lambda b: (0, 0)),
                  blk(2 * BRANCH_WIDTH), blk(BRANCH_WIDTH),
                  pl.BlockSpec((BRANCH_WIDTH, CTX_LEN), lambda b: (0, ctx0 + b))],
        out_specs=pl.BlockSpec((CTX_LEN, BRANCH_WIDTH), lambda b: (b, 0)),
        out_shape=jax.ShapeDtypeStruct((nb * CTX_LEN, BRANCH_WIDTH), BF16),
        compiler_params=_params(1),
        name="diff_attn_ctx",
    )(da_lambda, g_sub, qa, ka, va)


def _wa_pair_queries(q_ref, j, rows=slice(0, WA_BLOCK)):
    return jnp.concatenate([q_ref[rows, (2 * j) * LANES:(2 * j + 1) * LANES],
                            q_ref[rows, (2 * j + 1) * LANES:(2 * j + 2) * LANES]], axis=0)


def _wa_pair_sinks(sink_ref, j):
    lane = lax.broadcasted_iota(jnp.int32, (1, 2 * WA_BLOCK), 1)
    return jnp.where(lane < WA_BLOCK, sink_ref[j], sink_ref[j + WA_HEADS // 2]) * math.log2(math.e)


def _wa_pair_max(s, sk):
    return jnp.maximum(jnp.max(_da_fold(s, jnp.max), axis=0, keepdims=True), sk)


def _wa_pair_finish(o_ref, j, s, m, sk, v, rows=slice(0, WA_BLOCK)):
    e = jnp.exp2(s - m)
    den = jnp.sum(_da_fold(e, jnp.sum), axis=0, keepdims=True) + jnp.exp2(sk - m)
    ot = _da_pv(v, e) / den
    first = lax.broadcasted_iota(jnp.int32, (LANES, WA_BLOCK), 0) < WA_HEAD_DIM
    o_ref[rows, j * LANES:(j + 1) * LANES] = jnp.where(first, ot[:, :WA_BLOCK], ot[:, WA_BLOCK:]).T.astype(BF16)


def _wa_band_bias():
    band = 3 * WA_BLOCK
    key = np.arange(band + CTX_LEN)[:, None]
    qpos = np.arange(2 * WA_BLOCK)[None, :] % WA_BLOCK
    ok = (key >= band) | (np.abs(key - WA_BLOCK - qpos) <= WA_WINDOW)
    return np.where(ok, 0.0, -np.inf).astype(np.float32)


def _wa_kernel(sink_ref, bias_ref, q_ref, kp_ref, kc_ref, kn_ref, kx_ref, vp_ref, vc_ref, vn_ref, vx_ref,
               o_ref, s_scr, m_scr, *, n_steps, units):
    t = pl.program_id(0)
    i = lax.rem(jnp.minimum(t, units - 1), n_steps)

    @pl.when(t == 0)
    def _():
        s_scr[1] = jnp.zeros(s_scr.shape[1:], F32)
        m_scr[1] = jnp.zeros(m_scr.shape[1:], F32)

    def step(wr, rd):
        def gather(p_ref, c_ref, n_ref, x_ref):
            blocks = [p_ref[...]] + [c_ref[b * WA_BLOCK:(b + 1) * WA_BLOCK, :] for b in range(WA_STEP)] + [n_ref[...]]
            return [jnp.concatenate(blocks[b:b + 3] + [x_ref[...]], axis=0) for b in range(WA_STEP)]

        ks = gather(kp_ref, kc_ref, kn_ref, kx_ref)
        vs = gather(vp_ref, vc_ref, vn_ref, vx_ref)
        band = 3 * WA_BLOCK
        out = jnp.full((WA_BLOCK, 2 * WA_BLOCK), -jnp.inf, F32)
        for b in range(WA_STEP):
            rows = slice(b * WA_BLOCK, (b + 1) * WA_BLOCK)
            lo = bias_ref[:WA_BLOCK, :]
            hi = bias_ref[2 * WA_BLOCK:band, :]
            if b == 0:
                lo = jnp.where(i > 0, lo, out)
            if b == WA_STEP - 1:
                hi = jnp.where(i < n_steps - 1, hi, out)
            bias = jnp.concatenate([lo, bias_ref[WA_BLOCK:2 * WA_BLOCK, :], hi, bias_ref[band:, :]], axis=0)
            for j in range(WA_HEADS // 2):
                c0 = (b * (WA_HEADS // 2) + j) * 2 * WA_BLOCK
                cols = slice(c0, c0 + 2 * WA_BLOCK)
                sk = _wa_pair_sinks(sink_ref, j)
                s = _da_scores(ks[b], _wa_pair_queries(q_ref, j, rows)) + bias
                s_scr[wr, :, cols] = s
                m_scr[wr, :, cols] = _wa_pair_max(s, sk)
                _wa_pair_finish(o_ref, j, s_scr[rd, :, cols], m_scr[rd, :, cols], sk, vs[b], rows)

    @pl.when(t % 2 == 0)
    def _():
        step(0, 1)

    @pl.when(t % 2 == 1)
    def _():
        step(1, 0)


def _wa_call(qw, kw, vw, sink, nb, n):
    n_blk = n // WA_BLOCK
    n_steps = n_blk // WA_STEP
    units = nb * n_steps
    ctx0 = nb * n // CTX_LEN

    def scored(t):
        u = jnp.minimum(t, units - 1)
        return u // n_steps, u % n_steps

    def finished(t):
        u = jnp.maximum(t - 1, 0)
        return u // n_steps, u % n_steps

    def q_map(which):
        def m(t):
            b, i = which(t)
            return (b * n_steps + i, 0)
        return m

    def edge(which, delta):
        def m(t):
            b, i = which(t)
            return (b * n_blk + jnp.clip(i * WA_STEP + delta, 0, n_blk - 1), 0)
        return pl.BlockSpec((WA_BLOCK, LANES), m)

    def own(which):
        return pl.BlockSpec((WA_STEP * WA_BLOCK, LANES), q_map(which))

    def ctx(which):
        return pl.BlockSpec((CTX_LEN, LANES), lambda t: (ctx0 + which(t)[0], 0))

    bias = jnp.asarray(_wa_band_bias())
    keys = bias.shape[0]
    width = WA_STEP * WA_HEADS * WA_BLOCK
    kern = functools.partial(_wa_kernel, n_steps=n_steps, units=units)
    return pl.pallas_call(
        kern,
        grid=(units + 1,),
        in_specs=[pl.BlockSpec(memory_space=pltpu.SMEM),
                  _const_spec(bias.shape),
                  pl.BlockSpec((WA_STEP * WA_BLOCK, 2 * BRANCH_WIDTH), q_map(scored)),
                  edge(scored, -1), own(scored), edge(scored, WA_STEP), ctx(scored),
                  edge(finished, -1), own(finished), edge(finished, WA_STEP), ctx(finished)],
        out_specs=pl.BlockSpec((WA_STEP * WA_BLOCK, BRANCH_WIDTH), q_map(finished)),
        out_shape=jax.ShapeDtypeStruct((nb * n, BRANCH_WIDTH), BF16),
        scratch_shapes=[pltpu.VMEM((2, keys, width), F32), pltpu.VMEM((2, 1, width), F32)],
        compiler_params=_params(1),
        name="window_attn",
    )(sink, bias, qw, kw, kw, kw, kw, vw, vw, vw, vw)


def _wa_ctx_kernel(sink_ref, q_ref, k_ref, v_ref, o_ref):
    k = k_ref[...]
    units = [(slice(b * WA_BLOCK, (b + 1) * WA_BLOCK), j)
             for b in range(CTX_LEN // WA_BLOCK) for j in range(WA_HEADS // 2)]
    ss = [_da_scores(k, _wa_pair_queries(q_ref, j, rows)) for rows, j in units]
    for (rows, j), s in zip(units, ss):
        sk = _wa_pair_sinks(sink_ref, j)
        _wa_pair_finish(o_ref, j, s, _wa_pair_max(s, sk), sk, v_ref[...], rows)


def _wa_ctx_call(qw, kw, vw, sink, nb, n):
    ctx0 = nb * n // CTX_LEN
    rows = lambda w: pl.BlockSpec((CTX_LEN, w), lambda b: (ctx0 + b, 0))
    return pl.pallas_call(
        _wa_ctx_kernel,
        grid=(nb,),
        in_specs=[pl.BlockSpec(memory_space=pltpu.SMEM), rows(2 * BRANCH_WIDTH), rows(LANES), rows(LANES)],
        out_specs=pl.BlockSpec((CTX_LEN, BRANCH_WIDTH), lambda b: (b, 0)),
        out_shape=jax.ShapeDtypeStruct((nb * CTX_LEN, BRANCH_WIDTH), BF16),
        compiler_params=_params(1),
        name="window_attn_ctx",
    )(sink, qw, kw, vw)


def _pool_band():
    ext = POOL_SUB + 2 * POOL_HALO
    t = np.arange(POOL_SUB)[:, None]
    d = np.arange(ext)[None, :] - POOL_HALO - t
    return np.stack([((d >= -(w // 2)) & (d < w - w // 2)) for w in POOL_WINDOWS]).astype(np.float32)


def _merge_kernel(x_ref, mod_ref, g0_ref, g1_ref, wg_ref, yal_ref, yax_ref, ycl_ref, ycx_ref,
                  u_ref, up_ref, un_ref, band_ref, pw_ref, ps_ref, wb_ref, wo_ref, o_ref, *, n_lat, n):
    t = pl.program_id(0)
    is_lat = t < n_lat
    ya = jnp.where(is_lat, yal_ref[...], yax_ref[...])
    yc = jnp.where(is_lat, ycl_ref[...], ycx_ref[...])

    u = u_ref[...]
    halo_one = jnp.ones((POOL_HALO, POOL_GROUP), BF16)
    halo_zero = jnp.zeros((POOL_HALO, POOL_GROUP), BF16)
    n_sub = TM // POOL_SUB
    units = [(sb, g) for sb in range(n_sub) for g in range(POOL_GROUPS)]
    ext, ones = [], []
    for sb in range(n_sub):
        r0 = sb * POOL_SUB
        prev = up_ref[...] if sb == 0 else u[r0 - POOL_HALO:r0]
        nxt = un_ref[...] if r0 + POOL_SUB == TM else u[r0 + POOL_SUB:r0 + POOL_SUB + POOL_HALO]
        g_row = t * TM + r0
        seq_row = jnp.where(is_lat, g_row % n, g_row % CTX_LEN)
        seq_len = jnp.where(is_lat, n, CTX_LEN)
        prev_ok = seq_row != 0
        next_ok = seq_row + POOL_SUB != seq_len
        ext.append(jnp.concatenate([jnp.where(prev_ok, prev, jnp.zeros_like(prev)), u[r0:r0 + POOL_SUB],
                                    jnp.where(next_ok, nxt, jnp.zeros_like(nxt))], axis=0))
        ones.append(jnp.concatenate([jnp.where(prev_ok, halo_one, halo_zero),
                                     jnp.ones((POOL_SUB, POOL_GROUP), BF16),
                                     jnp.where(next_ok, halo_one, halo_zero)], axis=0))
    cols = lambda g: slice(g * POOL_GROUP, (g + 1) * POOL_GROUP)
    both = [jnp.dot(band_ref[g], jnp.concatenate([ext[sb][:, cols(g)], ones[sb]], axis=1),
                    preferred_element_type=F32) for sb, g in units]
    pooled = [(bt[:, :POOL_GROUP] / bt[:, POOL_GROUP:]
               - u[sb * POOL_SUB:(sb + 1) * POOL_SUB, cols(g)].astype(F32)).astype(BF16)
              for (sb, g), bt in zip(units, both)]
    outs = [jnp.dot(pq, pw_ref[g], preferred_element_type=F32) * ps_ref[:, cols(g)]
            for (sb, g), pq in zip(units, pooled)]
    yb = jnp.concatenate([jnp.concatenate(outs[sb * POOL_GROUPS:(sb + 1) * POOL_GROUPS], axis=1)
                          for sb in range(n_sub)], axis=0).astype(BF16)

    x = x_ref[...]
    h = _modulate(x, g0_ref[...], mod_ref[0, 0:1, :], mod_ref[0, 1:2, :]).astype(BF16)
    branches = (ya, yb, yc)
    m = None
    for k in range(N_BRANCH):
        gate = jnp.dot(h, wg_ref[:, k * D_MODEL:(k + 1) * D_MODEL], preferred_element_type=F32)
        gate = 1.0 / (1.0 + jnp.exp(-gate))
        term = gate * jnp.dot(branches[k], wb_ref[k], preferred_element_type=F32)
        m = term if m is None else m + term
    mix = jnp.dot(m.astype(BF16), wo_ref[...], preferred_element_type=F32)
    o_ref[...] = x + mod_ref[0, 2:3, :] * _rms(mix, g1_ref[...])


def _merge_call(tok, mod, g0, g1, wg, ya_lat, ya_ctx, yc_lat, yc_ctx, u, band, pool_w, pool_scale, wb, wo,
                nb, n, n_tiles):
    nt, d = tok.shape
    n_lat, mod_map, _ = _tile_maps(nb, n, CTX_LEN)
    row = lambda t: (t, 0)
    per_tile = TM // POOL_HALO
    last = nt // POOL_HALO - 1
    wide = pl.BlockSpec((TM, BRANCH_WIDTH), row)
    n_ctx = ya_ctx.shape[0] // TM
    lat_rows = pl.BlockSpec((TM, BRANCH_WIDTH), lambda t: (jnp.minimum(t, n_lat - 1), 0))
    ctx_rows = pl.BlockSpec((TM, BRANCH_WIDTH), lambda t: (jnp.clip(t - n_lat, 0, n_ctx - 1), 0))
    kern = functools.partial(_merge_kernel, n_lat=n_lat, n=n)
    return pl.pallas_call(
        kern,
        grid=(n_tiles,),
        in_specs=[pl.BlockSpec((TM, d), row),
                  pl.BlockSpec((1, 6, d), mod_map),
                  _const_spec((1, d)), _const_spec((1, d)),
                  _const_spec(wg.shape),
                  lat_rows, ctx_rows, lat_rows, ctx_rows, wide,
                  pl.BlockSpec((POOL_HALO, BRANCH_WIDTH), lambda t: (jnp.maximum(t * per_tile - 1, 0), 0)),
                  pl.BlockSpec((POOL_HALO, BRANCH_WIDTH), lambda t: (jnp.minimum((t + 1) * per_tile, last), 0)),
                  _const_spec(band.shape), _const_spec(pool_w.shape), _const_spec((1, BRANCH_WIDTH)),
                  _const_spec(wb.shape), _const_spec(wo.shape)],
        out_specs=pl.BlockSpec((TM, d), row),
        out_shape=jax.ShapeDtypeStruct((n_tiles * TM, d), F32),
        compiler_params=_params(1),
        name="pool_merge",
    )(tok, mod, g0, g1, wg, ya_lat, ya_ctx, yc_lat, yc_ctx, u, u, u, band, pool_w, pool_scale, wb, wo)


def _ffn_kernel(x_ref, mod_ref, g2_ref, g3_ref, wi_ref, wo_ref, o_ref):
    step = 256
    for rb in range(TM // FFN_ROWS):
        rows = slice(rb * FFN_ROWS, (rb + 1) * FFN_ROWS)
        x = x_ref[rows, :]
        h = _modulate(x, g2_ref[...], mod_ref[0, 3:4, :], mod_ref[0, 4:5, :]).astype(BF16)
        gates = [jnp.dot(h, wi_ref[:, c * step:(c + 1) * step], preferred_element_type=F32)
                 for c in range(D_FF // step)]
        ups = [jnp.dot(h, wi_ref[:, D_FF + c * step:D_FF + (c + 1) * step], preferred_element_type=F32)
               for c in range(D_FF // step)]
        act = jnp.concatenate([(g * (1.0 / (1.0 + jnp.exp(-g))) * u).astype(BF16) for g, u in zip(gates, ups)],
                              axis=1)
        acc = jnp.dot(act, wo_ref[...], preferred_element_type=F32)
        o_ref[rows, :] = x + mod_ref[0, 5:6, :] * _rms(acc, g3_ref[...])


def _ffn_call(tok, mod, g2, g3, wi, wo, nb, n):
    nt, d = tok.shape
    _, mod_map, _ = _tile_maps(nb, n, CTX_LEN)
    row = lambda t: (t, 0)
    return pl.pallas_call(
        _ffn_kernel,
        grid=(nt // TM,),
        in_specs=[pl.BlockSpec((TM, d), row),
                  pl.BlockSpec((1, 6, d), mod_map),
                  _const_spec((1, d)), _const_spec((1, d)),
                  _const_spec(wi.shape), _const_spec(wo.shape)],
        out_specs=pl.BlockSpec((TM, d), row),
        out_shape=jax.ShapeDtypeStruct((nt, d), F32),
        compiler_params=_params(1),
        name="swiglu_ffn",
    )(tok, mod, g2, g3, wi, wo)


def _permute_proj_columns(w):
    bw = BRANCH_WIDTH
    half = DA_QK_DIM // 2
    lead = w.shape[:-1]
    n = len(lead)
    keep = tuple(range(n))

    def diff_attn(x):
        x = x.reshape(lead + (DA_HEADS, 2, half, 2))
        return x.transpose(keep + (n, n + 3, n + 1, n + 2)).reshape(lead + (bw,))

    def win_queries(x):
        x = x.reshape(lead + (2, WA_HEADS // 2, half, 2))
        return x.transpose(keep + (n + 1, n + 3, n, n + 2)).reshape(lead + (bw,))

    def win_keys(x):
        x = x.reshape(lead + (WA_KV_HEADS, half, 2))
        return x.transpose(keep + (n + 2, n, n + 1)).reshape(lead + (LANES,))

    return jnp.concatenate([diff_attn(w[..., :bw]), diff_attn(w[..., bw:2 * bw]), w[..., 2 * bw:4 * bw],
                            win_queries(w[..., 4 * bw:5 * bw]), win_keys(w[..., 5 * bw:5 * bw + LANES]),
                            w[..., 5 * bw + LANES:PROJ_WIDTH]], axis=-1)


def _permute_wa_rows(w):
    lead = w.shape[:-2]
    n = len(lead)
    x = w.reshape(lead + (2, WA_HEADS // 2, WA_HEAD_DIM, w.shape[-1]))
    return x.transpose(tuple(range(n)) + (n + 1, n, n + 2, n + 3)).reshape(w.shape)


def _rope_tables(n):
    rows = n // GRID_W
    row = jnp.repeat(jnp.arange(rows, dtype=F32), GRID_W)
    col = jnp.tile(jnp.arange(GRID_W, dtype=F32), rows)
    n_freq = DA_QK_DIM // 4
    inv = ROPE_THETA ** (-jnp.arange(n_freq, dtype=F32) / n_freq)
    ang = jnp.concatenate([row[:, None] * inv, col[:, None] * inv], axis=-1)
    cos = jnp.tile(jnp.cos(ang), (1, 4))
    sin = jnp.tile(jnp.sin(ang), (1, 4)) * jnp.where(jnp.arange(LANES) < 64, -1.0, 1.0)
    cos = jnp.concatenate([cos, jnp.ones((TM, LANES), F32)], axis=0)
    sin = jnp.concatenate([sin, jnp.zeros((TM, LANES), F32)], axis=0)
    return cos, sin


def kernel(x, c, ctx, c_ctx, w_ada, b_ada, norm_g, w_in, da_lambda, da_subln_g,
           pool_w, pool_scale, wa_sink, w_branch, w_out, w_ffn_in, w_ffn_out):
    nb, n, d = x.shape
    depth = w_ada.shape[0]
    assert d == D_MODEL and ctx.shape[1] == CTX_LEN
    assert n % TM == 0 and (nb * CTX_LEN) % TM == 0 and n % CK == 0 and n % GRID_W == 0
    assert n % TQ == 0 and n % (WA_STEP * WA_BLOCK) == 0
    assert TM % POOL_SUB == 0 and CTX_LEN % POOL_SUB == 0 and (nb * n) % CTX_LEN == 0
    assert WA_HEAD_DIM == DA_QK_DIM

    tok = jnp.concatenate([x.reshape(nb * n, d), ctx.reshape(nb * CTX_LEN, d)], axis=0)
    cvec = jnp.concatenate([c, c_ctx[None, :]], axis=0)
    mod = _mod_call(cvec, w_ada, b_ada).reshape(depth, nb + 1, 6, d)
    cos, sin = _rope_tables(n)
    band = jnp.asarray(_pool_band(), BF16)

    w1 = _permute_proj_columns(w_in[:, :, :PROJ_WIDTH]).astype(BF16)
    wg = w_in[:, :, PROJ_WIDTH:].astype(BF16)
    wb = jnp.concatenate([w_branch[:, :2], _permute_wa_rows(w_branch[:, 2:])], axis=1).astype(BF16)
    wo = w_out.astype(BF16)
    wfi = w_ffn_in.astype(BF16)
    wfo = w_ffn_out.astype(BF16)
    pw = pool_w.astype(BF16)

    n_tiles_all = tok.shape[0] // TM
    n_tiles_lat = nb * n // TM
    for l in range(depth):
        last = l == depth - 1
        lam_init = 0.8 - 0.6 * math.exp(-0.3 * l)
        g = norm_g[l].reshape(4, 1, d)
        qa, ka, va, u, qw, kw, vw = _proj_call(tok, mod[l], g[0], w1[l], cos, sin, nb, n)
        g_sub = da_subln_g[l].reshape(1, DA_V_DIM)
        ya = _da_call(qa, ka, va, da_lambda[l], g_sub, lam_init, nb, n)
        ya_ctx = ya if last else _da_ctx_call(qa, ka, va, da_lambda[l], g_sub, lam_init, nb, n)
        yc = _wa_call(qw, kw, vw, wa_sink[l], nb, n)
        yc_ctx = yc if last else _wa_ctx_call(qw, kw, vw, wa_sink[l], nb, n)
        n_tiles = n_tiles_lat if last else n_tiles_all
        tok = _merge_call(tok, mod[l], g[0], g[1], wg[l], ya, ya_ctx, yc, yc_ctx, u, band, pw[l],
                          pool_scale[l].reshape(1, BRANCH_WIDTH), wb[l], wo[l], nb, n, n_tiles)
        tok = _ffn_call(tok, mod[l], g[2], g[3], wfi[l], wfo[l], nb, n)
    return tok[:nb * n].reshape(nb, n, d)
```

```python
import functools
import math

import numpy as np
import jax
import jax.numpy as jnp
from jax import lax
from jax.experimental import pallas as pl
from jax.experimental.pallas import tpu as pltpu

F32 = jnp.float32
BF16 = jnp.bfloat16

D_MODEL = 1024
CTX_LEN = 256
GRID_W = 64
EPS = 1e-6
ROPE_THETA = 10000.0
BRANCH_WIDTH = D_MODEL // 2
N_BRANCH = 3
DA_HEADS = 4
DA_V_DIM = BRANCH_WIDTH // DA_HEADS
DA_QK_DIM = DA_V_DIM // 2
POOL_WINDOWS = (2, 4, 8, 16)
POOL_GROUPS = 4
POOL_GROUP = BRANCH_WIDTH // POOL_GROUPS
WA_HEAD_DIM = 64
WA_HEADS = BRANCH_WIDTH // WA_HEAD_DIM
WA_KV_HEADS = 2
WA_WINDOW = 128
WA_BLOCK = 128
D_FF = ((8 * D_MODEL + 3 * 256 - 1) // (3 * 256)) * 256
PROJ_WIDTH = 5 * BRANCH_WIDTH + 2 * WA_KV_HEADS * WA_HEAD_DIM

LANES = 128
BF16_SUBLANES = 16
VMEM_LIMIT = 56 * 1024 * 1024

TM = 1024
ROW_BLOCK = 256
FFN_ROWS = 256
TQ = 512
CK = 256
WA_STEP = 4
POOL_SUB = 128
POOL_HALO = BF16_SUBLANES


def _params(n_axes):
    return pltpu.CompilerParams(dimension_semantics=("arbitrary",) * n_axes,
                                vmem_limit_bytes=VMEM_LIMIT)


def _const_spec(shape):
    nd = len(shape)
    return pl.BlockSpec(shape, lambda *_: (0,) * nd, pipeline_mode=pl.Buffered(1))


def _rms(x, g):
    return x * lax.rsqrt(jnp.mean(x * x, axis=-1, keepdims=True) + EPS) * g


def _modulate(x, g, shift, scale):
    return _rms(x, g) * (1.0 + scale) + shift


def _mod_kernel(c_ref, w_ref, b_ref, o_ref):
    c = c_ref[...]
    a = c * (1.0 / (1.0 + jnp.exp(-c)))
    o_ref[0] = jnp.dot(a, w_ref[0], preferred_element_type=F32,
                       precision=lax.Precision.HIGHEST) + b_ref[0]


def _mod_call(cvec, w_ada, b_ada):
    depth, d, width = w_ada.shape
    rows = cvec.shape[0]
    tn = 1536
    return pl.pallas_call(
        _mod_kernel,
        grid=(depth, width // tn),
        in_specs=[pl.BlockSpec((rows, d), lambda l, j: (0, 0)),
                  pl.BlockSpec((1, d, tn), lambda l, j: (l, 0, j)),
                  pl.BlockSpec((1, 1, tn), lambda l, j: (l, 0, j))],
        out_specs=pl.BlockSpec((1, rows, tn), lambda l, j: (l, 0, j)),
        out_shape=jax.ShapeDtypeStruct((depth, rows, width), F32),
        compiler_params=_params(2),
        name="adaln_mod",
    )(cvec, w_ada, b_ada.reshape(depth, 1, width))


def _proj_kernel(x_ref, mod_ref, g_ref, w_ref, c_ref, s_ref,
                 qa_ref, ka_ref, va_ref, u_ref, qw_ref, kw_ref, vw_ref):
    lane = lax.broadcasted_iota(jnp.int32, (ROW_BLOCK, LANES), 1)
    first = (lane & (DA_QK_DIM // 2)) == 0
    bw = BRANCH_WIDTH

    for rb in range(TM // ROW_BLOCK):
        rows = slice(rb * ROW_BLOCK, (rb + 1) * ROW_BLOCK)
        h = _modulate(x_ref[rows, :], g_ref[...], mod_ref[0, 0:1, :], mod_ref[0, 1:2, :]).astype(BF16)
        cos = c_ref[rows, :]
        sin = s_ref[rows, :]

        def proj(lo, width):
            return jnp.dot(h, w_ref[:, lo:lo + width], preferred_element_type=F32)

        def rope(yj):
            return yj * cos + pltpu.roll(yj, LANES // 2, 1) * sin

        def key_store(ref, y):
            for j in range(y.shape[1] // LANES):
                ref[rows, j * LANES:(j + 1) * LANES] = rope(y[:, j * LANES:(j + 1) * LANES]).astype(BF16)

        def query_store(ref, y, scale):
            for j in range(y.shape[1] // LANES):
                r = rope(y[:, j * LANES:(j + 1) * LANES]) * scale
                ref[rows, (2 * j) * LANES:(2 * j + 1) * LANES] = jnp.where(first, r, 0.0).astype(BF16)
                ref[rows, (2 * j + 1) * LANES:(2 * j + 2) * LANES] = jnp.where(first, 0.0, r).astype(BF16)

        ya, yk, yv, yu, yq = (proj(j * bw, bw) for j in range(5))
        kvw = proj(5 * bw, 2 * LANES)
        query_store(qa_ref, ya, DA_QK_DIM ** -0.5 * math.log2(math.e))
        key_store(ka_ref, yk)
        va_ref[:, rows] = yv.T.astype(BF16)
        u_ref[rows, :] = yu.astype(BF16)
        query_store(qw_ref, yq, WA_HEAD_DIM ** -0.5 * math.log2(math.e))
        key_store(kw_ref, kvw[:, :LANES])
        vw_ref[rows, :] = kvw[:, LANES:].astype(BF16)


def _tile_maps(nb, n):
    n_lat = nb * n // TM
    per_seq = n // TM

    def mod_map(t):
        return (jnp.where(t < n_lat, t // per_seq, nb), 0, 0)

    def rope_map(t):
        return (jnp.where(t < n_lat, t % per_seq, per_seq), 0)

    return n_lat, mod_map, rope_map


def _proj_call(tok, mod, g, w1, cos, sin, nb, n):
    nt, d = tok.shape
    _, mod_map, rope_map = _tile_maps(nb, n)
    row = lambda t: (t, 0)
    spec = lambda w: pl.BlockSpec((TM, w), row)
    sd = lambda w: jax.ShapeDtypeStruct((nt, w), BF16)
    widths = (2 * BRANCH_WIDTH, BRANCH_WIDTH, BRANCH_WIDTH, BRANCH_WIDTH, 2 * BRANCH_WIDTH, LANES, LANES)
    return pl.pallas_call(
        _proj_kernel,
        grid=(nt // TM,),
        in_specs=[pl.BlockSpec((TM, d), row),
                  pl.BlockSpec((1, 6, d), mod_map),
                  _const_spec((1, d)),
                  _const_spec(w1.shape),
                  pl.BlockSpec((TM, LANES), rope_map),
                  pl.BlockSpec((TM, LANES), rope_map)],
        out_specs=[pl.BlockSpec((BRANCH_WIDTH, TM), lambda t: (0, t)) if j == 2 else spec(w)
                   for j, w in enumerate(widths)],
        out_shape=[jax.ShapeDtypeStruct((BRANCH_WIDTH, nt), BF16) if j == 2 else sd(w)
                   for j, w in enumerate(widths)],
        compiler_params=_params(1),
        name="in_proj",
    )(tok, mod, g, w1, cos, sin)


def _da_lambda(dl_ref, lam_init):
    lp = dl_ref[...]
    return (jnp.exp(jnp.sum(lp[0:1] * lp[1:2], axis=-1, keepdims=True))
            - jnp.exp(jnp.sum(lp[2:3] * lp[3:4], axis=-1, keepdims=True)) + lam_init)


def _da_scores(k, qs):
    return lax.dot_general(k, qs, (((1,), (1,)), ((), ())), preferred_element_type=F32)


def _da_fold(x, op):
    return op(x.reshape(x.shape[0] // 8, 8, x.shape[1]), axis=0)


def _da_pv(v, e):
    return lax.dot_general(v, e.astype(BF16), (((0,), (0,)), ((), ())), preferred_element_type=F32)


def _da_pvt(vt, e):
    return jnp.dot(vt, e.astype(BF16), preferred_element_type=F32)


def _da_finish(o_ref, g_ref, lam, lam_init, l8, acc):
    tq = o_ref.shape[0]
    l = jnp.sum(l8, axis=0, keepdims=True)
    ot = acc[:, :tq] / l[:, :tq] - lam * (acc[:, tq:] / l[:, tq:])
    o_ref[...] = (_rms(ot.T, g_ref[...]) * (1.0 - lam_init)).astype(BF16)


def _da_kernel(dl_ref, g_ref, q_ref, kl_ref, kc_ref, vl_ref, vc_ref, o_ref, s_scr, m_scr, l_scr, acc_scr, *,
               lam_init):
    t = pl.program_id(0)
    lam = _da_lambda(dl_ref, lam_init)
    n_lat = kl_ref.shape[0]
    qs = jnp.concatenate([q_ref[:, :LANES], q_ref[:, LANES:]], axis=0)

    @pl.when(t == 0)
    def _():
        s_scr[1] = jnp.zeros(s_scr.shape[1:], F32)
        m_scr[1] = jnp.zeros(m_scr.shape[1:], F32)
        l_scr[...] = jnp.ones(l_scr.shape, F32)
        acc_scr[...] = jnp.zeros(acc_scr.shape, F32)

    def step(wr, rd):
        _da_finish(o_ref, g_ref, lam, lam_init, l_scr[...], acc_scr[...])
        m_prev = m_scr[rd]
        m8 = l8 = acc = None
        for c in range(n_lat // CK):
            rows = slice(c * CK, (c + 1) * CK)
            s = _da_scores(kl_ref[rows, :], qs)
            s_scr[wr, rows, :] = s
            f = _da_fold(s, jnp.max)
            m8 = f if m8 is None else jnp.maximum(m8, f)
            e = jnp.exp2(s_scr[rd, rows, :] - m_prev)
            f = _da_fold(e, jnp.sum)
            p = _da_pvt(vl_ref[:, rows], e)
            l8 = f if l8 is None else l8 + f
            acc = p if acc is None else acc + p
        rows = slice(n_lat, n_lat + CTX_LEN)
        sc = _da_scores(kc_ref[...], qs)
        s_scr[wr, rows, :] = sc
        m_scr[wr] = jnp.max(jnp.maximum(m8, _da_fold(sc, jnp.max)), axis=0, keepdims=True)
        e = jnp.exp2(s_scr[rd, rows, :] - m_prev)
        l_scr[...] = l8 + _da_fold(e, jnp.sum)
        acc_scr[...] = acc + _da_pvt(vc_ref[...], e)

    @pl.when(t % 2 == 0)
    def _():
        step(0, 1)

    @pl.when(t % 2 == 1)
    def _():
        step(1, 0)


def _da_call(qa, ka, va, da_lambda, g_sub, lam_init, nb, n):
    n_q = n // TQ
    units = nb * DA_HEADS * n_q
    ctx0 = nb * n // CTX_LEN

    def unit(u):
        u = jnp.clip(u, 0, units - 1)
        return u // (n_q * DA_HEADS), (u // n_q) % DA_HEADS, u % n_q

    def scored(t):
        return unit(t)

    def finished(t):
        return unit(t - 1)

    def written(t):
        return unit(t - 2)

    def tile_map(which):
        def m(t):
            b, h, i = which(t)
            return (b * n_q + i, h)
        return m

    def lat(which):
        return pl.BlockSpec((n, DA_V_DIM), lambda t: which(t)[:2])

    def ctx(which):
        def m(t):
            b, h, _ = which(t)
            return (ctx0 + b, h)
        return pl.BlockSpec((CTX_LEN, DA_V_DIM), m)

    def lat_t(which):
        return pl.BlockSpec((DA_V_DIM, n), lambda t: which(t)[1::-1])

    def ctx_t(which):
        def m(t):
            b, h, _ = which(t)
            return (h, ctx0 + b)
        return pl.BlockSpec((DA_V_DIM, CTX_LEN), m)

    kern = functools.partial(_da_kernel, lam_init=lam_init)
    return pl.pallas_call(
        kern,
        grid=(units + 2,),
        in_specs=[pl.BlockSpec(da_lambda.shape, lambda t: (0, 0)),
                  pl.BlockSpec((1, DA_V_DIM), lambda t: (0, 0)),
                  pl.BlockSpec((TQ, 2 * DA_V_DIM), tile_map(scored)),
                  lat(scored), ctx(scored), lat_t(finished), ctx_t(finished)],
        out_specs=pl.BlockSpec((TQ, DA_V_DIM), tile_map(written)),
        out_shape=jax.ShapeDtypeStruct((nb * n, BRANCH_WIDTH), BF16),
        scratch_shapes=[pltpu.VMEM((2, n + CTX_LEN, 2 * TQ), F32), pltpu.VMEM((2, 1, 2 * TQ), F32),
                        pltpu.VMEM((8, 2 * TQ), F32), pltpu.VMEM((DA_V_DIM, 2 * TQ), F32)],
        compiler_params=_params(1),
        name="diff_attn",
    )(da_lambda, g_sub, qa, ka, ka, va, va)


def _da_ctx_kernel(dl_ref, g_ref, q_ref, k_ref, v_ref, o_ref, *, lam_init):
    lam = _da_lambda(dl_ref, lam_init)
    heads = range(DA_HEADS)
    d = DA_V_DIM
    qs = [jnp.concatenate([q_ref[:, 2 * h * d:(2 * h + 1) * d], q_ref[:, (2 * h + 1) * d:(2 * h + 2) * d]], axis=0)
          for h in heads]
    ss = [_da_scores(k_ref[:, h * d:(h + 1) * d], qs[h]) for h in heads]
    es = [jnp.exp2(s - jnp.max(s, axis=0, keepdims=True)) for s in ss]
    accs = [_da_pvt(v_ref[h * d:(h + 1) * d, :], es[h]) for h in heads]
    for h in heads:
        _da_finish(o_ref.at[:, h * d:(h + 1) * d], g_ref, lam, lam_init, _da_fold(es[h], jnp.sum), accs[h])


def _da_ctx_call(qa, ka, va, da_lambda, g_sub, lam_init, nb, n):
    ctx0 = nb * n // CTX_LEN
    blk = lambda w: pl.BlockSpec((CTX_LEN, w), lambda b: (ctx0 + b, 0))
    kern = functools.partial(_da_ctx_kernel, lam_init=lam_init)
    return pl.pallas_call(
        kern,
        grid=(nb,),
        in_specs=[pl.BlockSpec(da_lambda.shape, lambda b: (0, 0)),
                  pl.BlockSpec((1, DA_V_DIM), lambda b: (0, 0)),
                  blk(2 * BRANCH_WIDTH), blk(BRANCH_WIDTH),
                  pl.BlockSpec((BRANCH_WIDTH, CTX_LEN), lambda b: (0, ctx0 + b))],
        out_specs=pl.BlockSpec((CTX_LEN, BRANCH_WIDTH), lambda b: (b, 0)),
        out_shape=jax.ShapeDtypeStruct((nb * CTX_LEN, BRANCH_WIDTH), BF16),
        compiler_params=_params(1),
        name="diff_attn_ctx",
    )(da_lambda, g_sub, qa, ka, va)


def _wa_pair_queries(q_ref, j, rows=slice(0, WA_BLOCK)):
    return jnp.concatenate([q_ref[rows, (2 * j) * LANES:(2 * j + 1) * LANES],
                            q_ref[rows, (2 * j + 1) * LANES:(2 * j + 2) * LANES]], axis=0)


def _wa_pair_sinks(sink_ref, j):
    lane = lax.broadcasted_iota(jnp.int32, (1, 2 * WA_BLOCK), 1)
    return jnp.where(lane < WA_BLOCK, sink_ref[j], sink_ref[j + WA_HEADS // 2]) * math.log2(math.e)


def _wa_pair_max(s, sk):
    return jnp.maximum(jnp.max(_da_fold(s, jnp.max), axis=0, keepdims=True), sk)


def _wa_pair_finish(o_ref, j, s, m, sk, v, rows=slice(0, WA_BLOCK)):
    e = jnp.exp2(s - m)
    den = jnp.sum(_da_fold(e, jnp.sum), axis=0, keepdims=True) + jnp.exp2(sk - m)
    ot = _da_pv(v, e) / den
    first = lax.broadcasted_iota(jnp.int32, (LANES, WA_BLOCK), 0) < WA_HEAD_DIM
    o_ref[rows, j * LANES:(j + 1) * LANES] = jnp.where(first, ot[:, :WA_BLOCK], ot[:, WA_BLOCK:]).T.astype(BF16)


def _wa_band_bias():
    band = 3 * WA_BLOCK
    key = np.arange(band + CTX_LEN)[:, None]
    qpos = np.arange(2 * WA_BLOCK)[None, :] % WA_BLOCK
    ok = (key >= band) | (np.abs(key - WA_BLOCK - qpos) <= WA_WINDOW)
    return np.where(ok, 0.0, -np.inf).astype(np.float32)


def _wa_kernel(sink_ref, bias_ref, q_ref, kp_ref, kc_ref, kn_ref, kx_ref, vp_ref, vc_ref, vn_ref, vx_ref,
               o_ref, s_scr, m_scr, *, n_steps, units):
    t = pl.program_id(0)
    i = lax.rem(jnp.minimum(t, units - 1), n_steps)

    @pl.when(t == 0)
    def _():
        s_scr[1] = jnp.zeros(s_scr.shape[1:], F32)
        m_scr[1] = jnp.zeros(m_scr.shape[1:], F32)

    def step(wr, rd):
        def gather(p_ref, c_ref, n_ref, x_ref):
            blocks = [p_ref[...]] + [c_ref[b * WA_BLOCK:(b + 1) * WA_BLOCK, :] for b in range(WA_STEP)] + [n_ref[...]]
            return [jnp.concatenate(blocks[b:b + 3] + [x_ref[...]], axis=0) for b in range(WA_STEP)]

        ks = gather(kp_ref, kc_ref, kn_ref, kx_ref)
        vs = gather(vp_ref, vc_ref, vn_ref, vx_ref)
        band = 3 * WA_BLOCK
        out = jnp.full((WA_BLOCK, 2 * WA_BLOCK), -jnp.inf, F32)
        for b in range(WA_STEP):
            rows = slice(b * WA_BLOCK, (b + 1) * WA_BLOCK)
            lo = bias_ref[:WA_BLOCK, :]
            hi = bias_ref[2 * WA_BLOCK:band, :]
            if b == 0:
                lo = jnp.where(i > 0, lo, out)
            if b == WA_STEP - 1:
                hi = jnp.where(i < n_steps - 1, hi, out)
            bias = jnp.concatenate([lo, bias_ref[WA_BLOCK:2 * WA_BLOCK, :], hi, bias_ref[band:, :]], axis=0)
            for j in range(WA_HEADS // 2):
                c0 = (b * (WA_HEADS // 2) + j) * 2 * WA_BLOCK
                cols = slice(c0, c0 + 2 * WA_BLOCK)
                sk = _wa_pair_sinks(sink_ref, j)
                s = _da_scores(ks[b], _wa_pair_queries(q_ref, j, rows)) + bias
                s_scr[wr, :, cols] = s
                m_scr[wr, :, cols] = _wa_pair_max(s, sk)
                _wa_pair_finish(o_ref, j, s_scr[rd, :, cols], m_scr[rd, :, cols], sk, vs[b], rows)

    @pl.when(t % 2 == 0)
    def _():
        step(0, 1)

    @pl.when(t % 2 == 1)
    def _():
        step(1, 0)


def _wa_call(qw, kw, vw, sink, nb, n):
    n_blk = n // WA_BLOCK
    n_steps = n_blk // WA_STEP
    units = nb * n_steps
    ctx0 = nb * n // CTX_LEN

    def scored(t):
        u = jnp.minimum(t, units - 1)
        return u // n_steps, u % n_steps

    def finished(t):
        u = jnp.maximum(t - 1, 0)
        return u // n_steps, u % n_steps

    def q_map(which):
        def m(t):
            b, i = which(t)
            return (b * n_steps + i, 0)
        return m

    def edge(which, delta):
        def m(t):
            b, i = which(t)
            return (b * n_blk + jnp.clip(i * WA_STEP + delta, 0, n_blk - 1), 0)
        return pl.BlockSpec((WA_BLOCK, LANES), m)

    def own(which):
        return pl.BlockSpec((WA_STEP * WA_BLOCK, LANES), q_map(which))

    def ctx(which):
        return pl.BlockSpec((CTX_LEN, LANES), lambda t: (ctx0 + which(t)[0], 0))

    bias = jnp.asarray(_wa_band_bias())
    keys = bias.shape[0]
    width = WA_STEP * WA_HEADS * WA_BLOCK
    kern = functools.partial(_wa_kernel, n_steps=n_steps, units=units)
    return pl.pallas_call(
        kern,
        grid=(units + 1,),
        in_specs=[pl.BlockSpec(memory_space=pltpu.SMEM),
                  _const_spec(bias.shape),
                  pl.BlockSpec((WA_STEP * WA_BLOCK, 2 * BRANCH_WIDTH), q_map(scored)),
                  edge(scored, -1), own(scored), edge(scored, WA_STEP), ctx(scored),
                  edge(finished, -1), own(finished), edge(finished, WA_STEP), ctx(finished)],
        out_specs=pl.BlockSpec((WA_STEP * WA_BLOCK, BRANCH_WIDTH), q_map(finished)),
        out_shape=jax.ShapeDtypeStruct((nb * n, BRANCH_WIDTH), BF16),
        scratch_shapes=[pltpu.VMEM((2, keys, width), F32), pltpu.VMEM((2, 1, width), F32)],
        compiler_params=_params(1),
        name="window_attn",
    )(sink, bias, qw, kw, kw, kw, kw, vw, vw, vw, vw)


def _wa_ctx_kernel(sink_ref, q_ref, k_ref, v_ref, o_ref):
    k = k_ref[...]
    units = [(slice(b * WA_BLOCK, (b + 1) * WA_BLOCK), j)
             for b in range(CTX_LEN // WA_BLOCK) for j in range(WA_HEADS // 2)]
    ss = [_da_scores(k, _wa_pair_queries(q_ref, j, rows)) for rows, j in units]
    for (rows, j), s in zip(units, ss):
        sk = _wa_pair_sinks(sink_ref, j)
        _wa_pair_finish(o_ref, j, s, _wa_pair_max(s, sk), sk, v_ref[...], rows)


def _wa_ctx_call(qw, kw, vw, sink, nb, n):
    ctx0 = nb * n // CTX_LEN
    rows = lambda w: pl.BlockSpec((CTX_LEN, w), lambda b: (ctx0 + b, 0))
    return pl.pallas_call(
        _wa_ctx_kernel,
        grid=(nb,),
        in_specs=[pl.BlockSpec(memory_space=pltpu.SMEM), rows(2 * BRANCH_WIDTH), rows(LANES), rows(LANES)],
        out_specs=pl.BlockSpec((CTX_LEN, BRANCH_WIDTH), lambda b: (b, 0)),
        out_shape=jax.ShapeDtypeStruct((nb * CTX_LEN, BRANCH_WIDTH), BF16),
        compiler_params=_params(1),
        name="window_attn_ctx",
    )(sink, qw, kw, vw)


def _pool_band():
    ext = POOL_SUB + 2 * POOL_HALO
    t = np.arange(POOL_SUB)[:, None]
    d = np.arange(ext)[None, :] - POOL_HALO - t
    return np.stack([((d >= -(w // 2)) & (d < w - w // 2)) for w in POOL_WINDOWS]).astype(np.float32)


def _merge_kernel(x_ref, mod_ref, g0_ref, g1_ref, wg_ref, yal_ref, yax_ref, ycl_ref, ycx_ref,
                  u_ref, up_ref, un_ref, band_ref, pw_ref, ps_ref, wb_ref, wo_ref, o_ref, *, n_lat, n):
    t = pl.program_id(0)
    is_lat = t < n_lat
    ya = jnp.where(is_lat, yal_ref[...], yax_ref[...])
    yc = jnp.where(is_lat, ycl_ref[...], ycx_ref[...])

    u = u_ref[...]
    halo_one = jnp.ones((POOL_HALO, POOL_GROUP), BF16)
    halo_zero = jnp.zeros((POOL_HALO, POOL_GROUP), BF16)
    n_sub = TM // POOL_SUB
    units = [(sb, g) for sb in range(n_sub) for g in range(POOL_GROUPS)]
    ext, ones = [], []
    for sb in range(n_sub):
        r0 = sb * POOL_SUB
        prev = up_ref[...] if sb == 0 else u[r0 - POOL_HALO:r0]
        nxt = un_ref[...] if r0 + POOL_SUB == TM else u[r0 + POOL_SUB:r0 + POOL_SUB + POOL_HALO]
        g_row = t * TM + r0
        seq_row = jnp.where(is_lat, g_row % n, g_row % CTX_LEN)
        seq_len = jnp.where(is_lat, n, CTX_LEN)
        prev_ok = seq_row != 0
        next_ok = seq_row + POOL_SUB != seq_len
        ext.append(jnp.concatenate([jnp.where(prev_ok, prev, jnp.zeros_like(prev)), u[r0:r0 + POOL_SUB],
                                    jnp.where(next_ok, nxt, jnp.zeros_like(nxt))], axis=0))
        ones.append(jnp.concatenate([jnp.where(prev_ok, halo_one, halo_zero),
                                     jnp.ones((POOL_SUB, POOL_GROUP), BF16),
                                     jnp.where(next_ok, halo_one, halo_zero)], axis=0))
    cols = lambda g: slice(g * POOL_GROUP, (g + 1) * POOL_GROUP)
    both = [jnp.dot(band_ref[g], jnp.concatenate([ext[sb][:, cols(g)], ones[sb]], axis=1),
                    preferred_element_type=F32) for sb, g in units]
    pooled = [(bt[:, :POOL_GROUP] / bt[:, POOL_GROUP:]
               - u[sb * POOL_SUB:(sb + 1) * POOL_SUB, cols(g)].astype(F32)).astype(BF16)
              for (sb, g), bt in zip(units, both)]
    outs = [jnp.dot(pq, pw_ref[g], preferred_element_type=F32) * ps_ref[:, cols(g)]
            for (sb, g), pq in zip(units, pooled)]
    yb = jnp.concatenate([jnp.concatenate(outs[sb * POOL_GROUPS:(sb + 1) * POOL_GROUPS], axis=1)
                          for sb in range(n_sub)], axis=0).astype(BF16)

    x = x_ref[...]
    h = _modulate(x, g0_ref[...], mod_ref[0, 0:1, :], mod_ref[0, 1:2, :]).astype(BF16)
    branches = (ya, yb, yc)
    m = None
    for k in range(N_BRANCH):
        gate = jnp.dot(h, wg_ref[:, k * D_MODEL:(k + 1) * D_MODEL], preferred_element_type=F32)
        gate = 1.0 / (1.0 + jnp.exp(-gate))
        term = gate * jnp.dot(branches[k], wb_ref[k], preferred_element_type=F32)
        m = term if m is None else m + term
    mix = jnp.dot(m.astype(BF16), wo_ref[...], preferred_element_type=F32)
    o_ref[...] = x + mod_ref[0, 2:3, :] * _rms(mix, g1_ref[...])


def _merge_call(tok, mod, g0, g1, wg, ya_lat, ya_ctx, yc_lat, yc_ctx, u, band, pool_w, pool_scale, wb, wo,
                nb, n, n_tiles):
    nt, d = tok.shape
    n_lat, mod_map, _ = _tile_maps(nb, n)
    row = lambda t: (t, 0)
    per_tile = TM // POOL_HALO
    last = nt // POOL_HALO - 1
    wide = pl.BlockSpec((TM, BRANCH_WIDTH), row)
    n_ctx = ya_ctx.shape[0] // TM
    lat_rows = pl.BlockSpec((TM, BRANCH_WIDTH), lambda t: (jnp.minimum(t, n_lat - 1), 0))
    ctx_rows = pl.BlockSpec((TM, BRANCH_WIDTH), lambda t: (jnp.clip(t - n_lat, 0, n_ctx - 1), 0))
    kern = functools.partial(_merge_kernel, n_lat=n_lat, n=n)
    return pl.pallas_call(
        kern,
        grid=(n_tiles,),
        in_specs=[pl.BlockSpec((TM, d), row),
                  pl.BlockSpec((1, 6, d), mod_map),
                  _const_spec((1, d)), _const_spec((1, d)),
                  _const_spec(wg.shape),
                  lat_rows, ctx_rows, lat_rows, ctx_rows, wide,
                  pl.BlockSpec((POOL_HALO, BRANCH_WIDTH), lambda t: (jnp.maximum(t * per_tile - 1, 0), 0)),
                  pl.BlockSpec((POOL_HALO, BRANCH_WIDTH), lambda t: (jnp.minimum((t + 1) * per_tile, last), 0)),
                  _const_spec(band.shape), _const_spec(pool_w.shape), _const_spec((1, BRANCH_WIDTH)),
                  _const_spec(wb.shape), _const_spec(wo.shape)],
        out_specs=pl.BlockSpec((TM, d), row),
        out_shape=jax.ShapeDtypeStruct((n_tiles * TM, d), F32),
        compiler_params=_params(1),
        name="pool_merge",
    )(tok, mod, g0, g1, wg, ya_lat, ya_ctx, yc_lat, yc_ctx, u, u, u, band, pool_w, pool_scale, wb, wo)


def _ffn_kernel(x_ref, mod_ref, g2_ref, g3_ref, wi_ref, wo_ref, o_ref):
    step = 256
    for rb in range(TM // FFN_ROWS):
        rows = slice(rb * FFN_ROWS, (rb + 1) * FFN_ROWS)
        x = x_ref[rows, :]
        h = _modulate(x, g2_ref[...], mod_ref[0, 3:4, :], mod_ref[0, 4:5, :]).astype(BF16)
        gates = [jnp.dot(h, wi_ref[:, c * step:(c + 1) * step], preferred_element_type=F32)
                 for c in range(D_FF // step)]
        ups = [jnp.dot(h, wi_ref[:, D_FF + c * step:D_FF + (c + 1) * step], preferred_element_type=F32)
               for c in range(D_FF // step)]
        act = jnp.concatenate([(g * (1.0 / (1.0 + jnp.exp(-g))) * u).astype(BF16) for g, u in zip(gates, ups)],
                              axis=1)
        acc = jnp.dot(act, wo_ref[...], preferred_element_type=F32)
        o_ref[rows, :] = x + mod_ref[0, 5:6, :] * _rms(acc, g3_ref[...])


def _ffn_call(tok, mod, g2, g3, wi, wo, nb, n):
    nt, d = tok.shape
    _, mod_map, _ = _tile_maps(nb, n)
    row = lambda t: (t, 0)
    return pl.pallas_call(
        _ffn_kernel,
        grid=(nt // TM,),
        in_specs=[pl.BlockSpec((TM, d), row),
                  pl.BlockSpec((1, 6, d), mod_map),
                  _const_spec((1, d)), _const_spec((1, d)),
                  _const_spec(wi.shape), _const_spec(wo.shape)],
        out_specs=pl.BlockSpec((TM, d), row),
        out_shape=jax.ShapeDtypeStruct((nt, d), F32),
        compiler_params=_params(1),
        name="swiglu_ffn",
    )(tok, mod, g2, g3, wi, wo)


def _permute_proj_columns(w):
    bw = BRANCH_WIDTH
    half = DA_QK_DIM // 2
    lead = w.shape[:-1]
    n = len(lead)
    keep = tuple(range(n))

    def diff_attn(x):
        x = x.reshape(lead + (DA_HEADS, 2, half, 2))
        return x.transpose(keep + (n, n + 3, n + 1, n + 2)).reshape(lead + (bw,))

    def win_queries(x):
        x = x.reshape(lead + (2, WA_HEADS // 2, half, 2))
        return x.transpose(keep + (n + 1, n + 3, n, n + 2)).reshape(lead + (bw,))

    def win_keys(x):
        x = x.reshape(lead + (WA_KV_HEADS, half, 2))
        return x.transpose(keep + (n + 2, n, n + 1)).reshape(lead + (LANES,))

    return jnp.concatenate([diff_attn(w[..., :bw]), diff_attn(w[..., bw:2 * bw]), w[..., 2 * bw:4 * bw],
                            win_queries(w[..., 4 * bw:5 * bw]), win_keys(w[..., 5 * bw:5 * bw + LANES]),
                            w[..., 5 * bw + LANES:PROJ_WIDTH]], axis=-1)


def _permute_wa_rows(w):
    lead = w.shape[:-2]
    n = len(lead)
    x = w.reshape(lead + (2, WA_HEADS // 2, WA_HEAD_DIM, w.shape[-1]))
    return x.transpose(tuple(range(n)) + (n + 1, n, n + 2, n + 3)).reshape(w.shape)


def _rope_tables(n):
    rows = n // GRID_W
    row = jnp.repeat(jnp.arange(rows, dtype=F32), GRID_W)
    col = jnp.tile(jnp.arange(GRID_W, dtype=F32), rows)
    n_freq = DA_QK_DIM // 4
    inv = ROPE_THETA ** (-jnp.arange(n_freq, dtype=F32) / n_freq)
    ang = jnp.concatenate([row[:, None] * inv, col[:, None] * inv], axis=-1)
    cos = jnp.tile(jnp.cos(ang), (1, 4))
    sin = jnp.tile(jnp.sin(ang), (1, 4)) * jnp.where(jnp.arange(LANES) < 64, -1.0, 1.0)
    cos = jnp.concatenate([cos, jnp.ones((TM, LANES), F32)], axis=0)
    sin = jnp.concatenate([sin, jnp.zeros((TM, LANES), F32)], axis=0)
    return cos, sin


def kernel(x, c, ctx, c_ctx, w_ada, b_ada, norm_g, w_in, da_lambda, da_subln_g,
           pool_w, pool_scale, wa_sink, w_branch, w_out, w_ffn_in, w_ffn_out):
    nb, n, d = x.shape
    depth = w_ada.shape[0]
    assert d == D_MODEL and ctx.shape[1] == CTX_LEN
    assert n % TM == 0 and (nb * CTX_LEN) % TM == 0 and n % CK == 0 and n % GRID_W == 0
    assert n % TQ == 0 and n % (WA_STEP * WA_BLOCK) == 0
    assert TM % POOL_SUB == 0 and CTX_LEN % POOL_SUB == 0 and (nb * n) % CTX_LEN == 0
    assert WA_HEAD_DIM == DA_QK_DIM

    tok = jnp.concatenate([x.reshape(nb * n, d), ctx.reshape(nb * CTX_LEN, d)], axis=0)
    cvec = jnp.concatenate([c, c_ctx[None, :]], axis=0)
    mod = _mod_call(cvec, w_ada, b_ada).reshape(depth, nb + 1, 6, d)
    cos, sin = _rope_tables(n)
    band = jnp.asarray(_pool_band(), BF16)

    w1 = _permute_proj_columns(w_in[:, :, :PROJ_WIDTH]).astype(BF16)
    wg = w_in[:, :, PROJ_WIDTH:].astype(BF16)
    wb = jnp.concatenate([w_branch[:, :2], _permute_wa_rows(w_branch[:, 2:])], axis=1).astype(BF16)
    wo = w_out.astype(BF16)
    wfi = w_ffn_in.astype(BF16)
    wfo = w_ffn_out.astype(BF16)
    pw = pool_w.astype(BF16)

    n_tiles_all = tok.shape[0] // TM
    n_tiles_lat = nb * n // TM
    for l in range(depth):
        last = l == depth - 1
        lam_init = 0.8 - 0.6 * math.exp(-0.3 * l)
        g = norm_g[l].reshape(4, 1, d)
        qa, ka, va, u, qw, kw, vw = _proj_call(tok, mod[l], g[0], w1[l], cos, sin, nb, n)
        g_sub = da_subln_g[l].reshape(1, DA_V_DIM)
        ya = _da_call(qa, ka, va, da_lambda[l], g_sub, lam_init, nb, n)
        ya_ctx = ya if last else _da_ctx_call(qa, ka, va, da_lambda[l], g_sub, lam_init, nb, n)
        yc = _wa_call(qw, kw, vw, wa_sink[l], nb, n)
        yc_ctx = yc if last else _wa_ctx_call(qw, kw, vw, wa_sink[l], nb, n)
        n_tiles = n_tiles_lat if last else n_tiles_all
        tok = _merge_call(tok, mod[l], g[0], g[1], wg[l], ya, ya_ctx, yc, yc_ctx, u, band, pw[l],
                          pool_scale[l].reshape(1, BRANCH_WIDTH), wb[l], wo[l], nb, n, n_tiles)
        tok = _ffn_call(tok, mod[l], g[2], g[3], wfi[l], wfo[l], nb, n)
    return tok[:nb * n].reshape(nb, n, d)
```

```python
import functools
import math

import numpy as np
import jax
import jax.numpy as jnp
from jax import lax
from jax.experimental import pallas as pl
from jax.experimental.pallas import tpu as pltpu

F32 = jnp.float32
BF16 = jnp.bfloat16

D_MODEL = 1024
CTX_LEN = 256
GRID_W = 64
EPS = 1e-6
ROPE_THETA = 10000.0
BRANCH_WIDTH = D_MODEL // 2
N_BRANCH = 3
DA_HEADS = 4
DA_V_DIM = BRANCH_WIDTH // DA_HEADS
DA_QK_DIM = DA_V_DIM // 2
POOL_WINDOWS = (2, 4, 8, 16)
POOL_GROUPS = 4
POOL_GROUP = BRANCH_WIDTH // POOL_GROUPS
WA_HEAD_DIM = 64
WA_HEADS = BRANCH_WIDTH // WA_HEAD_DIM
WA_KV_HEADS = 2
WA_WINDOW = 128
WA_BLOCK = 128
D_FF = ((8 * D_MODEL + 3 * 256 - 1) // (3 * 256)) * 256
PROJ_WIDTH = 5 * BRANCH_WIDTH + 2 * WA_KV_HEADS * WA_HEAD_DIM

LANES = 128
BF16_SUBLANES = 16
VMEM_LIMIT = 56 * 1024 * 1024

TM = 1024
ROW_BLOCK = 256
FFN_ROWS = 256
TQ = 512
CK = 256
WA_STEP = 4
POOL_SUB = 128
POOL_HALO = BF16_SUBLANES


def _params(n_axes):
    return pltpu.CompilerParams(dimension_semantics=("arbitrary",) * n_axes,
                                vmem_limit_bytes=VMEM_LIMIT)


def _const_spec(shape):
    nd = len(shape)
    return pl.BlockSpec(shape, lambda *_: (0,) * nd, pipeline_mode=pl.Buffered(1))


def _rms(x, g):
    return x * lax.rsqrt(jnp.mean(x * x, axis=-1, keepdims=True) + EPS) * g


def _modulate(x, g, shift, scale):
    return _rms(x, g) * (1.0 + scale) + shift


def _mod_kernel(c_ref, w_ref, b_ref, o_ref):
    c = c_ref[...]
    a = c * (1.0 / (1.0 + jnp.exp(-c)))
    o_ref[0] = jnp.dot(a, w_ref[0], preferred_element_type=F32,
                       precision=lax.Precision.HIGHEST) + b_ref[0]


def _mod_call(cvec, w_ada, b_ada):
    depth, d, width = w_ada.shape
    rows = cvec.shape[0]
    tn = 1536
    return pl.pallas_call(
        _mod_kernel,
        grid=(depth, width // tn),
        in_specs=[pl.BlockSpec((rows, d), lambda l, j: (0, 0)),
                  pl.BlockSpec((1, d, tn), lambda l, j: (l, 0, j)),
                  pl.BlockSpec((1, 1, tn), lambda l, j: (l, 0, j))],
        out_specs=pl.BlockSpec((1, rows, tn), lambda l, j: (l, 0, j)),
        out_shape=jax.ShapeDtypeStruct((depth, rows, width), F32),
        compiler_params=_params(2),
        name="adaln_mod",
    )(cvec, w_ada, b_ada.reshape(depth, 1, width))


def _proj_kernel(x_ref, mod_ref, g_ref, w_ref, c_ref, s_ref,
                 qa_ref, ka_ref, va_ref, u_ref, qw_ref, kw_ref, vw_ref):
    lane = lax.broadcasted_iota(jnp.int32, (ROW_BLOCK, LANES), 1)
    first = (lane & (DA_QK_DIM // 2)) == 0
    bw = BRANCH_WIDTH

    for rb in range(TM // ROW_BLOCK):
        rows = slice(rb * ROW_BLOCK, (rb + 1) * ROW_BLOCK)
        h = _modulate(x_ref[rows, :], g_ref[...], mod_ref[0, 0:1, :], mod_ref[0, 1:2, :]).astype(BF16)
        cos = c_ref[rows, :]
        sin = s_ref[rows, :]

        def proj(lo, width):
            return jnp.dot(h, w_ref[:, lo:lo + width], preferred_element_type=F32)

        def rope(yj):
            return yj * cos + pltpu.roll(yj, LANES // 2, 1) * sin

        def key_store(ref, y):
            for j in range(y.shape[1] // LANES):
                ref[rows, j * LANES:(j + 1) * LANES] = rope(y[:, j * LANES:(j + 1) * LANES]).astype(BF16)

        def query_store(ref, y, scale):
            for j in range(y.shape[1] // LANES):
                r = rope(y[:, j * LANES:(j + 1) * LANES]) * scale
                ref[rows, (2 * j) * LANES:(2 * j + 1) * LANES] = jnp.where(first, r, 0.0).astype(BF16)
                ref[rows, (2 * j + 1) * LANES:(2 * j + 2) * LANES] = jnp.where(first, 0.0, r).astype(BF16)

        ya, yk, yv, yu, yq = (proj(j * bw, bw) for j in range(5))
        kvw = proj(5 * bw, 2 * LANES)
        query_store(qa_ref, ya, DA_QK_DIM ** -0.5 * math.log2(math.e))
        key_store(ka_ref, yk)
        va_ref[:, rows] = yv.T.astype(BF16)
        u_ref[rows, :] = yu.astype(BF16)
        query_store(qw_ref, yq, WA_HEAD_DIM ** -0.5 * math.log2(math.e))
        key_store(kw_ref, kvw[:, :LANES])
        vw_ref[rows, :] = kvw[:, LANES:].astype(BF16)


def _tile_maps(nb, n):
    n_lat = nb * n // TM
    per_seq = n // TM

    def mod_map(t):
        return (jnp.where(t < n_lat, t // per_seq, nb), 0, 0)

    def rope_map(t):
        return (jnp.where(t < n_lat, t % per_seq, per_seq), 0)

    return n_lat, mod_map, rope_map


def _proj_call(tok, mod, g, w1, cos, sin, nb, n):
    nt, d = tok.shape
    _, mod_map, rope_map = _tile_maps(nb, n)
    row = lambda t: (t, 0)
    spec = lambda w: pl.BlockSpec((TM, w), row)
    sd = lambda w: jax.ShapeDtypeStruct((nt, w), BF16)
    widths = (2 * BRANCH_WIDTH, BRANCH_WIDTH, BRANCH_WIDTH, BRANCH_WIDTH, 2 * BRANCH_WIDTH, LANES, LANES)
    return pl.pallas_call(
        _proj_kernel,
        grid=(nt // TM,),
        in_specs=[pl.BlockSpec((TM, d), row),
                  pl.BlockSpec((1, 6, d), mod_map),
                  _const_spec((1, d)),
                  _const_spec(w1.shape),
                  pl.BlockSpec((TM, LANES), rope_map),
                  pl.BlockSpec((TM, LANES), rope_map)],
        out_specs=[pl.BlockSpec((BRANCH_WIDTH, TM), lambda t: (0, t)) if j == 2 else spec(w)
                   for j, w in enumerate(widths)],
        out_shape=[jax.ShapeDtypeStruct((BRANCH_WIDTH, nt), BF16) if j == 2 else sd(w)
                   for j, w in enumerate(widths)],
        compiler_params=_params(1),
        name="in_proj",
    )(tok, mod, g, w1, cos, sin)


def _da_lambda(dl_ref, lam_init):
    lp = dl_ref[...]
    return (jnp.exp(jnp.sum(lp[0:1] * lp[1:2], axis=-1, keepdims=True))
            - jnp.exp(jnp.sum(lp[2:3] * lp[3:4], axis=-1, keepdims=True)) + lam_init)


def _da_scores(k, qs):
    return lax.dot_general(k, qs, (((1,), (1,)), ((), ())), preferred_element_type=F32)


def _da_fold(x, op):
    return op(x.reshape(x.shape[0] // 8, 8, x.shape[1]), axis=0)


def _da_pv(v, e):
    return lax.dot_general(v, e.astype(BF16), (((0,), (0,)), ((), ())), preferred_element_type=F32)


def _da_pvt(vt, e):
    return jnp.dot(vt, e.astype(BF16), preferred_element_type=F32)


def _da_finish(o_ref, g_ref, lam, lam_init, l8, acc):
    tq = o_ref.shape[0]
    l = jnp.sum(l8, axis=0, keepdims=True)
    ot = acc[:, :tq] / l[:, :tq] - lam * (acc[:, tq:] / l[:, tq:])
    o_ref[...] = (_rms(ot.T, g_ref[...]) * (1.0 - lam_init)).astype(BF16)


def _da_kernel(dl_ref, g_ref, q_ref, kl_ref, kc_ref, vl_ref, vc_ref, o_ref, s_scr, m_scr, l_scr, acc_scr, *,
               lam_init):
    t = pl.program_id(0)
    lam = _da_lambda(dl_ref, lam_init)
    n_lat = kl_ref.shape[0]
    qs = jnp.concatenate([q_ref[:, :LANES], q_ref[:, LANES:]], axis=0)

    @pl.when(t == 0)
    def _():
        s_scr[1] = jnp.zeros(s_scr.shape[1:], F32)
        m_scr[1] = jnp.zeros(m_scr.shape[1:], F32)
        l_scr[...] = jnp.ones(l_scr.shape, F32)
        acc_scr[...] = jnp.zeros(acc_scr.shape, F32)

    def step(wr, rd):
        _da_finish(o_ref, g_ref, lam, lam_init, l_scr[...], acc_scr[...])
        m_prev = m_scr[rd]
        m8 = l8 = acc = None
        for c in range(n_lat // CK):
            rows = slice(c * CK, (c + 1) * CK)
            e = jnp.exp2(s_scr[rd, rows, :] - m_prev)
            f = _da_fold(e, jnp.sum)
            p = _da_pvt(vl_ref[:, rows], e)
            l8 = f if l8 is None else l8 + f
            acc = p if acc is None else acc + p
            s = _da_scores(kl_ref[rows, :], qs)
            s_scr[wr, rows, :] = s
            f = _da_fold(s, jnp.max)
            m8 = f if m8 is None else jnp.maximum(m8, f)
        rows = slice(n_lat, n_lat + CTX_LEN)
        sc = _da_scores(kc_ref[...], qs)
        s_scr[wr, rows, :] = sc
        m_scr[wr] = jnp.max(jnp.maximum(m8, _da_fold(sc, jnp.max)), axis=0, keepdims=True)
        e = jnp.exp2(s_scr[rd, rows, :] - m_prev)
        l_scr[...] = l8 + _da_fold(e, jnp.sum)
        acc_scr[...] = acc + _da_pvt(vc_ref[...], e)

    @pl.when(t % 2 == 0)
    def _():
        step(0, 1)

    @pl.when(t % 2 == 1)
    def _():
        step(1, 0)


def _da_call(qa, ka, va, da_lambda, g_sub, lam_init, nb, n):
    n_q = n // TQ
    units = nb * DA_HEADS * n_q
    ctx0 = nb * n // CTX_LEN

    def unit(u):
        u = jnp.clip(u, 0, units - 1)
        return u // (n_q * DA_HEADS), (u // n_q) % DA_HEADS, u % n_q

    def scored(t):
        return unit(t)

    def finished(t):
        return unit(t - 1)

    def written(t):
        return unit(t - 2)

    def tile_map(which):
        def m(t):
            b, h, i = which(t)
            return (b * n_q + i, h)
        return m

    def lat(which):
        return pl.BlockSpec((n, DA_V_DIM), lambda t: which(t)[:2])

    def ctx(which):
        def m(t):
            b, h, _ = which(t)
            return (ctx0 + b, h)
        return pl.BlockSpec((CTX_LEN, DA_V_DIM), m)

    def lat_t(which):
        return pl.BlockSpec((DA_V_DIM, n), lambda t: which(t)[1::-1])

    def ctx_t(which):
        def m(t):
            b, h, _ = which(t)
            return (h, ctx0 + b)
        return pl.BlockSpec((DA_V_DIM, CTX_LEN), m)

    kern = functools.partial(_da_kernel, lam_init=lam_init)
    return pl.pallas_call(
        kern,
        grid=(units + 2,),
        in_specs=[pl.BlockSpec(da_lambda.shape, lambda t: (0, 0)),
                  pl.BlockSpec((1, DA_V_DIM), lambda t: (0, 0)),
                  pl.BlockSpec((TQ, 2 * DA_V_DIM), tile_map(scored)),
                  lat(scored), ctx(scored), lat_t(finished), ctx_t(finished)],
        out_specs=pl.BlockSpec((TQ, DA_V_DIM), tile_map(written)),
        out_shape=jax.ShapeDtypeStruct((nb * n, BRANCH_WIDTH), BF16),
        scratch_shapes=[pltpu.VMEM((2, n + CTX_LEN, 2 * TQ), F32), pltpu.VMEM((2, 1, 2 * TQ), F32),
                        pltpu.VMEM((8, 2 * TQ), F32), pltpu.VMEM((DA_V_DIM, 2 * TQ), F32)],
        compiler_params=_params(1),
        name="diff_attn",
    )(da_lambda, g_sub, qa, ka, ka, va, va)


def _da_ctx_kernel(dl_ref, g_ref, q_ref, k_ref, v_ref, o_ref, *, lam_init):
    lam = _da_lambda(dl_ref, lam_init)
    heads = range(DA_HEADS)
    d = DA_V_DIM
    qs = [jnp.concatenate([q_ref[:, 2 * h * d:(2 * h + 1) * d], q_ref[:, (2 * h + 1) * d:(2 * h + 2) * d]], axis=0)
          for h in heads]
    ss = [_da_scores(k_ref[:, h * d:(h + 1) * d], qs[h]) for h in heads]
    es = [jnp.exp2(s - jnp.max(s, axis=0, keepdims=True)) for s in ss]
    accs = [_da_pvt(v_ref[h * d:(h + 1) * d, :], es[h]) for h in heads]
    for h in heads:
        _da_finish(o_ref.at[:, h * d:(h + 1) * d], g_ref, lam, lam_init, _da_fold(es[h], jnp.sum), accs[h])


def _da_ctx_call(qa, ka, va, da_lambda, g_sub, lam_init, nb, n):
    ctx0 = nb * n // CTX_LEN
    blk = lambda w: pl.BlockSpec((CTX_LEN, w), lambda b: (ctx0 + b, 0))
    kern = functools.partial(_da_ctx_kernel, lam_init=lam_init)
    return pl.pallas_call(
        kern,
        grid=(nb,),
        in_specs=[pl.BlockSpec(da_lambda.shape, lambda b: (0, 0)),
                  pl.BlockSpec((1, DA_V_DIM), lambda b: (0, 0)),
                  blk(2 * BRANCH_WIDTH), blk(BRANCH_WIDTH),
                  pl.BlockSpec((BRANCH_WIDTH, CTX_LEN), lambda b: (0, ctx0 + b))],
        out_specs=pl.BlockSpec((CTX_LEN, BRANCH_WIDTH), lambda b: (b, 0)),
        out_shape=jax.ShapeDtypeStruct((nb * CTX_LEN, BRANCH_WIDTH), BF16),
        compiler_params=_params(1),
        name="diff_attn_ctx",
    )(da_lambda, g_sub, qa, ka, va)


def _wa_pair_queries(q_ref, j, rows=slice(0, WA_BLOCK)):
    return jnp.concatenate([q_ref[rows, (2 * j) * LANES:(2 * j + 1) * LANES],
                            q_ref[rows, (2 * j + 1) * LANES:(2 * j + 2) * LANES]], axis=0)


def _wa_pair_sinks(sink_ref, j):
    lane = lax.broadcasted_iota(jnp.int32, (1, 2 * WA_BLOCK), 1)
    return jnp.where(lane < WA_BLOCK, sink_ref[j], sink_ref[j + WA_HEADS // 2]) * math.log2(math.e)


def _wa_pair_max(s, sk):
    return jnp.maximum(jnp.max(_da_fold(s, jnp.max), axis=0, keepdims=True), sk)


def _wa_pair_finish(o_ref, j, s, m, sk, v, rows=slice(0, WA_BLOCK)):
    e = jnp.exp2(s - m)
    den = jnp.sum(_da_fold(e, jnp.sum), axis=0, keepdims=True) + jnp.exp2(sk - m)
    ot = _da_pv(v, e) / den
    first = lax.broadcasted_iota(jnp.int32, (LANES, WA_BLOCK), 0) < WA_HEAD_DIM
    o_ref[rows, j * LANES:(j + 1) * LANES] = jnp.where(first, ot[:, :WA_BLOCK], ot[:, WA_BLOCK:]).T.astype(BF16)


def _wa_band_bias():
    band = 3 * WA_BLOCK
    key = np.arange(band + CTX_LEN)[:, None]
    qpos = np.arange(2 * WA_BLOCK)[None, :] % WA_BLOCK
    ok = (key >= band) | (np.abs(key - WA_BLOCK - qpos) <= WA_WINDOW)
    return np.where(ok, 0.0, -np.inf).astype(np.float32)


def _wa_kernel(sink_ref, bias_ref, q_ref, kp_ref, kc_ref, kn_ref, kx_ref, vp_ref, vc_ref, vn_ref, vx_ref,
               o_ref, s_scr, m_scr, *, n_steps, units):
    t = pl.program_id(0)
    i = lax.rem(jnp.minimum(t, units - 1), n_steps)

    @pl.when(t == 0)
    def _():
        s_scr[1] = jnp.zeros(s_scr.shape[1:], F32)
        m_scr[1] = jnp.zeros(m_scr.shape[1:], F32)

    def step(wr, rd):
        def gather(p_ref, c_ref, n_ref, x_ref):
            blocks = [p_ref[...]] + [c_ref[b * WA_BLOCK:(b + 1) * WA_BLOCK, :] for b in range(WA_STEP)] + [n_ref[...]]
            return [jnp.concatenate(blocks[b:b + 3] + [x_ref[...]], axis=0) for b in range(WA_STEP)]

        ks = gather(kp_ref, kc_ref, kn_ref, kx_ref)
        vs = gather(vp_ref, vc_ref, vn_ref, vx_ref)
        band = 3 * WA_BLOCK
        out = jnp.full((WA_BLOCK, 2 * WA_BLOCK), -jnp.inf, F32)
        for b in range(WA_STEP):
            rows = slice(b * WA_BLOCK, (b + 1) * WA_BLOCK)
            lo = bias_ref[:WA_BLOCK, :]
            hi = bias_ref[2 * WA_BLOCK:band, :]
            if b == 0:
                lo = jnp.where(i > 0, lo, out)
            if b == WA_STEP - 1:
                hi = jnp.where(i < n_steps - 1, hi, out)
            bias = jnp.concatenate([lo, bias_ref[WA_BLOCK:2 * WA_BLOCK, :], hi, bias_ref[band:, :]], axis=0)
            for j in range(WA_HEADS // 2):
                c0 = (b * (WA_HEADS // 2) + j) * 2 * WA_BLOCK
                cols = slice(c0, c0 + 2 * WA_BLOCK)
                sk = _wa_pair_sinks(sink_ref, j)
                s = _da_scores(ks[b], _wa_pair_queries(q_ref, j, rows)) + bias
                s_scr[wr, :, cols] = s
                m_scr[wr, :, cols] = _wa_pair_max(s, sk)
                _wa_pair_finish(o_ref, j, s_scr[rd, :, cols], m_scr[rd, :, cols], sk, vs[b], rows)

    @pl.when(t % 2 == 0)
    def _():
        step(0, 1)

    @pl.when(t % 2 == 1)
    def _():
        step(1, 0)


def _wa_call(qw, kw, vw, sink, nb, n):
    n_blk = n // WA_BLOCK
    n_steps = n_blk // WA_STEP
    units = nb * n_steps
    ctx0 = nb * n // CTX_LEN

    def scored(t):
        u = jnp.minimum(t, units - 1)
        return u // n_steps, u % n_steps

    def finished(t):
        u = jnp.maximum(t - 1, 0)
        return u // n_steps, u % n_steps

    def q_map(which):
        def m(t):
            b, i = which(t)
            return (b * n_steps + i, 0)
        return m

    def edge(which, delta):
        def m(t):
            b, i = which(t)
            return (b * n_blk + jnp.clip(i * WA_STEP + delta, 0, n_blk - 1), 0)
        return pl.BlockSpec((WA_BLOCK, LANES), m)

    def own(which):
        return pl.BlockSpec((WA_STEP * WA_BLOCK, LANES), q_map(which))

    def ctx(which):
        return pl.BlockSpec((CTX_LEN, LANES), lambda t: (ctx0 + which(t)[0], 0))

    bias = jnp.asarray(_wa_band_bias())
    keys = bias.shape[0]
    width = WA_STEP * WA_HEADS * WA_BLOCK
    kern = functools.partial(_wa_kernel, n_steps=n_steps, units=units)
    return pl.pallas_call(
        kern,
        grid=(units + 1,),
        in_specs=[pl.BlockSpec(memory_space=pltpu.SMEM),
                  _const_spec(bias.shape),
                  pl.BlockSpec((WA_STEP * WA_BLOCK, 2 * BRANCH_WIDTH), q_map(scored)),
                  edge(scored, -1), own(scored), edge(scored, WA_STEP), ctx(scored),
                  edge(finished, -1), own(finished), edge(finished, WA_STEP), ctx(finished)],
        out_specs=pl.BlockSpec((WA_STEP * WA_BLOCK, BRANCH_WIDTH), q_map(finished)),
        out_shape=jax.ShapeDtypeStruct((nb * n, BRANCH_WIDTH), BF16),
        scratch_shapes=[pltpu.VMEM((2, keys, width), F32), pltpu.VMEM((2, 1, width), F32)],
        compiler_params=_params(1),
        name="window_attn",
    )(sink, bias, qw, kw, kw, kw, kw, vw, vw, vw, vw)


def _wa_ctx_kernel(sink_ref, q_ref, k_ref, v_ref, o_ref):
    k = k_ref[...]
    units = [(slice(b * WA_BLOCK, (b + 1) * WA_BLOCK), j)
             for b in range(CTX_LEN // WA_BLOCK) for j in range(WA_HEADS // 2)]
    ss = [_da_scores(k, _wa_pair_queries(q_ref, j, rows)) for rows, j in units]
    for (rows, j), s in zip(units, ss):
        sk = _wa_pair_sinks(sink_ref, j)
        _wa_pair_finish(o_ref, j, s, _wa_pair_max(s, sk), sk, v_ref[...], rows)


def _wa_ctx_call(qw, kw, vw, sink, nb, n):
    ctx0 = nb * n // CTX_LEN
    rows = lambda w: pl.BlockSpec((CTX_LEN, w), lambda b: (ctx0 + b, 0))
    return pl.pallas_call(
        _wa_ctx_kernel,
        grid=(nb,),
        in_specs=[pl.BlockSpec(memory_space=pltpu.SMEM), rows(2 * BRANCH_WIDTH), rows(LANES), rows(LANES)],
        out_specs=pl.BlockSpec((CTX_LEN, BRANCH_WIDTH), lambda b: (b, 0)),
        out_shape=jax.ShapeDtypeStruct((nb * CTX_LEN, BRANCH_WIDTH), BF16),
        compiler_params=_params(1),
        name="window_attn_ctx",
    )(sink, qw, kw, vw)


def _pool_band():
    ext = POOL_SUB + 2 * POOL_HALO
    t = np.arange(POOL_SUB)[:, None]
    d = np.arange(ext)[None, :] - POOL_HALO - t
    return np.stack([((d >= -(w // 2)) & (d < w - w // 2)) for w in POOL_WINDOWS]).astype(np.float32)


def _merge_kernel(x_ref, mod_ref, g0_ref, g1_ref, wg_ref, yal_ref, yax_ref, ycl_ref, ycx_ref,
                  u_ref, up_ref, un_ref, band_ref, pw_ref, ps_ref, wb_ref, wo_ref, o_ref, *, n_lat, n):
    t = pl.program_id(0)
    is_lat = t < n_lat
    ya = jnp.where(is_lat, yal_ref[...], yax_ref[...])
    yc = jnp.where(is_lat, ycl_ref[...], ycx_ref[...])

    u = u_ref[...]
    halo_one = jnp.ones((POOL_HALO, POOL_GROUP), BF16)
    halo_zero = jnp.zeros((POOL_HALO, POOL_GROUP), BF16)
    n_sub = TM // POOL_SUB
    units = [(sb, g) for sb in range(n_sub) for g in range(POOL_GROUPS)]
    ext, ones = [], []
    for sb in range(n_sub):
        r0 = sb * POOL_SUB
        prev = up_ref[...] if sb == 0 else u[r0 - POOL_HALO:r0]
        nxt = un_ref[...] if r0 + POOL_SUB == TM else u[r0 + POOL_SUB:r0 + POOL_SUB + POOL_HALO]
        g_row = t * TM + r0
        seq_row = jnp.where(is_lat, g_row % n, g_row % CTX_LEN)
        seq_len = jnp.where(is_lat, n, CTX_LEN)
        prev_ok = seq_row != 0
        next_ok = seq_row + POOL_SUB != seq_len
        ext.append(jnp.concatenate([jnp.where(prev_ok, prev, jnp.zeros_like(prev)), u[r0:r0 + POOL_SUB],
                                    jnp.where(next_ok, nxt, jnp.zeros_like(nxt))], axis=0))
        ones.append(jnp.concatenate([jnp.where(prev_ok, halo_one, halo_zero),
                                     jnp.ones((POOL_SUB, POOL_GROUP), BF16),
                                     jnp.where(next_ok, halo_one, halo_zero)], axis=0))
    cols = lambda g: slice(g * POOL_GROUP, (g + 1) * POOL_GROUP)
    both = [jnp.dot(band_ref[g], jnp.concatenate([ext[sb][:, cols(g)], ones[sb]], axis=1),
                    preferred_element_type=F32) for sb, g in units]
    pooled = [(bt[:, :POOL_GROUP] / bt[:, POOL_GROUP:]
               - u[sb * POOL_SUB:(sb + 1) * POOL_SUB, cols(g)].astype(F32)).astype(BF16)
              for (sb, g), bt in zip(units, both)]
    outs = [jnp.dot(pq, pw_ref[g], preferred_element_type=F32) * ps_ref[:, cols(g)]
            for (sb, g), pq in zip(units, pooled)]
    yb = jnp.concatenate([jnp.concatenate(outs[sb * POOL_GROUPS:(sb + 1) * POOL_GROUPS], axis=1)
                          for sb in range(n_sub)], axis=0).astype(BF16)

    x = x_ref[...]
    h = _modulate(x, g0_ref[...], mod_ref[0, 0:1, :], mod_ref[0, 1:2, :]).astype(BF16)
    branches = (ya, yb, yc)
    m = None
    for k in range(N_BRANCH):
        gate = jnp.dot(h, wg_ref[:, k * D_MODEL:(k + 1) * D_MODEL], preferred_element_type=F32)
        gate = 1.0 / (1.0 + jnp.exp(-gate))
        term = gate * jnp.dot(branches[k], wb_ref[k], preferred_element_type=F32)
        m = term if m is None else m + term
    mix = jnp.dot(m.astype(BF16), wo_ref[...], preferred_element_type=F32)
    o_ref[...] = x + mod_ref[0, 2:3, :] * _rms(mix, g1_ref[...])


def _merge_call(tok, mod, g0, g1, wg, ya_lat, ya_ctx, yc_lat, yc_ctx, u, band, pool_w, pool_scale, wb, wo,
                nb, n, n_tiles):
    nt, d = tok.shape
    n_lat, mod_map, _ = _tile_maps(nb, n)
    row = lambda t: (t, 0)
    per_tile = TM // POOL_HALO
    last = nt // POOL_HALO - 1
    wide = pl.BlockSpec((TM, BRANCH_WIDTH), row)
    n_ctx = ya_ctx.shape[0] // TM
    lat_rows = pl.BlockSpec((TM, BRANCH_WIDTH), lambda t: (jnp.minimum(t, n_lat - 1), 0))
    ctx_rows = pl.BlockSpec((TM, BRANCH_WIDTH), lambda t: (jnp.clip(t - n_lat, 0, n_ctx - 1), 0))
    kern = functools.partial(_merge_kernel, n_lat=n_lat, n=n)
    return pl.pallas_call(
        kern,
        grid=(n_tiles,),
        in_specs=[pl.BlockSpec((TM, d), row),
                  pl.BlockSpec((1, 6, d), mod_map),
                  _const_spec((1, d)), _const_spec((1, d)),
                  _const_spec(wg.shape),
                  lat_rows, ctx_rows, lat_rows, ctx_rows, wide,
                  pl.BlockSpec((POOL_HALO, BRANCH_WIDTH), lambda t: (jnp.maximum(t * per_tile - 1, 0), 0)),
                  pl.BlockSpec((POOL_HALO, BRANCH_WIDTH), lambda t: (jnp.minimum((t + 1) * per_tile, last), 0)),
                  _const_spec(band.shape), _const_spec(pool_w.shape), _const_spec((1, BRANCH_WIDTH)),
                  _const_spec(wb.shape), _const_spec(wo.shape)],
        out_specs=pl.BlockSpec((TM, d), row),
        out_shape=jax.ShapeDtypeStruct((n_tiles * TM, d), F32),
        compiler_params=_params(1),
        name="pool_merge",
    )(tok, mod, g0, g1, wg, ya_lat, ya_ctx, yc_lat, yc_ctx, u, u, u, band, pool_w, pool_scale, wb, wo)


def _ffn_kernel(x_ref, mod_ref, g2_ref, g3_ref, wi_ref, wo_ref, o_ref):
    step = 256
    for rb in range(TM // FFN_ROWS):
        rows = slice(rb * FFN_ROWS, (rb + 1) * FFN_ROWS)
        x = x_ref[rows, :]
        h = _modulate(x, g2_ref[...], mod_ref[0, 3:4, :], mod_ref[0, 4:5, :]).astype(BF16)
        gates = [jnp.dot(h, wi_ref[:, c * step:(c + 1) * step], preferred_element_type=F32)
                 for c in range(D_FF // step)]
        ups = [jnp.dot(h, wi_ref[:, D_FF + c * step:D_FF + (c + 1) * step], preferred_element_type=F32)
               for c in range(D_FF // step)]
        act = jnp.concatenate([(g * (1.0 / (1.0 + jnp.exp(-g))) * u).astype(BF16) for g, u in zip(gates, ups)],
                              axis=1)
        acc = jnp.dot(act, wo_ref[...], preferred_element_type=F32)
        o_ref[rows, :] = x + mod_ref[0, 5:6, :] * _rms(acc, g3_ref[...])


def _ffn_call(tok, mod, g2, g3, wi, wo, nb, n):
    nt, d = tok.shape
    _, mod_map, _ = _tile_maps(nb, n)
    row = lambda t: (t, 0)
    return pl.pallas_call(
        _ffn_kernel,
        grid=(nt // TM,),
        in_specs=[pl.BlockSpec((TM, d), row),
                  pl.BlockSpec((1, 6, d), mod_map),
                  _const_spec((1, d)), _const_spec((1, d)),
                  _const_spec(wi.shape), _const_spec(wo.shape)],
        out_specs=pl.BlockSpec((TM, d), row),
        out_shape=jax.ShapeDtypeStruct((nt, d), F32),
        compiler_params=_params(1),
        name="swiglu_ffn",
    )(tok, mod, g2, g3, wi, wo)


def _permute_proj_columns(w):
    bw = BRANCH_WIDTH
    half = DA_QK_DIM // 2
    lead = w.shape[:-1]
    n = len(lead)
    keep = tuple(range(n))

    def diff_attn(x):
        x = x.reshape(lead + (DA_HEADS, 2, half, 2))
        return x.transpose(keep + (n, n + 3, n + 1, n + 2)).reshape(lead + (bw,))

    def win_queries(x):
        x = x.reshape(lead + (2, WA_HEADS // 2, half, 2))
        return x.transpose(keep + (n + 1, n + 3, n, n + 2)).reshape(lead + (bw,))

    def win_keys(x):
        x = x.reshape(lead + (WA_KV_HEADS, half, 2))
        return x.transpose(keep + (n + 2, n, n + 1)).reshape(lead + (LANES,))

    return jnp.concatenate([diff_attn(w[..., :bw]), diff_attn(w[..., bw:2 * bw]), w[..., 2 * bw:4 * bw],
                            win_queries(w[..., 4 * bw:5 * bw]), win_keys(w[..., 5 * bw:5 * bw + LANES]),
                            w[..., 5 * bw + LANES:PROJ_WIDTH]], axis=-1)


def _permute_wa_rows(w):
    lead = w.shape[:-2]
    n = len(lead)
    x = w.reshape(lead + (2, WA_HEADS // 2, WA_HEAD_DIM, w.shape[-1]))
    return x.transpose(tuple(range(n)) + (n + 1, n, n + 2, n + 3)).reshape(w.shape)


def _rope_tables(n):
    rows = n // GRID_W
    row = jnp.repeat(jnp.arange(rows, dtype=F32), GRID_W)
    col = jnp.tile(jnp.arange(GRID_W, dtype=F32), rows)
    n_freq = DA_QK_DIM // 4
    inv = ROPE_THETA ** (-jnp.arange(n_freq, dtype=F32) / n_freq)
    ang = jnp.concatenate([row[:, None] * inv, col[:, None] * inv], axis=-1)
    cos = jnp.tile(jnp.cos(ang), (1, 4))
    sin = jnp.tile(jnp.sin(ang), (1, 4)) * jnp.where(jnp.arange(LANES) < 64, -1.0, 1.0)
    cos = jnp.concatenate([cos, jnp.ones((TM, LANES), F32)], axis=0)
    sin = jnp.concatenate([sin, jnp.zeros((TM, LANES), F32)], axis=0)
    return cos, sin


def kernel(x, c, ctx, c_ctx, w_ada, b_ada, norm_g, w_in, da_lambda, da_subln_g,
           pool_w, pool_scale, wa_sink, w_branch, w_out, w_ffn_in, w_ffn_out):
    nb, n, d = x.shape
    depth = w_ada.shape[0]
    assert d == D_MODEL and ctx.shape[1] == CTX_LEN
    assert n % TM == 0 and (nb * CTX_LEN) % TM == 0 and n % CK == 0 and n % GRID_W == 0
    assert n % TQ == 0 and n % (WA_STEP * WA_BLOCK) == 0
    assert TM % POOL_SUB == 0 and CTX_LEN % POOL_SUB == 0 and (nb * n) % CTX_LEN == 0
    assert WA_HEAD_DIM == DA_QK_DIM

    tok = jnp.concatenate([x.reshape(nb * n, d), ctx.reshape(nb * CTX_LEN, d)], axis=0)
    cvec = jnp.concatenate([c, c_ctx[None, :]], axis=0)
    mod = _mod_call(cvec, w_ada, b_ada).reshape(depth, nb + 1, 6, d)
    cos, sin = _rope_tables(n)
    band = jnp.asarray(_pool_band(), BF16)

    w1 = _permute_proj_columns(w_in[:, :, :PROJ_WIDTH]).astype(BF16)
    wg = w_in[:, :, PROJ_WIDTH:].astype(BF16)
    wb = jnp.concatenate([w_branch[:, :2], _permute_wa_rows(w_branch[:, 2:])], axis=1).astype(BF16)
    wo = w_out.astype(BF16)
    wfi = w_ffn_in.astype(BF16)
    wfo = w_ffn_out.astype(BF16)
    pw = pool_w.astype(BF16)

    n_tiles_all = tok.shape[0] // TM
    n_tiles_lat = nb * n // TM
    for l in range(depth):
        last = l == depth - 1
        lam_init = 0.8 - 0.6 * math.exp(-0.3 * l)
        g = norm_g[l].reshape(4, 1, d)
        qa, ka, va, u, qw, kw, vw = _proj_call(tok, mod[l], g[0], w1[l], cos, sin, nb, n)
        g_sub = da_subln_g[l].reshape(1, DA_V_DIM)
        ya = _da_call(qa, ka, va, da_lambda[l], g_sub, lam_init, nb, n)
        ya_ctx = ya if last else _da_ctx_call(qa, ka, va, da_lambda[l], g_sub, lam_init, nb, n)
        yc = _wa_call(qw, kw, vw, wa_sink[l], nb, n)
        yc_ctx = yc if last else _wa_ctx_call(qw, kw, vw, wa_sink[l], nb, n)
        n_tiles = n_tiles_lat if last else n_tiles_all
        tok = _merge_call(tok, mod[l], g[0], g[1], wg[l], ya, ya_ctx, yc, yc_ctx, u, band, pw[l],
                          pool_scale[l].reshape(1, BRANCH_WIDTH), wb[l], wo[l], nb, n, n_tiles)
        tok = _ffn_call(tok, mod[l], g[2], g[3], wfi[l], wfo[l], nb, n)
    return tok[:nb * n].reshape(nb, n, d)
```
